```python
import jax
import jax.numpy as jnp
from jax import lax
import numpy as np

D_MODEL = 1024
BATCH = 4
SEQ = 4096
DEPTH = 2

GRID_W = 64
CTX_LEN = 256
HEAD_DIM = 64
GLA_HEADS = 4
NA_HEADS = 6
SWA_HEADS = 6
SWA_KV_HEADS = 2
GLA_W = GLA_HEADS * HEAD_DIM
NA_W = NA_HEADS * HEAD_DIM
SWA_W = SWA_HEADS * HEAD_DIM
SWA_KV_W = SWA_KV_HEADS * HEAD_DIM
MIX_WIDTH = GLA_W + NA_W + SWA_W
GLA_RANK = 16
GLA_TAU = 16.0
GLA_CHUNK = 64
NA_KH = 8
NA_KW = 16
NA_QW = 16
NA_CB = NA_KW + NA_QW
SWA_WINDOW = 128
SWA_BLOCK = 128
ROPE_THETA = 10000.0
NORM_EPS = 1e-6
D_FF = -(-8 * D_MODEL // (3 * 256)) * 256
IN_SPLITS = (GLA_W, GLA_W, GLA_W, GLA_W, GLA_RANK, GLA_RANK, NA_W, NA_W, NA_W, SWA_W, SWA_KV_W, SWA_KV_W)
IN_WIDTH = sum(IN_SPLITS)

kernel_name = 'hybrid_parallel_head_groups_dit'


def rms_norm(x, g):
    xf = x.astype(jnp.float32)
    y = xf * lax.rsqrt(jnp.mean(xf * xf, axis=-1, keepdims=True) + NORM_EPS)
    return (y * g.astype(jnp.float32)).astype(x.dtype)


def split_cols(p):
    bounds = [int(b) for b in np.cumsum(IN_SPLITS)[:-1]]
    return jnp.split(p, bounds, axis=-1)


def split_heads(a, n_heads):
    return a.reshape(a.shape[:-1] + (n_heads, HEAD_DIM))


def axial_rope(n_tokens):
    t = jnp.arange(n_tokens, dtype=jnp.int32)
    row = (t // GRID_W).astype(jnp.float32)
    col = (t % GRID_W).astype(jnp.float32)
    n_freq = HEAD_DIM // 4
    inv_freq = ROPE_THETA ** (-jnp.arange(n_freq, dtype=jnp.float32) / n_freq)
    ang = jnp.concatenate([row[:, None] * inv_freq, col[:, None] * inv_freq], axis=-1)
    return jnp.cos(ang), jnp.sin(ang)


def apply_rope(a, cos, sin):
    half = HEAD_DIM // 2
    af = a.astype(jnp.float32)
    a1, a2 = af[..., :half], af[..., half:]
    cs = cos[None, :, None, :]
    sn = sin[None, :, None, :]
    return jnp.concatenate([a1 * cs - a2 * sn, a1 * sn + a2 * cs], axis=-1).astype(a.dtype)


def gla_heads(a):
    B, T, _ = a.shape
    return a.reshape(B, T, GLA_HEADS, HEAD_DIM).transpose(0, 2, 1, 3)


def gla_log_decay(z_low, w2, b2):
    z = (z_low @ w2 + b2).astype(jnp.float32)
    return gla_heads(jax.nn.log_sigmoid(z) / GLA_TAU)


def gla_chunked(q, k, v, log_a, s0):
    B, H, T, dk = q.shape
    dv = v.shape[-1]
    n = T // GLA_CHUNK

    def chunks(a):
        return a.astype(jnp.float32).reshape(B, H, n, GLA_CHUNK, a.shape[-1])

    qc, kc, vc, la = chunks(q), chunks(k), chunks(v), chunks(log_a)
    b = jnp.cumsum(la, axis=3)
    b_last = b[:, :, :, -1:, :]
    q_in = qc * jnp.exp(b)
    k_in = kc * jnp.exp(-b)
    k_end = kc * jnp.exp(b_last - b)
    lower = jnp.tril(jnp.ones((GLA_CHUNK, GLA_CHUNK), dtype=bool))
    a_intra = jnp.where(lower, jnp.einsum('bhncd,bhnsd->bhncs', q_in, k_in), 0.0)
    o_intra = jnp.einsum('bhncs,bhnse->bhnce', a_intra, vc)
    chunk_state = jnp.einsum('bhnsd,bhnse->bhnde', k_end, vc)
    chunk_decay = jnp.exp(b_last[:, :, :, 0, :])

    def step(state, inp):
        dec, cs = inp
        return dec[..., None] * state + cs, state

    s_final, s_enter = lax.scan(step, s0, (jnp.moveaxis(chunk_decay, 2, 0), jnp.moveaxis(chunk_state, 2, 0)))
    o_inter = jnp.einsum('bhncd,nbhde->bhnce', q_in, s_enter)
    o = (o_intra + o_inter).reshape(B, H, T, dv).astype(v.dtype)
    return o, s_final


def gla_final_state(k, v, log_a):
    A = jnp.cumsum(log_a.astype(jnp.float32), axis=2)
    w = jnp.exp(A[:, :, -1:, :] - A)
    return jnp.einsum('bhtd,bhte->bhde', k.astype(jnp.float32) * w, v.astype(jnp.float32))


def gla_output(o, g, norm_g):
    B, H, T, dv = o.shape
    o = rms_norm(o.transpose(0, 2, 1, 3), norm_g.reshape(H, dv))
    return o.reshape(B, T, H * dv) * jax.nn.silu(g)


def gla_mixer(parts, parts_c, wa2_f, ba_f, wa2_b, ba_b, norm_g, need_ctx_out):
    q, k, v, g, za_f, za_b = parts
    qc, kc, vc, gc, zac_f, zac_b = parts_c
    scale = HEAD_DIM ** -0.5

    def flip(a):
        return jnp.flip(a, axis=2)

    K_c, V_c = gla_heads(kc), gla_heads(vc)
    la_cf = gla_log_decay(zac_f, wa2_f, ba_f)
    la_cb = gla_log_decay(zac_b, wa2_b, ba_b)
    if need_ctx_out:
        Q_c = gla_heads(qc) * scale
        zeros = jnp.zeros(K_c.shape[:2] + (HEAD_DIM, HEAD_DIM), jnp.float32)
        oc_f, s_f = gla_chunked(Q_c, K_c, V_c, la_cf, zeros)
        oc_b, s_b = gla_chunked(flip(Q_c), flip(K_c), flip(V_c), flip(la_cb), zeros)
        yc = gla_output(oc_f + flip(oc_b), gc, norm_g)
    else:
        s_f = gla_final_state(K_c, V_c, la_cf)
        s_b = gla_final_state(flip(K_c), flip(V_c), flip(la_cb))
        yc = None
    Q, K, V = gla_heads(q) * scale, gla_heads(k), gla_heads(v)
    o_f, _ = gla_chunked(Q, K, V, gla_log_decay(za_f, wa2_f, ba_f), s_f)
    o_b, _ = gla_chunked(flip(Q), flip(K), flip(V), flip(gla_log_decay(za_b, wa2_b, ba_b)), s_b)
    y = gla_output(o_f + flip(o_b), g, norm_g)
    return y, yc


def context_attention(qc, kc, vc, sink):
    B, L, Hq, dh = qc.shape
    Hkv = kc.shape[2]
    G = Hq // Hkv
    s = jnp.einsum('bqhgd,bkhd->bhgqk', qc.reshape(B, L, Hkv, G, dh), kc).astype(jnp.float32) * dh ** -0.5
    if sink is None:
        p = jax.nn.softmax(s, axis=-1)
    else:
        s_sink = jnp.broadcast_to(sink.astype(jnp.float32).reshape(1, Hkv, G, 1, 1), s.shape[:-1] + (1,))
        p = jax.nn.softmax(jnp.concatenate([s, s_sink], axis=-1), axis=-1)[..., :-1]
    o = jnp.einsum('bhgqk,bkhd->bqhgd', p.astype(vc.dtype), vc)
    return o.reshape(B, L, Hq * dh)


def neighborhood_attention(q, k, v, kc, vc, rpb):
    B, T, H, dh = q.shape
    rows = T // GRID_W
    kh = min(NA_KH, rows)
    ncb = GRID_W // NA_QW
    r = np.arange(rows)
    row_idx = np.clip(r - kh // 2, 0, rows - kh)[:, None] + np.arange(kh)[None, :]
    q_cols = np.arange(GRID_W).reshape(ncb, NA_QW)
    win_start = np.clip(q_cols - NA_KW // 2, 0, GRID_W - NA_KW)
    col_idx = np.clip(np.arange(ncb) * NA_QW - NA_KW // 2, 0, GRID_W - NA_CB)[:, None] + np.arange(NA_CB)[None, :]
    key_col = col_idx[:, None, :]
    valid = (key_col >= win_start[:, :, None]) & (key_col < win_start[:, :, None] + NA_KW)
    dr = row_idx - r[:, None] + NA_KH - 1
    dc = np.clip(key_col - q_cols[:, :, None] + NA_KW - 1, 0, 2 * NA_KW - 2)
    bias = rpb.astype(jnp.float32)[:, dr[:, None, None, :, None], dc[None, :, :, None, :]]
    bias = jnp.where(valid[None, None, :, :, None, :], bias, -jnp.inf)

    q_blk = q.reshape(B, rows, ncb, NA_QW, H, dh)
    k_grid = k.reshape(B, rows, GRID_W, H, dh)
    v_grid = v.reshape(B, rows, GRID_W, H, dh)
    ri = row_idx[:, None, :, None]
    ci = col_idx[None, :, None, :]
    k_blk = k_grid[:, ri, ci]
    v_blk = v_grid[:, ri, ci]
    scale = dh ** -0.5
    s_loc = jnp.einsum('brnqhd,brnkchd->bhrnqkc', q_blk, k_blk).astype(jnp.float32) * scale + bias
    n_loc = kh * NA_CB
    s_loc = s_loc.reshape(s_loc.shape[:5] + (n_loc,))
    s_ctx = jnp.einsum('brnqhd,blhd->bhrnql', q_blk, kc).astype(jnp.float32) * scale
    p = jax.nn.softmax(jnp.concatenate([s_loc, s_ctx], axis=-1), axis=-1)
    p_loc = p[..., :n_loc].reshape(p.shape[:5] + (kh, NA_CB)).astype(v.dtype)
    p_ctx = p[..., n_loc:].astype(vc.dtype)
    o = (jnp.einsum('bhrnqkc,brnkchd->brnqhd', p_loc, v_blk)
         + jnp.einsum('bhrnql,blhd->brnqhd', p_ctx, vc))
    return o.reshape(B, T, H * dh)


def sliding_window_attention(q, k, v, kc, vc, sink):
    B, T, Hq, dh = q.shape
    Hkv = k.shape[2]
    G = Hq // Hkv
    bs = SWA_BLOCK
    nb = T // bs
    qb = q.reshape(B, nb, bs, Hkv, G, dh)
    pad = ((0, 0), (bs, bs), (0, 0), (0, 0))
    kp, vp = jnp.pad(k, pad), jnp.pad(v, pad)
    idx = np.arange(nb)[:, None] * bs + np.arange(3 * bs)[None, :]
    kb, vb = kp[:, idx], vp[:, idx]
    kpos = idx - bs
    qpos = np.arange(T).reshape(nb, bs)
    valid = ((np.abs(qpos[:, :, None] - kpos[:, None, :]) <= SWA_WINDOW)
             & (kpos[:, None, :] >= 0) & (kpos[:, None, :] < T))
    scale = dh ** -0.5
    s_loc = jnp.einsum('bnqhgd,bnkhd->bhgnqk', qb, kb).astype(jnp.float32) * scale
    s_loc = jnp.where(valid, s_loc, -jnp.inf)
    s_ctx = jnp.einsum('bnqhgd,blhd->bhgnql', qb, kc).astype(jnp.float32) * scale
    s_sink = jnp.broadcast_to(sink.astype(jnp.float32).reshape(1, Hkv, G, 1, 1, 1), s_loc.shape[:-1] + (1,))
    p = jax.nn.softmax(jnp.concatenate([s_loc, s_ctx, s_sink], axis=-1), axis=-1)
    n_loc = 3 * bs
    L = kc.shape[1]
    p_loc = p[..., :n_loc].astype(v.dtype)
    p_ctx = p[..., n_loc:n_loc + L].astype(vc.dtype)
    o = (jnp.einsum('bhgnqk,bnkhd->bnqhgd', p_loc, vb)
         + jnp.einsum('bhgnql,blhd->bnqhgd', p_ctx, vc))
    return o.reshape(B, T, Hq * dh)


def token_mixers(h, hc, w_in, w_out, wa2_f, ba_f, wa2_b, ba_b, gla_norm, na_rpb, swa_sink, cos, sin, need_ctx_out):
    p = split_cols(h @ w_in)
    pc = split_cols(hc @ w_in)
    y_gla, yc_gla = gla_mixer(p[0:6], pc[0:6], wa2_f, ba_f, wa2_b, ba_b, gla_norm, need_ctx_out)
    nq, nk, nv = (split_heads(a, NA_HEADS) for a in p[6:9])
    nkc, nvc = split_heads(pc[7], NA_HEADS), split_heads(pc[8], NA_HEADS)
    y_na = neighborhood_attention(nq, nk, nv, nkc, nvc, na_rpb)
    sq = apply_rope(split_heads(p[9], SWA_HEADS), cos, sin)
    sk = apply_rope(split_heads(p[10], SWA_KV_HEADS), cos, sin)
    sv = split_heads(p[11], SWA_KV_HEADS)
    skc, svc = split_heads(pc[10], SWA_KV_HEADS), split_heads(pc[11], SWA_KV_HEADS)
    y_swa = sliding_window_attention(sq, sk, sv, skc, svc, swa_sink)
    y = jnp.concatenate([y_gla, y_na, y_swa], axis=-1) @ w_out
    if not need_ctx_out:
        return y, None
    yc_na = context_attention(split_heads(pc[6], NA_HEADS), nkc, nvc, None)
    yc_swa = context_attention(split_heads(pc[9], SWA_HEADS), skc, svc, swa_sink)
    yc = jnp.concatenate([yc_gla, yc_na, yc_swa], axis=-1) @ w_out
    return y, yc


def swiglu(h, w_gate, w_up, w_down):
    return (jax.nn.silu(h @ w_gate) * (h @ w_up)) @ w_down


def setup_inputs(seed: int = 0) -> dict:
    key = jax.random.key(seed)
    ks = jax.random.split(key, 24)
    f32 = jnp.float32
    D = D_MODEL

    def nrm(k, shape, s):
        return jax.random.normal(k, shape, f32) * s

    return {
        'x': nrm(ks[0], (BATCH, SEQ, D), 1.0),
        'c': nrm(ks[1], (BATCH, D), 1.0),
        'ctx': nrm(ks[2], (BATCH, CTX_LEN, D), 1.0),
        'c_ctx': nrm(ks[3], (D,), 1.0),
        'w_mod': nrm(ks[4], (DEPTH, D, 6 * D), 0.5 * D ** -0.5),
        'b_mod': nrm(ks[5], (DEPTH, 6 * D), 0.02),
        'norm_mix': 1.0 + nrm(ks[6], (DEPTH, D), 0.05),
        'norm_ffn': 1.0 + nrm(ks[7], (DEPTH, D), 0.05),
        'w_in': nrm(ks[8], (DEPTH, D, IN_WIDTH), D ** -0.5),
        'gla_wa2_f': nrm(ks[9], (DEPTH, GLA_RANK, GLA_W), GLA_RANK ** -0.5),
        'gla_ba_f': nrm(ks[10], (DEPTH, GLA_W), 0.5),
        'gla_wa2_b': nrm(ks[11], (DEPTH, GLA_RANK, GLA_W), GLA_RANK ** -0.5),
        'gla_ba_b': nrm(ks[12], (DEPTH, GLA_W), 0.5),
        'gla_norm': 1.0 + nrm(ks[13], (DEPTH, GLA_W), 0.05),
        'na_rpb': nrm(ks[14], (DEPTH, NA_HEADS, 2 * NA_KH - 1, 2 * NA_KW - 1), 0.1),
        'swa_sink': nrm(ks[15], (DEPTH, SWA_HEADS), 0.5),
        'w_out': nrm(ks[16], (DEPTH, MIX_WIDTH, D), MIX_WIDTH ** -0.5),
        'w_gate': nrm(ks[17], (DEPTH, D, D_FF), D ** -0.5),
        'w_up': nrm(ks[18], (DEPTH, D, D_FF), D ** -0.5),
        'w_down': nrm(ks[19], (DEPTH, D_FF, D), D_FF ** -0.5),
        'final_norm': 1.0 + nrm(ks[20], (D,), 0.05),
    }


def reference(x, c, ctx, c_ctx, w_mod, b_mod, norm_mix, norm_ffn, w_in, gla_wa2_f, gla_ba_f, gla_wa2_b, gla_ba_b,
              gla_norm, na_rpb, swa_sink, w_out, w_gate, w_up, w_down, final_norm):
    T = x.shape[1]
    cos, sin = axial_rope(T)
    s_lat = jax.nn.silu(c)
    s_ctx = jax.nn.silu(c_ctx)
    xc = ctx
    for i in range(DEPTH):
        need_ctx_out = i < DEPTH - 1
        mod = (s_lat @ w_mod[i] + b_mod[i])[:, None, :]
        mod_c = s_ctx @ w_mod[i] + b_mod[i]
        sh1, sc1, g1, sh2, sc2, g2 = jnp.split(mod, 6, axis=-1)
        sh1c, sc1c, g1c, sh2c, sc2c, g2c = jnp.split(mod_c, 6, axis=-1)
        h = rms_norm(x, norm_mix[i]) * (1.0 + sc1) + sh1
        hc = rms_norm(xc, norm_mix[i]) * (1.0 + sc1c) + sh1c
        y, yc = token_mixers(h, hc, w_in[i], w_out[i], gla_wa2_f[i], gla_ba_f[i], gla_wa2_b[i], gla_ba_b[i],
                             gla_norm[i], na_rpb[i], swa_sink[i], cos, sin, need_ctx_out)
        x = x + g1 * y
        h = rms_norm(x, norm_ffn[i]) * (1.0 + sc2) + sh2
        x = x + g2 * swiglu(h, w_gate[i], w_up[i], w_down[i])
        if need_ctx_out:
            xc = xc + g1c * yc
            hc = rms_norm(xc, norm_ffn[i]) * (1.0 + sc2c) + sh2c
            xc = xc + g2c * swiglu(hc, w_gate[i], w_up[i], w_down[i])
    return rms_norm(x, final_norm)
```

```python
import functools

import jax
import jax.numpy as jnp
import numpy as np
from jax import lax
from jax.experimental import pallas as pl
from jax.experimental.pallas import tpu as pltpu

F32 = jnp.float32
BF16 = jnp.bfloat16

D_MODEL = 1024
HEAD_DIM = 64
GRID_W = 64
GLA_HEADS = 4
NA_HEADS = 6
SWA_HEADS = 6
SWA_KV_HEADS = 2
GLA_W = GLA_HEADS * HEAD_DIM
NA_W = NA_HEADS * HEAD_DIM
SWA_W = SWA_HEADS * HEAD_DIM
SWA_KV_W = SWA_KV_HEADS * HEAD_DIM
GLA_RANK = 16
GLA_TAU = 16.0
GLA_CHUNK = 64
NA_KH = 8
NA_KW = 16
SWA_WINDOW = 128
ROPE_THETA = 10000.0
NORM_EPS = 1e-6

LANES = 128
ROW_BLOCK = 256
PAIR_W = 2 * HEAD_DIM
NA_CLASSES = 8
VMEM_LIMIT = 56 * 1024 * 1024

C_GLA = 0
C_ZA = C_GLA + 4 * GLA_W
C_NA = C_ZA + LANES
C_SQ = C_NA + 3 * NA_W
C_SK = C_SQ + SWA_W
C_SV = C_SK + SWA_KV_W
IN_PERM_W = C_SV + SWA_KV_W


def _dot(a, b):
    return jnp.dot(a, b, preferred_element_type=F32)


def _dot_nt(a, b):
    return lax.dot_general(a, b, (((1,), (1,)), ((), ())), preferred_element_type=F32)


def _dot_tn(a, b):
    return lax.dot_general(a, b, (((0,), (0,)), ((), ())), preferred_element_type=F32)


def _idiv(a, n):
    assert n & (n - 1) == 0
    return a >> (n.bit_length() - 1)


def _imod(a, n):
    assert n & (n - 1) == 0
    return a & (n - 1)


def _split2(a):
    hi = a.astype(BF16)
    lo = (a - hi.astype(F32)).astype(BF16)
    return hi, lo


def _split3(a):
    hi = a.astype(BF16)
    r = a - hi.astype(F32)
    mid = r.astype(BF16)
    lo = (r - mid.astype(F32)).astype(BF16)
    return hi, mid, lo


def _dot_split(a, b):
    ah, al = _split2(a)
    bh, bl = _split2(b)
    return _dot(ah, bh) + _dot(al, bh) + _dot(ah, bl)


def _silu(a):
    return a * jax.nn.sigmoid(a)


def _rms(x):
    return x * lax.rsqrt(jnp.mean(x * x, axis=-1, keepdims=True) + NORM_EPS)


def _params(n_parallel, n_arbitrary=0):
    return pltpu.CompilerParams(
        dimension_semantics=("parallel",) * n_parallel + ("arbitrary",) * n_arbitrary,
        vmem_limit_bytes=VMEM_LIMIT)


def _mod_kernel(c_ref, w_ref, b_ref, o_ref):
    o_ref[...] = _dot_split(_silu(c_ref[...]), w_ref[...]) + b_ref[...]


def _modulation(cc, w_mod, b_mod):
    depth, d, n = w_mod.shape
    tn = 1536
    return pl.pallas_call(
        _mod_kernel,
        grid=(depth, n // tn),
        in_specs=[
            pl.BlockSpec((8, d), lambda i, k: (0, 0)),
            pl.BlockSpec((None, d, tn), lambda i, k: (i, 0, k)),
            pl.BlockSpec((None, 1, tn), lambda i, k: (i, 0, k)),
        ],
        out_specs=pl.BlockSpec((None, 8, tn), lambda i, k: (i, 0, k)),
        out_shape=jax.ShapeDtypeStruct((depth, 8, n), F32),
        compiler_params=_params(2),
    )(cc, w_mod, b_mod.reshape(depth, 1, n))


def _mod_spec(chunk, batch, ctx_first):
    if ctx_first:
        return pl.BlockSpec((None, 1, D_MODEL), lambda b, j: (jnp.where(j == 0, batch, b), 0, chunk))
    return pl.BlockSpec((None, 1, D_MODEL), lambda b, j: (b, 0, chunk))


def _in_proj_kernel(x_ref, sh_ref, sc_ref, nw_ref, w_ref, cos_ref, sin_ref,
                    gla_ref, za_ref, na_ref, sq_ref, sk_ref, sv_ref):
    h = _rms(x_ref[...]) * nw_ref[...]
    h = (h * (1.0 + sc_ref[...]) + sh_ref[...]).astype(BF16)
    gla_ref[...] = _dot(h, w_ref[:, C_GLA:C_ZA])
    za_ref[...] = _dot(h, w_ref[:, C_ZA:C_NA])
    na_ref[...] = _dot(h, w_ref[:, C_NA:C_SQ]).astype(BF16)
    sv_ref[...] = _dot(h, w_ref[:, C_SV:IN_PERM_W]).astype(BF16)
    qk = _dot(h, w_ref[:, C_SQ:C_SV])
    cos = cos_ref[...]
    sin = sin_ref[...]
    lane = lax.broadcasted_iota(jnp.int32, cos.shape, 1)
    first_half = _imod(lane, HEAD_DIM) < (HEAD_DIM // 2)
    n_q = SWA_W // LANES
    for cb in range(n_q + 1):
        a = qk[:, cb * LANES:(cb + 1) * LANES]
        partner = jnp.where(first_half, pltpu.roll(a, LANES - HEAD_DIM // 2, 1), pltpu.roll(a, HEAD_DIM // 2, 1))
        r = (a * cos + partner * sin).astype(BF16)
        if cb < n_q:
            sq_ref[:, cb * LANES:(cb + 1) * LANES] = r
        else:
            sk_ref[...] = r


def _in_proj(xs, mods, norm_w, w_perm, cos_t, sin_t):
    batch, s, d = xs.shape
    tm = ROW_BLOCK
    row = lambda w: pl.BlockSpec((None, tm, w), lambda b, j: (b, j, 0))
    const = lambda shp: pl.BlockSpec(shp, lambda b, j: (0,) * len(shp))
    tab = pl.BlockSpec((tm, LANES), lambda b, j: (j, 0))
    return pl.pallas_call(
        _in_proj_kernel,
        grid=(batch, s // tm),
        in_specs=[row(d), _mod_spec(0, batch, True), _mod_spec(1, batch, True), const((1, d)),
                  const((d, IN_PERM_W)), tab, tab],
        out_specs=[row(4 * GLA_W), row(LANES), row(3 * NA_W), row(SWA_W), row(SWA_KV_W), row(SWA_KV_W)],
        out_shape=[
            jax.ShapeDtypeStruct((batch, s, 4 * GLA_W), F32),
            jax.ShapeDtypeStruct((batch, s, LANES), F32),
            jax.ShapeDtypeStruct((batch, s, 3 * NA_W), BF16),
            jax.ShapeDtypeStruct((batch, s, SWA_W), BF16),
            jax.ShapeDtypeStruct((batch, s, SWA_KV_W), BF16),
            jax.ShapeDtypeStruct((batch, s, SWA_KV_W), BF16),
        ],
        compiler_params=_params(2),
    )(xs, mods, mods, norm_w, w_perm, cos_t, sin_t)


def _gla_direction(q_ref, k_ref, v_ref, z_ref, wa_ref, ba_ref, o_ref, st_ref, reverse):
    c = GLA_CHUNK
    w = GLA_W
    n_chunks = ROW_BLOCK // c
    r64 = lax.broadcasted_iota(jnp.int32, (c, c), 0)
    c64 = lax.broadcasted_iota(jnp.int32, (c, c), 1)
    cum_m = (r64 <= c64 if reverse else r64 >= c64).astype(BF16)
    row_h = _idiv(lax.broadcasted_iota(jnp.int32, (GLA_HEADS * c, w), 0), c)
    col_h = _idiv(lax.broadcasted_iota(jnp.int32, (GLA_HEADS * c, w), 1), c)
    same_head = row_h == col_h
    ar = _imod(lax.broadcasted_iota(jnp.int32, (GLA_HEADS * c, c), 0), c)
    ac = lax.broadcasted_iota(jnp.int32, (GLA_HEADS * c, c), 1)
    causal = ar <= ac if reverse else ar >= ac
    out_head = _idiv(lax.broadcasted_iota(jnp.int32, (c, w), 1), c)

    z = _dot_split(z_ref[...], wa_ref[...]) + ba_ref[...]
    log_a = (jnp.minimum(z, 0.0) - jnp.log1p(jnp.exp(-jnp.abs(z)))) * (1.0 / GLA_TAU)
    st = st_ref[...]
    for ci in (range(n_chunks - 1, -1, -1) if reverse else range(n_chunks)):
        rows = slice(ci * c, (ci + 1) * c)
        hi, mid, lo = _split3(log_a[rows])
        b = _dot(cum_m, hi) + _dot(cum_m, mid) + _dot(cum_m, lo)
        b_tot = b[0:1] if reverse else b[c - 1:c]
        k = k_ref[rows, :]
        v = v_ref[rows, :].astype(BF16)
        q_in = (q_ref[rows, :] * jnp.exp(b)).astype(BF16)
        k_in = (k * jnp.exp(-b)).astype(BF16)
        k_end = (k * jnp.exp(b_tot - b)).astype(BF16)
        qx = jnp.where(same_head, jnp.concatenate([q_in] * GLA_HEADS, axis=0), jnp.zeros((), BF16))
        a = jnp.where(causal, _dot_nt(qx, k_in), 0.0).astype(BF16)
        r = _dot(a, v)
        o = _dot_nt(q_in, st.astype(BF16))
        for h in range(GLA_HEADS):
            o = o + jnp.where(out_head == h, r[h * c:(h + 1) * c], 0.0)
        o_ref[rows, :] = o
        st = st * jnp.exp(b_tot) + jnp.where(same_head, _dot_tn(v, k_end), 0.0)
    st_ref[...] = st


def _gla_kernel(qf_ref, kf_ref, vf_ref, zf_ref, qb_ref, kb_ref, vb_ref, zb_ref,
                waf_ref, wab_ref, baf_ref, bab_ref, of_ref, ob_ref, stf_ref, stb_ref):
    @pl.when(pl.program_id(1) == 0)
    def _():
        stf_ref[...] = jnp.zeros_like(stf_ref)
        stb_ref[...] = jnp.zeros_like(stb_ref)

    _gla_direction(qf_ref, kf_ref, vf_ref, zf_ref, waf_ref, baf_ref, of_ref, stf_ref, False)
    _gla_direction(qb_ref, kb_ref, vb_ref, zb_ref, wab_ref, bab_ref, ob_ref, stb_ref, True)


def _gla(gla_in, za, wa_f, wa_b, ba_f, ba_b):
    batch, s, _ = gla_in.shape
    tm = ROW_BLOCK
    nblk = s // tm
    fwd = lambda j: j
    bwd = lambda j: jnp.where(j == 0, 0, nblk - j)

    def col(order, cb):
        return pl.BlockSpec((None, tm, GLA_W), lambda b, j: (b, order(j), cb))

    def zspec(order):
        return pl.BlockSpec((None, tm, LANES), lambda b, j: (b, order(j), 0))

    const = lambda shp: pl.BlockSpec(shp, lambda b, j: (0,) * len(shp))
    return pl.pallas_call(
        _gla_kernel,
        grid=(batch, nblk),
        in_specs=[col(fwd, 0), col(fwd, 1), col(fwd, 2), zspec(fwd),
                  col(bwd, 0), col(bwd, 1), col(bwd, 2), zspec(bwd),
                  const((LANES, GLA_W)), const((LANES, GLA_W)), const((1, GLA_W)), const((1, GLA_W))],
        out_specs=[col(fwd, 0), col(bwd, 0)],
        out_shape=[jax.ShapeDtypeStruct((batch, s, GLA_W), F32)] * 2,
        scratch_shapes=[pltpu.VMEM((GLA_W, GLA_W), F32)] * 2,
        compiler_params=_params(1, 1),
    )(gla_in, gla_in, gla_in, za, gla_in, gla_in, gla_in, za, wa_f, wa_b, ba_f, ba_b)


def _stack_pair(q):
    left = lax.broadcasted_iota(jnp.int32, q.shape, 1) < HEAD_DIM
    zero = jnp.zeros((), q.dtype)
    return jnp.concatenate([jnp.where(left, q, zero), jnp.where(left, zero, q)], axis=0)


def _unstack_pair(o):
    n = o.shape[0] // 2
    left = lax.broadcasted_iota(jnp.int32, (n, o.shape[1]), 1) < HEAD_DIM
    return jnp.where(left, o[:n], o[n:])


def _softmax_pv(scores, values, extra_logit=None):
    m = functools.reduce(jnp.maximum, [jnp.max(s, axis=-1, keepdims=True) for s in scores])
    if extra_logit is not None:
        m = jnp.maximum(m, extra_logit)
    ps = [jnp.exp(s - m) for s in scores]
    denom = functools.reduce(jnp.add, [jnp.sum(p, axis=-1, keepdims=True) for p in ps])
    if extra_logit is not None:
        denom = denom + jnp.exp(extra_logit - m)
    o = functools.reduce(jnp.add, [_dot(p.astype(BF16), v) for p, v in zip(ps, values)])
    return o / denom


def _na_kernel(q_ref, k_ref, v_ref, bias_ref, o_ref, *, ctx_len, grid_rows):
    j = pl.program_id(1)
    n_pairs = NA_W // PAIR_W
    rows_per_block = ROW_BLOCK // GRID_W
    n_loc = NA_KH * GRID_W

    @pl.when(j == 0)
    def _():
        for p in range(n_pairs):
            cols = slice(p * PAIR_W, (p + 1) * PAIR_W)
            qx = _stack_pair(q_ref[:, cols])
            s_ctx = _dot_nt(qx, k_ref[0:ctx_len, cols])
            o = _softmax_pv([s_ctx], [v_ref[0:ctx_len, cols]])
            o_ref[:, cols] = _unstack_pair(o).astype(o_ref.dtype)

    @pl.when(j > 0)
    def _():
        for rr in range(rows_per_block):
            r = (j - 1) * rows_per_block + rr
            cls = jnp.where(r < NA_KH // 2, r,
                            jnp.where(r > grid_rows - NA_KH // 2, r - (grid_rows - NA_CLASSES), NA_KH // 2))
            k0 = pl.multiple_of(ctx_len + jnp.clip(r - NA_KH // 2, 0, grid_rows - NA_KH) * GRID_W, GRID_W)
            for p in range(n_pairs):
                cols = slice(p * PAIR_W, (p + 1) * PAIR_W)
                qx = _stack_pair(q_ref[rr * GRID_W:(rr + 1) * GRID_W, cols])
                s_loc = _dot_nt(qx, k_ref[pl.ds(k0, n_loc), cols]) + bias_ref[cls, p]
                s_ctx = _dot_nt(qx, k_ref[0:ctx_len, cols])
                o = _softmax_pv([s_loc, s_ctx], [v_ref[pl.ds(k0, n_loc), cols], v_ref[0:ctx_len, cols]])
                o_ref[rr * GRID_W:(rr + 1) * GRID_W, cols] = _unstack_pair(o).astype(o_ref.dtype)


def _na(na_qkv, bias, ctx_len):
    batch, s, _ = na_qkv.shape
    tm = ROW_BLOCK
    grid_rows = (s - ctx_len) // GRID_W
    full = lambda cb: pl.BlockSpec((None, s, NA_W), lambda b, j: (b, 0, cb))
    return pl.pallas_call(
        functools.partial(_na_kernel, ctx_len=ctx_len, grid_rows=grid_rows),
        grid=(batch, s // tm),
        in_specs=[pl.BlockSpec((None, tm, NA_W), lambda b, j: (b, j, 0)), full(1), full(2),
                  pl.BlockSpec(bias.shape, lambda b, j: (0, 0, 0, 0))],
        out_specs=pl.BlockSpec((None, tm, NA_W), lambda b, j: (b, j, 0)),
        out_shape=jax.ShapeDtypeStruct((batch, s, NA_W), BF16),
        compiler_params=_params(2),
    )(na_qkv, na_qkv, na_qkv, bias)


def _swa_kernel(sink_ref, q_ref, k_ref, v_ref, o_ref, *, ctx_len, seq_len):
    j = pl.program_id(1)
    tm = ROW_BLOCK
    n_pairs = SWA_W // PAIR_W
    n_loc = tm + 2 * SWA_WINDOW
    top = lax.broadcasted_iota(jnp.int32, (2 * tm, 1), 0) < tm
    k_ctx = k_ref[0:ctx_len, :]
    v_ctx = v_ref[0:ctx_len, :]

    def sink_col(p):
        return jnp.where(top, sink_ref[p], sink_ref[p + n_pairs])

    @pl.when(j == 0)
    def _():
        for p in range(n_pairs):
            cols = slice(p * PAIR_W, (p + 1) * PAIR_W)
            qx = _stack_pair(q_ref[:, cols])
            o = _softmax_pv([_dot_nt(qx, k_ctx)], [v_ctx], sink_col(p))
            o_ref[:, cols] = _unstack_pair(o).astype(o_ref.dtype)

    @pl.when(j > 0)
    def _():
        q0 = (j - 1) * tm
        k0 = jnp.clip(q0 - SWA_WINDOW, 0, seq_len - n_loc)
        qpos = q0 + _imod(lax.broadcasted_iota(jnp.int32, (2 * tm, n_loc), 0), tm)
        kpos = k0 + lax.broadcasted_iota(jnp.int32, (2 * tm, n_loc), 1)
        in_window = jnp.abs(qpos - kpos) <= SWA_WINDOW
        start = pl.multiple_of(ctx_len + k0, SWA_WINDOW)
        k_loc = k_ref[pl.ds(start, n_loc), :]
        v_loc = v_ref[pl.ds(start, n_loc), :]
        for p in range(n_pairs):
            cols = slice(p * PAIR_W, (p + 1) * PAIR_W)
            qx = _stack_pair(q_ref[:, cols])
            s_loc = jnp.where(in_window, _dot_nt(qx, k_loc), -jnp.inf)
            o = _softmax_pv([s_loc, _dot_nt(qx, k_ctx)], [v_loc, v_ctx], sink_col(p))
            o_ref[:, cols] = _unstack_pair(o).astype(o_ref.dtype)


def _swa(sink, sq, sk, sv, ctx_len):
    batch, s, _ = sq.shape
    tm = ROW_BLOCK
    full = pl.BlockSpec((None, s, SWA_KV_W), lambda b, j: (b, 0, 0))
    return pl.pallas_call(
        functools.partial(_swa_kernel, ctx_len=ctx_len, seq_len=s - ctx_len),
        grid=(batch, s // tm),
        in_specs=[pl.BlockSpec(memory_space=pltpu.SMEM),
                  pl.BlockSpec((None, tm, SWA_W), lambda b, j: (b, j, 0)), full, full],
        out_specs=pl.BlockSpec((None, tm, SWA_W), lambda b, j: (b, j, 0)),
        out_shape=jax.ShapeDtypeStruct((batch, s, SWA_W), BF16),
        compiler_params=_params(2),
    )(sink, sq, sk, sv)


def _out_proj_kernel(of_ref, ob_ref, gate_ref, na_ref, swa_ref, x_ref, g1_ref, sh2_ref, sc2_ref,
                     nffn_ref, gn_ref, avg_ref, wg_ref, wn_ref, ws_ref, x1_ref, h2_ref):
    o = of_ref[...] + ob_ref[...]
    hi, lo = _split2(o * o)
    avg = avg_ref[...]
    ms = _dot(hi, avg) + _dot(lo, avg)
    y = o * lax.rsqrt(ms + NORM_EPS) * gn_ref[...] * _silu(gate_ref[...])
    acc = _dot(y.astype(BF16), wg_ref[...]) + _dot(na_ref[...], wn_ref[...]) + _dot(swa_ref[...], ws_ref[...])
    x1 = x_ref[...] + g1_ref[...] * acc
    x1_ref[...] = x1
    h2 = _rms(x1) * nffn_ref[...]
    h2_ref[...] = (h2 * (1.0 + sc2_ref[...]) + sh2_ref[...]).astype(BF16)


def _out_proj(o_f, o_b, gla_in, y_na, y_swa, xs, mods, norm_ffn, gla_norm, avg, w_g, w_n, w_s, skip_ctx):
    batch, s, d = xs.shape
    tm = ROW_BLOCK
    off = 1 if skip_ctx else 0
    nblk = s // tm - off
    row = lambda w, cb=0: pl.BlockSpec((None, tm, w), lambda b, j: (b, j + off, cb))
    out = lambda w: pl.BlockSpec((None, tm, w), lambda b, j: (b, j, 0))
    const = lambda shp: pl.BlockSpec(shp, lambda b, j: (0,) * len(shp))
    ms = lambda chunk: _mod_spec(chunk, batch, not skip_ctx)
    return pl.pallas_call(
        _out_proj_kernel,
        grid=(batch, nblk),
        in_specs=[row(GLA_W), row(GLA_W), row(GLA_W, 3), row(NA_W), row(SWA_W), row(d),
                  ms(2), ms(3), ms(4), const((1, d)), const((1, GLA_W)), const((GLA_W, GLA_W)),
                  const((GLA_W, d)), const((NA_W, d)), const((SWA_W, d))],
        out_specs=[out(d), out(d)],
        out_shape=[jax.ShapeDtypeStruct((batch, nblk * tm, d), F32),
                   jax.ShapeDtypeStruct((batch, nblk * tm, d), BF16)],
        compiler_params=_params(2),
    )(o_f, o_b, gla_in, y_na, y_swa, xs, mods, mods, mods, norm_ffn, gla_norm, avg, w_g, w_n, w_s)


def _ffn_kernel(h_ref, x_ref, g2_ref, wg_ref, wu_ref, wd_ref, fn_ref, o_ref, *, ff_chunk, final):
    h = h_ref[...]
    d_ff = wg_ref.shape[1]
    acc = jnp.zeros(x_ref.shape, F32)
    for c0 in range(0, d_ff, ff_chunk):
        g = _dot(h, wg_ref[:, c0:c0 + ff_chunk])
        u = _dot(h, wu_ref[:, c0:c0 + ff_chunk])
        acc = acc + _dot((_silu(g) * u).astype(BF16), wd_ref[c0:c0 + ff_chunk, :])
    x2 = x_ref[...] + g2_ref[...] * acc
    if final:
        x2 = _rms(x2) * fn_ref[...]
    o_ref[...] = x2


def _ffn(h2, x1, mods, w_gate, w_up, w_down, final_norm, latent_only, final):
    batch, s, d = x1.shape
    d_ff = w_gate.shape[1]
    tm = ROW_BLOCK
    row = lambda: pl.BlockSpec((None, tm, d), lambda b, j: (b, j, 0))
    const = lambda shp: pl.BlockSpec(shp, lambda b, j: (0,) * len(shp))
    return pl.pallas_call(
        functools.partial(_ffn_kernel, ff_chunk=d_ff // 2, final=final),
        grid=(batch, s // tm),
        in_specs=[row(), row(), _mod_spec(5, batch, not latent_only),
                  const((d, d_ff)), const((d, d_ff)), const((d_ff, d)), const((1, d))],
        out_specs=row(),
        out_shape=jax.ShapeDtypeStruct((batch, s, d), F32),
        compiler_params=_params(2),
    )(h2, x1, mods, w_gate, w_up, w_down, final_norm)


def _rope_tables(ctx_len, seq_len):
    t = jnp.arange(seq_len, dtype=jnp.int32)
    row = (t // GRID_W).astype(F32)
    col = (t % GRID_W).astype(F32)
    n_freq = HEAD_DIM // 4
    inv_freq = ROPE_THETA ** (-jnp.arange(n_freq, dtype=F32) / n_freq)
    ang = jnp.concatenate([row[:, None] * inv_freq, col[:, None] * inv_freq], axis=-1)
    cos, sin = jnp.cos(ang), jnp.sin(ang)
    cos_h = jnp.concatenate([cos, cos], axis=-1)
    sin_h = jnp.concatenate([-sin, sin], axis=-1)
    cos_t = jnp.concatenate([jnp.ones((ctx_len, HEAD_DIM), F32), cos_h], axis=0)
    sin_t = jnp.concatenate([jnp.zeros((ctx_len, HEAD_DIM), F32), sin_h], axis=0)
    return jnp.tile(cos_t, (1, 2)), jnp.tile(sin_t, (1, 2))


def _swa_pair_perm():
    n_pairs = SWA_W // PAIR_W
    heads = [h for p in range(n_pairs) for h in (p, p + n_pairs)]
    return np.concatenate([np.arange(h * HEAD_DIM, (h + 1) * HEAD_DIM) for h in heads])


def _permute_w_in(w):
    scale = HEAD_DIM ** -0.5
    o = np.cumsum([0, GLA_W, GLA_W, GLA_W, GLA_W, GLA_RANK, GLA_RANK, NA_W, NA_W, NA_W, SWA_W, SWA_KV_W, SWA_KV_W])
    d = w.shape[0]
    sq = w[:, o[9]:o[10]][:, _swa_pair_perm()] * scale
    parts = [w[:, o[0]:o[1]] * scale, w[:, o[1]:o[4]],
             w[:, o[4]:o[6]], jnp.zeros((d, LANES - 2 * GLA_RANK), w.dtype),
             w[:, o[6]:o[7]] * scale, w[:, o[7]:o[9]],
             sq, w[:, o[10]:o[12]]]
    return jnp.concatenate(parts, axis=1).astype(BF16)


def _na_bias_table(rpb, grid_rows):
    kh = NA_KH
    class_rows = np.array([0, 1, 2, 3, kh // 2, grid_rows - 3, grid_rows - 2, grid_rows - 1])
    key_rows = np.clip(class_rows - kh // 2, 0, grid_rows - kh)[:, None] + np.arange(kh)[None, :]
    dr = key_rows - class_rows[:, None] + kh - 1
    qc = np.arange(GRID_W)
    kc = np.arange(GRID_W)
    win = np.clip(qc - NA_KW // 2, 0, GRID_W - NA_KW)
    valid = (kc[None, :] >= win[:, None]) & (kc[None, :] < win[:, None] + NA_KW)
    dc = np.clip(kc[None, :] - qc[:, None] + NA_KW - 1, 0, 2 * NA_KW - 2)
    b = rpb.astype(F32)[:, dr[:, None, :, None], dc[None, :, None, :]]
    b = jnp.where(valid[None, None, :, None, :], b, -jnp.inf)
    b = b.reshape(NA_HEADS // 2, 2, NA_CLASSES, GRID_W, kh * GRID_W)
    return b.transpose(2, 0, 1, 3, 4).reshape(NA_CLASSES, NA_HEADS // 2, 2 * GRID_W, kh * GRID_W)


def _pad_rank(wa2, offset):
    return jnp.zeros((LANES, GLA_W), F32).at[offset:offset + GLA_RANK].set(wa2)


def kernel(x, c, ctx, c_ctx, w_mod, b_mod, norm_mix, norm_ffn, w_in, gla_wa2_f, gla_ba_f, gla_wa2_b, gla_ba_b,
           gla_norm, na_rpb, swa_sink, w_out, w_gate, w_up, w_down, final_norm):
    batch, seq_len, d = x.shape
    ctx_len = ctx.shape[1]
    depth = w_mod.shape[0]
    assert d == D_MODEL and ctx_len == ROW_BLOCK and seq_len % ROW_BLOCK == 0 and batch < 8
    assert seq_len % GRID_W == 0 and seq_len // GRID_W >= 2 * NA_KH

    xs = jnp.concatenate([ctx, x], axis=1)
    cc = jnp.concatenate([c, c_ctx[None, :], jnp.zeros((8 - batch - 1, d), F32)], axis=0)
    mods = _modulation(cc, w_mod, b_mod)
    cos_t, sin_t = _rope_tables(ctx_len, seq_len)
    head_avg = jnp.asarray(np.kron(np.eye(GLA_HEADS), np.full((HEAD_DIM, HEAD_DIM), 1.0 / HEAD_DIM)), BF16)
    swa_rows = GLA_W + NA_W + _swa_pair_perm()

    for i in range(depth):
        last = i == depth - 1
        mods_i = mods[i].reshape(8, 1, 6 * d)
        gla_in, za, na_qkv, sq, sk, sv = _in_proj(xs, mods_i, norm_mix[i].reshape(1, d), _permute_w_in(w_in[i]),
                                                  cos_t, sin_t)
        o_f, o_b = _gla(gla_in, za, _pad_rank(gla_wa2_f[i], 0), _pad_rank(gla_wa2_b[i], GLA_RANK),
                        gla_ba_f[i].reshape(1, GLA_W), gla_ba_b[i].reshape(1, GLA_W))
        y_na = _na(na_qkv, _na_bias_table(na_rpb[i], seq_len // GRID_W), ctx_len)
        y_swa = _swa(swa_sink[i], sq, sk, sv, ctx_len)
        w_o = w_out[i]
        x1, h2 = _out_proj(o_f, o_b, gla_in, y_na, y_swa, xs, mods_i, norm_ffn[i].reshape(1, d),
                           gla_norm[i].reshape(1, GLA_W), head_avg,
                           w_o[:GLA_W].astype(BF16), w_o[GLA_W:GLA_W + NA_W].astype(BF16),
                           w_o[swa_rows].astype(BF16), skip_ctx=last)
        xs = _ffn(h2, x1, mods_i, w_gate[i].astype(BF16), w_up[i].astype(BF16), w_down[i].astype(BF16),
                  final_norm.reshape(1, d), latent_only=last, final=last)
    return xs
```

```python
import functools

import jax
import jax.numpy as jnp
import numpy as np
from jax import lax
from jax.experimental import pallas as pl
from jax.experimental.pallas import tpu as pltpu

F32 = jnp.float32
BF16 = jnp.bfloat16

D_MODEL = 1024
HEAD_DIM = 64
GRID_W = 64
GLA_HEADS = 4
NA_HEADS = 6
SWA_HEADS = 6
SWA_KV_HEADS = 2
GLA_W = GLA_HEADS * HEAD_DIM
NA_W = NA_HEADS * HEAD_DIM
SWA_W = SWA_HEADS * HEAD_DIM
SWA_KV_W = SWA_KV_HEADS * HEAD_DIM
GLA_RANK = 16
GLA_TAU = 16.0
GLA_CHUNK = 64
NA_KH = 8
NA_KW = 16
SWA_WINDOW = 128
ROPE_THETA = 10000.0
NORM_EPS = 1e-6

LANES = 128
ROW_BLOCK = 256
PAIR_W = 2 * HEAD_DIM
NA_CLASSES = 8
VMEM_LIMIT = 56 * 1024 * 1024

C_GLA = 0
C_ZA = C_GLA + 4 * GLA_W
C_NA = C_ZA + LANES
C_SQ = C_NA + 3 * NA_W
C_SK = C_SQ + SWA_W
C_SV = C_SK + SWA_KV_W
IN_PERM_W = C_SV + SWA_KV_W


def _dot(a, b):
    return jnp.dot(a, b, preferred_element_type=F32)


def _dot_nt(a, b):
    return lax.dot_general(a, b, (((1,), (1,)), ((), ())), preferred_element_type=F32)


def _dot_tn(a, b):
    return lax.dot_general(a, b, (((0,), (0,)), ((), ())), preferred_element_type=F32)


def _idiv(a, n):
    assert n & (n - 1) == 0
    return a >> (n.bit_length() - 1)


def _imod(a, n):
    assert n & (n - 1) == 0
    return a & (n - 1)


def _split2(a):
    hi = a.astype(BF16)
    lo = (a - hi.astype(F32)).astype(BF16)
    return hi, lo


def _split3(a):
    hi = a.astype(BF16)
    r = a - hi.astype(F32)
    mid = r.astype(BF16)
    lo = (r - mid.astype(F32)).astype(BF16)
    return hi, mid, lo


def _dot_split(a, b):
    ah, al = _split2(a)
    bh, bl = _split2(b)
    return _dot(ah, bh) + _dot(al, bh) + _dot(ah, bl)


def _silu(a):
    return a * jax.nn.sigmoid(a)


def _rms(x):
    return x * lax.rsqrt(jnp.mean(x * x, axis=-1, keepdims=True) + NORM_EPS)


def _params(n_parallel, n_arbitrary=0):
    return pltpu.CompilerParams(
        dimension_semantics=("parallel",) * n_parallel + ("arbitrary",) * n_arbitrary,
        vmem_limit_bytes=VMEM_LIMIT)


def _mod_kernel(c_ref, w_ref, b_ref, o_ref):
    o_ref[...] = _dot_split(_silu(c_ref[...]), w_ref[...]) + b_ref[...]


def _modulation(cc, w_mod, b_mod):
    depth, d, n = w_mod.shape
    tn = 1536
    return pl.pallas_call(
        _mod_kernel,
        grid=(depth, n // tn),
        in_specs=[
            pl.BlockSpec((8, d), lambda i, k: (0, 0)),
            pl.BlockSpec((None, d, tn), lambda i, k: (i, 0, k)),
            pl.BlockSpec((None, 1, tn), lambda i, k: (i, 0, k)),
        ],
        out_specs=pl.BlockSpec((None, 8, tn), lambda i, k: (i, 0, k)),
        out_shape=jax.ShapeDtypeStruct((depth, 8, n), F32),
        compiler_params=_params(2),
    )(cc, w_mod, b_mod.reshape(depth, 1, n))


def _mod_spec(chunk, batch, ctx_first):
    if ctx_first:
        return pl.BlockSpec((None, 1, D_MODEL), lambda b, j: (jnp.where(j == 0, batch, b), 0, chunk))
    return pl.BlockSpec((None, 1, D_MODEL), lambda b, j: (b, 0, chunk))


def _in_proj_kernel(x_ref, sh_ref, sc_ref, nw_ref, w_ref, cos_ref, sin_ref,
                    gla_ref, za_ref, na_ref, sq_ref, sk_ref, sv_ref):
    h = _rms(x_ref[...]) * nw_ref[...]
    h = (h * (1.0 + sc_ref[...]) + sh_ref[...]).astype(BF16)
    gla_ref[...] = _dot(h, w_ref[:, C_GLA:C_ZA])
    za_ref[...] = _dot(h, w_ref[:, C_ZA:C_NA])
    na_ref[...] = _dot(h, w_ref[:, C_NA:C_SQ]).astype(BF16)
    sv_ref[...] = _dot(h, w_ref[:, C_SV:IN_PERM_W]).astype(BF16)
    qk = _dot(h, w_ref[:, C_SQ:C_SV])
    cos = cos_ref[...]
    sin = sin_ref[...]
    lane = lax.broadcasted_iota(jnp.int32, cos.shape, 1)
    first_half = _imod(lane, HEAD_DIM) < (HEAD_DIM // 2)
    n_q = SWA_W // LANES
    for cb in range(n_q + 1):
        a = qk[:, cb * LANES:(cb + 1) * LANES]
        partner = jnp.where(first_half, pltpu.roll(a, LANES - HEAD_DIM // 2, 1), pltpu.roll(a, HEAD_DIM // 2, 1))
        r = (a * cos + partner * sin).astype(BF16)
        if cb < n_q:
            sq_ref[:, cb * LANES:(cb + 1) * LANES] = r
        else:
            sk_ref[...] = r


def _in_proj(xs, mods, norm_w, w_perm, cos_t, sin_t):
    batch, s, d = xs.shape
    tm = ROW_BLOCK
    row = lambda w: pl.BlockSpec((None, tm, w), lambda b, j: (b, j, 0))
    const = lambda shp: pl.BlockSpec(shp, lambda b, j: (0,) * len(shp))
    tab = pl.BlockSpec((tm, LANES), lambda b, j: (j, 0))
    return pl.pallas_call(
        _in_proj_kernel,
        grid=(batch, s // tm),
        in_specs=[row(d), _mod_spec(0, batch, True), _mod_spec(1, batch, True), const((1, d)),
                  const((d, IN_PERM_W)), tab, tab],
        out_specs=[row(4 * GLA_W), row(LANES), row(3 * NA_W), row(SWA_W), row(SWA_KV_W), row(SWA_KV_W)],
        out_shape=[
            jax.ShapeDtypeStruct((batch, s, 4 * GLA_W), F32),
            jax.ShapeDtypeStruct((batch, s, LANES), F32),
            jax.ShapeDtypeStruct((batch, s, 3 * NA_W), BF16),
            jax.ShapeDtypeStruct((batch, s, SWA_W), BF16),
            jax.ShapeDtypeStruct((batch, s, SWA_KV_W), BF16),
            jax.ShapeDtypeStruct((batch, s, SWA_KV_W), BF16),
        ],
        compiler_params=_params(2),
    )(xs, mods, mods, norm_w, w_perm, cos_t, sin_t)


def _gla_direction(q_ref, k_ref, v_ref, z_ref, wa_ref, ba_ref, o_ref, st_ref, reverse):
    c = GLA_CHUNK
    w = GLA_W
    n_chunks = ROW_BLOCK // c
    r64 = lax.broadcasted_iota(jnp.int32, (c, c), 0)
    c64 = lax.broadcasted_iota(jnp.int32, (c, c), 1)
    cum_m = (r64 <= c64 if reverse else r64 >= c64).astype(BF16)
    row_h = _idiv(lax.broadcasted_iota(jnp.int32, (GLA_HEADS * c, w), 0), c)
    col_h = _idiv(lax.broadcasted_iota(jnp.int32, (GLA_HEADS * c, w), 1), c)
    same_head = row_h == col_h
    ar = _imod(lax.broadcasted_iota(jnp.int32, (GLA_HEADS * c, c), 0), c)
    ac = lax.broadcasted_iota(jnp.int32, (GLA_HEADS * c, c), 1)
    causal = ar <= ac if reverse else ar >= ac
    out_head = _idiv(lax.broadcasted_iota(jnp.int32, (c, w), 1), c)

    z = _dot_split(z_ref[...], wa_ref[...]) + ba_ref[...]
    log_a = (jnp.minimum(z, 0.0) - jnp.log1p(jnp.exp(-jnp.abs(z)))) * (1.0 / GLA_TAU)
    st = st_ref[...]
    for ci in (range(n_chunks - 1, -1, -1) if reverse else range(n_chunks)):
        rows = slice(ci * c, (ci + 1) * c)
        hi, mid, lo = _split3(log_a[rows])
        b = _dot(cum_m, hi) + _dot(cum_m, mid) + _dot(cum_m, lo)
        b_tot = b[0:1] if reverse else b[c - 1:c]
        k = k_ref[rows, :]
        v = v_ref[rows, :].astype(BF16)
        q_in = (q_ref[rows, :] * jnp.exp(b)).astype(BF16)
        k_in = (k * jnp.exp(-b)).astype(BF16)
        k_end = (k * jnp.exp(b_tot - b)).astype(BF16)
        qx = jnp.where(same_head, jnp.concatenate([q_in] * GLA_HEADS, axis=0), jnp.zeros((), BF16))
        a = jnp.where(causal, _dot_nt(qx, k_in), 0.0).astype(BF16)
        r = _dot(a, v)
        o = _dot_nt(q_in, st.astype(BF16))
        for h in range(GLA_HEADS):
            o = o + jnp.where(out_head == h, r[h * c:(h + 1) * c], 0.0)
        o_ref[rows, :] = o
        st = st * jnp.exp(b_tot) + jnp.where(same_head, _dot_tn(v, k_end), 0.0)
    st_ref[...] = st


def _gla_kernel(qf_ref, kf_ref, vf_ref, zf_ref, qb_ref, kb_ref, vb_ref, zb_ref,
                waf_ref, wab_ref, baf_ref, bab_ref, of_ref, ob_ref, stf_ref, stb_ref):
    @pl.when(pl.program_id(1) == 0)
    def _():
        stf_ref[...] = jnp.zeros_like(stf_ref)
        stb_ref[...] = jnp.zeros_like(stb_ref)

    _gla_direction(qf_ref, kf_ref, vf_ref, zf_ref, waf_ref, baf_ref, of_ref, stf_ref, False)
    _gla_direction(qb_ref, kb_ref, vb_ref, zb_ref, wab_ref, bab_ref, ob_ref, stb_ref, True)


def _gla(gla_in, za, wa_f, wa_b, ba_f, ba_b):
    batch, s, _ = gla_in.shape
    tm = ROW_BLOCK
    nblk = s // tm
    fwd = lambda j: j
    bwd = lambda j: jnp.where(j == 0, 0, nblk - j)

    def col(order, cb):
        return pl.BlockSpec((None, tm, GLA_W), lambda b, j: (b, order(j), cb))

    def zspec(order):
        return pl.BlockSpec((None, tm, LANES), lambda b, j: (b, order(j), 0))

    const = lambda shp: pl.BlockSpec(shp, lambda b, j: (0,) * len(shp))
    return pl.pallas_call(
        _gla_kernel,
        grid=(batch, nblk),
        in_specs=[col(fwd, 0), col(fwd, 1), col(fwd, 2), zspec(fwd),
                  col(bwd, 0), col(bwd, 1), col(bwd, 2), zspec(bwd),
                  const((LANES, GLA_W)), const((LANES, GLA_W)), const((1, GLA_W)), const((1, GLA_W))],
        out_specs=[col(fwd, 0), col(bwd, 0)],
        out_shape=[jax.ShapeDtypeStruct((batch, s, GLA_W), F32)] * 2,
        scratch_shapes=[pltpu.VMEM((GLA_W, GLA_W), F32)] * 2,
        compiler_params=_params(1, 1),
    )(gla_in, gla_in, gla_in, za, gla_in, gla_in, gla_in, za, wa_f, wa_b, ba_f, ba_b)


def _stack_pair(q):
    left = lax.broadcasted_iota(jnp.int32, q.shape, 1) < HEAD_DIM
    zero = jnp.zeros((), q.dtype)
    return jnp.concatenate([jnp.where(left, q, zero), jnp.where(left, zero, q)], axis=0)


def _unstack_pair(o):
    n = o.shape[0] // 2
    left = lax.broadcasted_iota(jnp.int32, (n, o.shape[1]), 1) < HEAD_DIM
    return jnp.where(left, o[:n], o[n:])


def _softmax_pv(scores, values, extra_logit=None):
    m = functools.reduce(jnp.maximum, [jnp.max(s, axis=-1, keepdims=True) for s in scores])
    if extra_logit is not None:
        m = jnp.maximum(m, extra_logit)
    ps = [jnp.exp(s - m) for s in scores]
    denom = functools.reduce(jnp.add, [jnp.sum(p, axis=-1, keepdims=True) for p in ps])
    if extra_logit is not None:
        denom = denom + jnp.exp(extra_logit - m)
    o = functools.reduce(jnp.add, [_dot(p.astype(BF16), v) for p, v in zip(ps, values)])
    return o / denom


def _na_kernel(q_ref, k_ref, v_ref, bias_ref, o_ref, *, ctx_len, grid_rows):
    j = pl.program_id(1)
    n_pairs = NA_W // PAIR_W
    rows_per_block = ROW_BLOCK // GRID_W
    n_loc = NA_KH * GRID_W

    @pl.when(j == 0)
    def _():
        for p in range(n_pairs):
            cols = slice(p * PAIR_W, (p + 1) * PAIR_W)
            qx = _stack_pair(q_ref[:, cols])
            s_ctx = _dot_nt(qx, k_ref[0:ctx_len, cols])
            o = _softmax_pv([s_ctx], [v_ref[0:ctx_len, cols]])
            o_ref[:, cols] = _unstack_pair(o).astype(o_ref.dtype)

    @pl.when(j > 0)
    def _():
        for rr in range(rows_per_block):
            r = (j - 1) * rows_per_block + rr
            cls = jnp.where(r < NA_KH // 2, r,
                            jnp.where(r > grid_rows - NA_KH // 2, r - (grid_rows - NA_CLASSES), NA_KH // 2))
            k0 = pl.multiple_of(ctx_len + jnp.clip(r - NA_KH // 2, 0, grid_rows - NA_KH) * GRID_W, GRID_W)
            for p in range(n_pairs):
                cols = slice(p * PAIR_W, (p + 1) * PAIR_W)
                qx = _stack_pair(q_ref[rr * GRID_W:(rr + 1) * GRID_W, cols])
                s_loc = _dot_nt(qx, k_ref[pl.ds(k0, n_loc), cols]) + bias_ref[cls, p]
                s_ctx = _dot_nt(qx, k_ref[0:ctx_len, cols])
                o = _softmax_pv([s_loc, s_ctx], [v_ref[pl.ds(k0, n_loc), cols], v_ref[0:ctx_len, cols]])
                o_ref[rr * GRID_W:(rr + 1) * GRID_W, cols] = _unstack_pair(o).astype(o_ref.dtype)


def _na(na_qkv, bias, ctx_len):
    batch, s, _ = na_qkv.shape
    tm = ROW_BLOCK
    grid_rows = (s - ctx_len) // GRID_W
    full = lambda cb: pl.BlockSpec((None, s, NA_W), lambda b, j: (b, 0, cb))
    return pl.pallas_call(
        functools.partial(_na_kernel, ctx_len=ctx_len, grid_rows=grid_rows),
        grid=(batch, s // tm),
        in_specs=[pl.BlockSpec((None, tm, NA_W), lambda b, j: (b, j, 0)), full(1), full(2),
                  pl.BlockSpec(bias.shape, lambda b, j: (0, 0, 0, 0))],
        out_specs=pl.BlockSpec((None, tm, NA_W), lambda b, j: (b, j, 0)),
        out_shape=jax.ShapeDtypeStruct((batch, s, NA_W), BF16),
        compiler_params=_params(2),
    )(na_qkv, na_qkv, na_qkv, bias)


def _swa_kernel(sink_ref, q_ref, k_ref, v_ref, o_ref, *, ctx_len, seq_len):
    j = pl.program_id(1)
    tm = ROW_BLOCK
    n_pairs = SWA_W // PAIR_W
    n_loc = tm + 2 * SWA_WINDOW
    top = lax.broadcasted_iota(jnp.int32, (2 * tm, 1), 0) < tm
    k_ctx = k_ref[0:ctx_len, :]
    v_ctx = v_ref[0:ctx_len, :]

    def sink_col(p):
        return jnp.where(top, sink_ref[p], sink_ref[p + n_pairs])

    @pl.when(j == 0)
    def _():
        for p in range(n_pairs):
            cols = slice(p * PAIR_W, (p + 1) * PAIR_W)
            qx = _stack_pair(q_ref[:, cols])
            o = _softmax_pv([_dot_nt(qx, k_ctx)], [v_ctx], sink_col(p))
            o_ref[:, cols] = _unstack_pair(o).astype(o_ref.dtype)

    @pl.when(j > 0)
    def _():
        q0 = (j - 1) * tm
        k0 = jnp.clip(q0 - SWA_WINDOW, 0, seq_len - n_loc)
        qpos = q0 + _imod(lax.broadcasted_iota(jnp.int32, (2 * tm, n_loc), 0), tm)
        kpos = k0 + lax.broadcasted_iota(jnp.int32, (2 * tm, n_loc), 1)
        in_window = jnp.abs(qpos - kpos) <= SWA_WINDOW
        start = pl.multiple_of(ctx_len + k0, SWA_WINDOW)
        k_loc = k_ref[pl.ds(start, n_loc), :]
        v_loc = v_ref[pl.ds(start, n_loc), :]
        for p in range(n_pairs):
            cols = slice(p * PAIR_W, (p + 1) * PAIR_W)
            qx = _stack_pair(q_ref[:, cols])
            s_loc = jnp.where(in_window, _dot_nt(qx, k_loc), -jnp.inf)
            o = _softmax_pv([s_loc, _dot_nt(qx, k_ctx)], [v_loc, v_ctx], sink_col(p))
            o_ref[:, cols] = _unstack_pair(o).astype(o_ref.dtype)


def _swa(sink, sq, sk, sv, ctx_len):
    batch, s, _ = sq.shape
    tm = ROW_BLOCK
    full = pl.BlockSpec((None, s, SWA_KV_W), lambda b, j: (b, 0, 0))
    return pl.pallas_call(
        functools.partial(_swa_kernel, ctx_len=ctx_len, seq_len=s - ctx_len),
        grid=(batch, s // tm),
        in_specs=[pl.BlockSpec(memory_space=pltpu.SMEM),
                  pl.BlockSpec((None, tm, SWA_W), lambda b, j: (b, j, 0)), full, full],
        out_specs=pl.BlockSpec((None, tm, SWA_W), lambda b, j: (b, j, 0)),
        out_shape=jax.ShapeDtypeStruct((batch, s, SWA_W), BF16),
        compiler_params=_params(2),
    )(sink, sq, sk, sv)


def _out_proj_kernel(of_ref, ob_ref, gate_ref, na_ref, swa_ref, x_ref, g1_ref, sh2_ref, sc2_ref,
                     nffn_ref, gn_ref, avg_ref, wg_ref, wn_ref, ws_ref, x1_ref, h2_ref):
    o = of_ref[...] + ob_ref[...]
    hi, lo = _split2(o * o)
    avg = avg_ref[...]
    ms = _dot(hi, avg) + _dot(lo, avg)
    y = o * lax.rsqrt(ms + NORM_EPS) * gn_ref[...] * _silu(gate_ref[...])
    acc = _dot(y.astype(BF16), wg_ref[...]) + _dot(na_ref[...], wn_ref[...]) + _dot(swa_ref[...], ws_ref[...])
    x1 = x_ref[...] + g1_ref[...] * acc
    x1_ref[...] = x1
    h2 = _rms(x1) * nffn_ref[...]
    h2_ref[...] = (h2 * (1.0 + sc2_ref[...]) + sh2_ref[...]).astype(BF16)


def _out_proj(o_f, o_b, gla_in, y_na, y_swa, xs, mods, norm_ffn, gla_norm, avg, w_g, w_n, w_s, skip_ctx):
    batch, s, d = xs.shape
    tm = ROW_BLOCK
    off = 1 if skip_ctx else 0
    nblk = s // tm - off
    row = lambda w, cb=0: pl.BlockSpec((None, tm, w), lambda b, j: (b, j + off, cb))
    out = lambda w: pl.BlockSpec((None, tm, w), lambda b, j: (b, j, 0))
    const = lambda shp: pl.BlockSpec(shp, lambda b, j: (0,) * len(shp))
    ms = lambda chunk: _mod_spec(chunk, batch, not skip_ctx)
    return pl.pallas_call(
        _out_proj_kernel,
        grid=(batch, nblk),
        in_specs=[row(GLA_W), row(GLA_W), row(GLA_W, 3), row(NA_W), row(SWA_W), row(d),
                  ms(2), ms(3), ms(4), const((1, d)), const((1, GLA_W)), const((GLA_W, GLA_W)),
                  const((GLA_W, d)), const((NA_W, d)), const((SWA_W, d))],
        out_specs=[out(d), out(d)],
        out_shape=[jax.ShapeDtypeStruct((batch, nblk * tm, d), F32),
                   jax.ShapeDtypeStruct((batch, nblk * tm, d), BF16)],
        compiler_params=_params(2),
    )(o_f, o_b, gla_in, y_na, y_swa, xs, mods, mods, mods, norm_ffn, gla_norm, avg, w_g, w_n, w_s)


def _ffn_kernel(h_ref, x_ref, g2_ref, wg_ref, wu_ref, wd_ref, fn_ref, o_ref, *, ff_chunk, final):
    h = h_ref[...]
    d_ff = wg_ref.shape[1]
    acc = jnp.zeros(x_ref.shape, F32)
    for c0 in range(0, d_ff, ff_chunk):
        g = _dot(h, wg_ref[:, c0:c0 + ff_chunk])
        u = _dot(h, wu_ref[:, c0:c0 + ff_chunk])
        acc = acc + _dot((_silu(g) * u).astype(BF16), wd_ref[c0:c0 + ff_chunk, :])
    x2 = x_ref[...] + g2_ref[...] * acc
    if final:
        x2 = _rms(x2) * fn_ref[...]
    o_ref[...] = x2


def _ffn(h2, x1, mods, w_gate, w_up, w_down, final_norm, latent_only, final):
    batch, s, d = x1.shape
    d_ff = w_gate.shape[1]
    tm = ROW_BLOCK
    row = lambda: pl.BlockSpec((None, tm, d), lambda b, j: (b, j, 0))
    const = lambda shp: pl.BlockSpec(shp, lambda b, j: (0,) * len(shp))
    return pl.pallas_call(
        functools.partial(_ffn_kernel, ff_chunk=d_ff // 2, final=final),
        grid=(batch, s // tm),
        in_specs=[row(), row(), _mod_spec(5, batch, not latent_only),
                  const((d, d_ff)), const((d, d_ff)), const((d_ff, d)), const((1, d))],
        out_specs=row(),
        out_shape=jax.ShapeDtypeStruct((batch, s, d), F32),
        compiler_params=_params(2),
    )(h2, x1, mods, w_gate, w_up, w_down, final_norm)


def _rope_tables(ctx_len, seq_len):
    t = jnp.arange(seq_len, dtype=jnp.int32)
    row = (t // GRID_W).astype(F32)
    col = (t % GRID_W).astype(F32)
    n_freq = HEAD_DIM // 4
    inv_freq = ROPE_THETA ** (-jnp.arange(n_freq, dtype=F32) / n_freq)
    ang = jnp.concatenate([row[:, None] * inv_freq, col[:, None] * inv_freq], axis=-1)
    cos, sin = jnp.cos(ang), jnp.sin(ang)
    cos_h = jnp.concatenate([cos, cos], axis=-1)
    sin_h = jnp.concatenate([-sin, sin], axis=-1)
    cos_t = jnp.concatenate([jnp.ones((ctx_len, HEAD_DIM), F32), cos_h], axis=0)
    sin_t = jnp.concatenate([jnp.zeros((ctx_len, HEAD_DIM), F32), sin_h], axis=0)
    return jnp.tile(cos_t, (1, 2)), jnp.tile(sin_t, (1, 2))


def _swa_pair_order(a, axis):
    n_pairs = SWA_W // PAIR_W
    heads = [h for p in range(n_pairs) for h in (p, p + n_pairs)]
    return jnp.concatenate([lax.slice_in_dim(a, h * HEAD_DIM, (h + 1) * HEAD_DIM, axis=axis) for h in heads],
                           axis=axis)


def _permute_w_in(w):
    scale = HEAD_DIM ** -0.5
    o = np.cumsum([0, GLA_W, GLA_W, GLA_W, GLA_W, GLA_RANK, GLA_RANK, NA_W, NA_W, NA_W, SWA_W, SWA_KV_W, SWA_KV_W])
    d = w.shape[0]
    parts = [w[:, o[0]:o[1]] * scale, w[:, o[1]:o[4]],
             w[:, o[4]:o[6]], jnp.zeros((d, LANES - 2 * GLA_RANK), w.dtype),
             w[:, o[6]:o[7]] * scale, w[:, o[7]:o[9]],
             _swa_pair_order(w[:, o[9]:o[10]], 1) * scale, w[:, o[10]:o[12]]]
    return jnp.concatenate(parts, axis=1).astype(BF16)


def _na_bias_table(rpb, grid_rows):
    kh = NA_KH
    class_rows = np.array([0, 1, 2, 3, kh // 2, grid_rows - 3, grid_rows - 2, grid_rows - 1])
    key_rows = np.clip(class_rows - kh // 2, 0, grid_rows - kh)[:, None] + np.arange(kh)[None, :]
    dr = key_rows - class_rows[:, None] + kh - 1
    qc = np.arange(GRID_W)
    kc = np.arange(GRID_W)
    win = np.clip(qc - NA_KW // 2, 0, GRID_W - NA_KW)
    valid = (kc[None, :] >= win[:, None]) & (kc[None, :] < win[:, None] + NA_KW)
    dc = kc[None, :] - qc[:, None] + NA_KW - 1
    onehot = ((dc[None] == np.arange(2 * NA_KW - 1)[:, None, None]) & valid[None]).astype(np.float32)
    t = jnp.einsum('hde,eck->hdck', rpb.astype(F32), onehot, precision=lax.Precision.HIGHEST)
    t = jnp.where(valid[None, None], t, -jnp.inf)
    b = jnp.stack([jnp.stack([t[:, dr[a, j]] for j in range(kh)], axis=2) for a in range(NA_CLASSES)], axis=0)
    return b.reshape(NA_CLASSES, NA_HEADS // 2, 2 * GRID_W, kh * GRID_W)


def _pad_rank(wa2, offset):
    return jnp.zeros((LANES, GLA_W), F32).at[offset:offset + GLA_RANK].set(wa2)


def kernel(x, c, ctx, c_ctx, w_mod, b_mod, norm_mix, norm_ffn, w_in, gla_wa2_f, gla_ba_f, gla_wa2_b, gla_ba_b,
           gla_norm, na_rpb, swa_sink, w_out, w_gate, w_up, w_down, final_norm):
    batch, seq_len, d = x.shape
    ctx_len = ctx.shape[1]
    depth = w_mod.shape[0]
    assert d == D_MODEL and ctx_len == ROW_BLOCK and seq_len % ROW_BLOCK == 0 and batch < 8
    assert seq_len % GRID_W == 0 and seq_len // GRID_W >= 2 * NA_KH

    xs = jnp.concatenate([ctx, x], axis=1)
    cc = jnp.concatenate([c, c_ctx[None, :], jnp.zeros((8 - batch - 1, d), F32)], axis=0)
    mods = _modulation(cc, w_mod, b_mod)
    cos_t, sin_t = _rope_tables(ctx_len, seq_len)
    head_avg = jnp.asarray(np.kron(np.eye(GLA_HEADS), np.full((HEAD_DIM, HEAD_DIM), 1.0 / HEAD_DIM)), BF16)

    for i in range(depth):
        last = i == depth - 1
        mods_i = mods[i].reshape(8, 1, 6 * d)
        gla_in, za, na_qkv, sq, sk, sv = _in_proj(xs, mods_i, norm_mix[i].reshape(1, d), _permute_w_in(w_in[i]),
                                                  cos_t, sin_t)
        o_f, o_b = _gla(gla_in, za, _pad_rank(gla_wa2_f[i], 0), _pad_rank(gla_wa2_b[i], GLA_RANK),
                        gla_ba_f[i].reshape(1, GLA_W), gla_ba_b[i].reshape(1, GLA_W))
        y_na = _na(na_qkv, _na_bias_table(na_rpb[i], seq_len // GRID_W), ctx_len)
        y_swa = _swa(swa_sink[i], sq, sk, sv, ctx_len)
        w_o = w_out[i]
        x1, h2 = _out_proj(o_f, o_b, gla_in, y_na, y_swa, xs, mods_i, norm_ffn[i].reshape(1, d),
                           gla_norm[i].reshape(1, GLA_W), head_avg,
                           w_o[:GLA_W].astype(BF16), w_o[GLA_W:GLA_W + NA_W].astype(BF16),
                           _swa_pair_order(w_o[GLA_W + NA_W:], 0).astype(BF16), skip_ctx=last)
        xs = _ffn(h2, x1, mods_i, w_gate[i].astype(BF16), w_up[i].astype(BF16), w_down[i].astype(BF16),
                  final_norm.reshape(1, d), latent_only=last, final=last)
    return xs
```

```python
import functools

import jax
import jax.numpy as jnp
import numpy as np
from jax import lax
from jax.experimental import pallas as pl
from jax.experimental.pallas import tpu as pltpu

F32 = jnp.float32
BF16 = jnp.bfloat16

D_MODEL = 1024
HEAD_DIM = 64
GRID_W = 64
GLA_HEADS = 4
NA_HEADS = 6
SWA_HEADS = 6
SWA_KV_HEADS = 2
GLA_W = GLA_HEADS * HEAD_DIM
NA_W = NA_HEADS * HEAD_DIM
SWA_W = SWA_HEADS * HEAD_DIM
SWA_KV_W = SWA_KV_HEADS * HEAD_DIM
GLA_RANK = 16
GLA_TAU = 16.0
GLA_CHUNK = 64
NA_KH = 8
NA_KW = 16
SWA_WINDOW = 128
ROPE_THETA = 10000.0
NORM_EPS = 1e-6

LANES = 128
ROW_BLOCK = 256
PAIR_W = 2 * HEAD_DIM
NA_CLASSES = 8
VMEM_LIMIT = 56 * 1024 * 1024

C_GLA = 0
C_ZA = C_GLA + 4 * GLA_W
C_NA = C_ZA + LANES
C_SQ = C_NA + 3 * NA_W
C_SK = C_SQ + SWA_W
C_SV = C_SK + SWA_KV_W
IN_PERM_W = C_SV + SWA_KV_W


def _dot(a, b):
    return jnp.dot(a, b, preferred_element_type=F32)


def _dot_nt(a, b):
    return lax.dot_general(a, b, (((1,), (1,)), ((), ())), preferred_element_type=F32)


def _dot_tn(a, b):
    return lax.dot_general(a, b, (((0,), (0,)), ((), ())), preferred_element_type=F32)


def _idiv(a, n):
    assert n & (n - 1) == 0
    return a >> (n.bit_length() - 1)


def _imod(a, n):
    assert n & (n - 1) == 0
    return a & (n - 1)


def _split2(a):
    hi = a.astype(BF16)
    lo = (a - hi.astype(F32)).astype(BF16)
    return hi, lo


def _split3(a):
    hi = a.astype(BF16)
    r = a - hi.astype(F32)
    mid = r.astype(BF16)
    lo = (r - mid.astype(F32)).astype(BF16)
    return hi, mid, lo


def _dot_split(a, b):
    ah, al = _split2(a)
    bh, bl = _split2(b)
    return _dot(ah, bh) + _dot(al, bh) + _dot(ah, bl)


def _silu(a):
    return a * jax.nn.sigmoid(a)


def _rms(x):
    return x * lax.rsqrt(jnp.mean(x * x, axis=-1, keepdims=True) + NORM_EPS)


def _params(n_parallel, n_arbitrary=0):
    return pltpu.CompilerParams(
        dimension_semantics=("parallel",) * n_parallel + ("arbitrary",) * n_arbitrary,
        vmem_limit_bytes=VMEM_LIMIT)


def _mod_kernel(c_ref, w_ref, b_ref, o_ref):
    o_ref[...] = _dot_split(_silu(c_ref[...]), w_ref[...]) + b_ref[...]


def _modulation(cc, w_mod, b_mod):
    depth, d, n = w_mod.shape
    tn = 1536
    return pl.pallas_call(
        _mod_kernel,
        grid=(depth, n // tn),
        in_specs=[
            pl.BlockSpec((8, d), lambda i, k: (0, 0)),
            pl.BlockSpec((None, d, tn), lambda i, k: (i, 0, k)),
            pl.BlockSpec((None, 1, tn), lambda i, k: (i, 0, k)),
        ],
        out_specs=pl.BlockSpec((None, 8, tn), lambda i, k: (i, 0, k)),
        out_shape=jax.ShapeDtypeStruct((depth, 8, n), F32),
        compiler_params=_params(2),
    )(cc, w_mod, b_mod.reshape(depth, 1, n))


def _mod_spec(chunk, batch, ctx_first):
    if ctx_first:
        return pl.BlockSpec((None, 1, D_MODEL), lambda b, j: (jnp.where(j == 0, batch, b), 0, chunk))
    return pl.BlockSpec((None, 1, D_MODEL), lambda b, j: (b, 0, chunk))


def _in_proj_kernel(x_ref, sh_ref, sc_ref, nw_ref, w_ref, cos_ref, sin_ref,
                    gla_ref, za_ref, na_ref, sq_ref, sk_ref, sv_ref):
    h = _rms(x_ref[...]) * nw_ref[...]
    h = (h * (1.0 + sc_ref[...]) + sh_ref[...]).astype(BF16)
    gla_ref[...] = _dot(h, w_ref[:, C_GLA:C_ZA])
    za_ref[...] = _dot(h, w_ref[:, C_ZA:C_NA])
    na_ref[...] = _dot(h, w_ref[:, C_NA:C_SQ]).astype(BF16)
    sv_ref[...] = _dot(h, w_ref[:, C_SV:IN_PERM_W]).astype(BF16)
    qk = _dot(h, w_ref[:, C_SQ:C_SV])
    cos = cos_ref[...]
    sin = sin_ref[...]
    lane = lax.broadcasted_iota(jnp.int32, cos.shape, 1)
    first_half = _imod(lane, HEAD_DIM) < (HEAD_DIM // 2)
    n_q = SWA_W // LANES
    for cb in range(n_q + 1):
        a = qk[:, cb * LANES:(cb + 1) * LANES]
        partner = jnp.where(first_half, pltpu.roll(a, LANES - HEAD_DIM // 2, 1), pltpu.roll(a, HEAD_DIM // 2, 1))
        r = (a * cos + partner * sin).astype(BF16)
        if cb < n_q:
            sq_ref[:, cb * LANES:(cb + 1) * LANES] = r
        else:
            sk_ref[...] = r


def _in_proj(xs, mods, norm_w, w_perm, cos_t, sin_t):
    batch, s, d = xs.shape
    tm = ROW_BLOCK
    row = lambda w: pl.BlockSpec((None, tm, w), lambda b, j: (b, j, 0))
    const = lambda shp: pl.BlockSpec(shp, lambda b, j: (0,) * len(shp))
    tab = pl.BlockSpec((tm, LANES), lambda b, j: (j, 0))
    return pl.pallas_call(
        _in_proj_kernel,
        grid=(batch, s // tm),
        in_specs=[row(d), _mod_spec(0, batch, True), _mod_spec(1, batch, True), const((1, d)),
                  const((d, IN_PERM_W)), tab, tab],
        out_specs=[row(4 * GLA_W), row(LANES), row(3 * NA_W), row(SWA_W), row(SWA_KV_W), row(SWA_KV_W)],
        out_shape=[
            jax.ShapeDtypeStruct((batch, s, 4 * GLA_W), F32),
            jax.ShapeDtypeStruct((batch, s, LANES), F32),
            jax.ShapeDtypeStruct((batch, s, 3 * NA_W), BF16),
            jax.ShapeDtypeStruct((batch, s, SWA_W), BF16),
            jax.ShapeDtypeStruct((batch, s, SWA_KV_W), BF16),
            jax.ShapeDtypeStruct((batch, s, SWA_KV_W), BF16),
        ],
        compiler_params=_params(2),
    )(xs, mods, mods, norm_w, w_perm, cos_t, sin_t)


def _gla_kernel(qf_ref, kf_ref, vf_ref, zf_ref, qb_ref, kb_ref, vb_ref, zb_ref,
                waf_ref, wab_ref, baf_ref, bab_ref, of_ref, ob_ref, stf_ref, stb_ref):
    @pl.when(pl.program_id(1) == 0)
    def _():
        stf_ref[...] = jnp.zeros_like(stf_ref)
        stb_ref[...] = jnp.zeros_like(stb_ref)

    c = GLA_CHUNK
    w = GLA_W
    n_chunks = ROW_BLOCK // c
    r64 = lax.broadcasted_iota(jnp.int32, (c, c), 0)
    c64 = lax.broadcasted_iota(jnp.int32, (c, c), 1)
    row_h = _idiv(lax.broadcasted_iota(jnp.int32, (GLA_HEADS * c, w), 0), c)
    col_h = _idiv(lax.broadcasted_iota(jnp.int32, (GLA_HEADS * c, w), 1), c)
    same_head = row_h == col_h
    ar = _imod(lax.broadcasted_iota(jnp.int32, (GLA_HEADS * c, c), 0), c)
    ac = lax.broadcasted_iota(jnp.int32, (GLA_HEADS * c, c), 1)
    out_head = _idiv(lax.broadcasted_iota(jnp.int32, (c, w), 1), c)
    dirs = [
        dict(q=qf_ref, k=kf_ref, v=vf_ref, z=zf_ref, wa=waf_ref, ba=baf_ref, o=of_ref, st=stf_ref, rev=False,
             order=range(n_chunks), cum=(r64 >= c64).astype(BF16), causal=ar >= ac),
        dict(q=qb_ref, k=kb_ref, v=vb_ref, z=zb_ref, wa=wab_ref, ba=bab_ref, o=ob_ref, st=stb_ref, rev=True,
             order=range(n_chunks - 1, -1, -1), cum=(r64 <= c64).astype(BF16), causal=ar <= ac),
    ]
    units = [(d, ci) for d in dirs for ci in d['order']]
    rows = lambda ci: slice(ci * c, (ci + 1) * c)

    log_a = []
    for d in dirs:
        z = _dot_split(d['z'][...], d['wa'][...]) + d['ba'][...]
        log_a.append((jnp.minimum(z, 0.0) - jnp.log1p(jnp.exp(-jnp.abs(z)))) * (1.0 / GLA_TAU))
    b, b_tot = [], []
    for (d, ci), la in zip(units, [la for la in log_a for _ in range(n_chunks)]):
        hi, mid, lo = _split3(la[rows(ci)])
        b.append(_dot(d['cum'], hi) + _dot(d['cum'], mid) + _dot(d['cum'], lo))
        b_tot.append(b[-1][0:1] if d['rev'] else b[-1][c - 1:c])
    k = [d['k'][rows(ci), :] for d, ci in units]
    v = [d['v'][rows(ci), :].astype(BF16) for d, ci in units]
    q_in = [(d['q'][rows(ci), :] * jnp.exp(bu)).astype(BF16) for (d, ci), bu in zip(units, b)]
    k_in = [(ku * jnp.exp(-bu)).astype(BF16) for ku, bu in zip(k, b)]
    k_end = [(ku * jnp.exp(bt - bu)).astype(BF16) for ku, bu, bt in zip(k, b, b_tot)]
    qx = [jnp.where(same_head, jnp.concatenate([qu] * GLA_HEADS, axis=0), jnp.zeros((), BF16)) for qu in q_in]
    a = [jnp.where(d['causal'], _dot_nt(qxu, ku), 0.0).astype(BF16) for (d, _), qxu, ku in zip(units, qx, k_in)]
    r = [_dot(au, vu) for au, vu in zip(a, v)]
    upd = [jnp.where(same_head, _dot_tn(vu, ku), 0.0) for vu, ku in zip(v, k_end)]
    st_enter = []
    for di, d in enumerate(dirs):
        st = d['st'][...]
        for u in range(di * n_chunks, (di + 1) * n_chunks):
            st_enter.append(st.astype(BF16))
            st = st * jnp.exp(b_tot[u]) + upd[u]
        d['st'][...] = st
    for (d, ci), qu, su, ru in zip(units, q_in, st_enter, r):
        o = _dot_nt(qu, su)
        for h in range(GLA_HEADS):
            o = o + jnp.where(out_head == h, ru[h * c:(h + 1) * c], 0.0)
        d['o'][rows(ci), :] = o


def _gla(gla_in, za, wa_f, wa_b, ba_f, ba_b):
    batch, s, _ = gla_in.shape
    tm = ROW_BLOCK
    nblk = s // tm
    fwd = lambda j: j
    bwd = lambda j: jnp.where(j == 0, 0, nblk - j)

    def col(order, cb):
        return pl.BlockSpec((None, tm, GLA_W), lambda b, j: (b, order(j), cb))

    def zspec(order):
        return pl.BlockSpec((None, tm, LANES), lambda b, j: (b, order(j), 0))

    const = lambda shp: pl.BlockSpec(shp, lambda b, j: (0,) * len(shp))
    return pl.pallas_call(
        _gla_kernel,
        grid=(batch, nblk),
        in_specs=[col(fwd, 0), col(fwd, 1), col(fwd, 2), zspec(fwd),
                  col(bwd, 0), col(bwd, 1), col(bwd, 2), zspec(bwd),
                  const((LANES, GLA_W)), const((LANES, GLA_W)), const((1, GLA_W)), const((1, GLA_W))],
        out_specs=[col(fwd, 0), col(bwd, 0)],
        out_shape=[jax.ShapeDtypeStruct((batch, s, GLA_W), F32)] * 2,
        scratch_shapes=[pltpu.VMEM((GLA_W, GLA_W), F32)] * 2,
        compiler_params=_params(1, 1),
    )(gla_in, gla_in, gla_in, za, gla_in, gla_in, gla_in, za, wa_f, wa_b, ba_f, ba_b)


def _stack_pair(q):
    left = lax.broadcasted_iota(jnp.int32, q.shape, 1) < HEAD_DIM
    zero = jnp.zeros((), q.dtype)
    return jnp.concatenate([jnp.where(left, q, zero), jnp.where(left, zero, q)], axis=0)


def _unstack_pair(o):
    n = o.shape[0] // 2
    left = lax.broadcasted_iota(jnp.int32, (n, o.shape[1]), 1) < HEAD_DIM
    return jnp.where(left, o[:n], o[n:])


def _softmax_weights(scores, extra_logit=None):
    m = functools.reduce(jnp.maximum, [jnp.max(s, axis=-1, keepdims=True) for s in scores])
    if extra_logit is not None:
        m = jnp.maximum(m, extra_logit)
    ps = [jnp.exp(s - m) for s in scores]
    denom = functools.reduce(jnp.add, [jnp.sum(p, axis=-1, keepdims=True) for p in ps])
    if extra_logit is not None:
        denom = denom + jnp.exp(extra_logit - m)
    return [p.astype(BF16) for p in ps], denom


def _softmax_pv(scores, values, extra_logit=None):
    ps, denom = _softmax_weights(scores, extra_logit)
    return functools.reduce(jnp.add, [_dot(p, v) for p, v in zip(ps, values)]) / denom


def _na_kernel(q_ref, k_ref, v_ref, bias_ref, o_ref, *, ctx_len, grid_rows):
    j = pl.program_id(1)
    n_pairs = NA_W // PAIR_W
    rows_per_block = ROW_BLOCK // GRID_W
    n_loc = NA_KH * GRID_W
    unit = 2 * GRID_W
    pair_cols = [slice(p * PAIR_W, (p + 1) * PAIR_W) for p in range(n_pairs)]
    units = [(rr, p) for p in range(n_pairs) for rr in range(rows_per_block)]
    qx = {(rr, p): _stack_pair(q_ref[rr * GRID_W:(rr + 1) * GRID_W, pair_cols[p]]) for rr, p in units}
    s_ctx = [_dot_nt(jnp.concatenate([qx[rr, p] for rr in range(rows_per_block)], axis=0),
                     k_ref[0:ctx_len, pair_cols[p]]) for p in range(n_pairs)]

    def ctx_out(w):
        return [_dot(jnp.concatenate([w[rr, p][0][-1] for rr in range(rows_per_block)], axis=0),
                     v_ref[0:ctx_len, pair_cols[p]]) for p in range(n_pairs)]

    def finish(o, w):
        for rr, p in units:
            o_ref[rr * GRID_W:(rr + 1) * GRID_W, pair_cols[p]] = _unstack_pair(o[rr, p] / w[rr, p][1]).astype(
                o_ref.dtype)

    @pl.when(j == 0)
    def _():
        w = {(rr, p): _softmax_weights([s_ctx[p][rr * unit:(rr + 1) * unit]]) for rr, p in units}
        o_ctx = ctx_out(w)
        finish({(rr, p): o_ctx[p][rr * unit:(rr + 1) * unit] for rr, p in units}, w)

    @pl.when(j > 0)
    def _():
        cls, k0 = [], []
        for rr in range(rows_per_block):
            r = (j - 1) * rows_per_block + rr
            cls.append(jnp.where(r < NA_KH // 2, r,
                                 jnp.where(r > grid_rows - NA_KH // 2, r - (grid_rows - NA_CLASSES), NA_KH // 2)))
            k0.append(pl.multiple_of(ctx_len + jnp.clip(r - NA_KH // 2, 0, grid_rows - NA_KH) * GRID_W, GRID_W))
        s_loc = {(rr, p): _dot_nt(qx[rr, p], k_ref[pl.ds(k0[rr], n_loc), pair_cols[p]]) + bias_ref[cls[rr], p]
                 for rr, p in units}
        w = {(rr, p): _softmax_weights([s_loc[rr, p], s_ctx[p][rr * unit:(rr + 1) * unit]]) for rr, p in units}
        o_ctx = ctx_out(w)
        o = {(rr, p): _dot(w[rr, p][0][0], v_ref[pl.ds(k0[rr], n_loc), pair_cols[p]])
             + o_ctx[p][rr * unit:(rr + 1) * unit] for rr, p in units}
        finish(o, w)


def _na(na_qkv, bias, ctx_len):
    batch, s, _ = na_qkv.shape
    tm = ROW_BLOCK
    grid_rows = (s - ctx_len) // GRID_W
    full = lambda cb: pl.BlockSpec((None, s, NA_W), lambda b, j: (b, 0, cb))
    return pl.pallas_call(
        functools.partial(_na_kernel, ctx_len=ctx_len, grid_rows=grid_rows),
        grid=(batch, s // tm),
        in_specs=[pl.BlockSpec((None, tm, NA_W), lambda b, j: (b, j, 0)), full(1), full(2),
                  pl.BlockSpec(bias.shape, lambda b, j: (0, 0, 0, 0))],
        out_specs=pl.BlockSpec((None, tm, NA_W), lambda b, j: (b, j, 0)),
        out_shape=jax.ShapeDtypeStruct((batch, s, NA_W), BF16),
        compiler_params=_params(2),
    )(na_qkv, na_qkv, na_qkv, bias)


def _swa_kernel(sink_ref, q_ref, k_ref, v_ref, o_ref, *, ctx_len, seq_len):
    j = pl.program_id(1)
    tm = ROW_BLOCK
    n_pairs = SWA_W // PAIR_W
    n_loc = tm + 2 * SWA_WINDOW
    pair_cols = [slice(p * PAIR_W, (p + 1) * PAIR_W) for p in range(n_pairs)]
    lane = lax.broadcasted_iota(jnp.int32, (tm, PAIR_W), 1)
    zero = jnp.zeros((), BF16)
    units = [(p + side * n_pairs, p, side) for p in range(n_pairs) for side in (0, 1)]
    qx = [jnp.where((lane < HEAD_DIM) == (side == 0), q_ref[:, pair_cols[p]], zero) for _, p, side in units]
    k_ctx = k_ref[0:ctx_len, :]
    v_ctx = v_ref[0:ctx_len, :]

    def finish(o):
        for p, cols in enumerate(pair_cols):
            o_ref[:, cols] = jnp.where(lane < HEAD_DIM, o[2 * p], o[2 * p + 1]).astype(o_ref.dtype)

    @pl.when(j == 0)
    def _():
        s_ctx = [_dot_nt(q, k_ctx) for q in qx]
        w = [_softmax_weights([s], sink_ref[h]) for s, (h, _, _) in zip(s_ctx, units)]
        finish([_dot(ps[0], v_ctx) / denom for ps, denom in w])

    @pl.when(j > 0)
    def _():
        q0 = (j - 1) * tm
        k0 = jnp.clip(q0 - SWA_WINDOW, 0, seq_len - n_loc)
        qpos = q0 + lax.broadcasted_iota(jnp.int32, (tm, n_loc), 0)
        kpos = k0 + lax.broadcasted_iota(jnp.int32, (tm, n_loc), 1)
        in_window = jnp.abs(qpos - kpos) <= SWA_WINDOW
        start = pl.multiple_of(ctx_len + k0, SWA_WINDOW)
        k_loc = k_ref[pl.ds(start, n_loc), :]
        v_loc = v_ref[pl.ds(start, n_loc), :]
        s_loc = [jnp.where(in_window, _dot_nt(q, k_loc), -jnp.inf) for q in qx]
        s_ctx = [_dot_nt(q, k_ctx) for q in qx]
        w = [_softmax_weights([sl, sc], sink_ref[h]) for sl, sc, (h, _, _) in zip(s_loc, s_ctx, units)]
        finish([(_dot(ps[0], v_loc) + _dot(ps[1], v_ctx)) / denom for ps, denom in w])


def _swa(sink, sq, sk, sv, ctx_len):
    batch, s, _ = sq.shape
    tm = ROW_BLOCK
    full = pl.BlockSpec((None, s, SWA_KV_W), lambda b, j: (b, 0, 0))
    return pl.pallas_call(
        functools.partial(_swa_kernel, ctx_len=ctx_len, seq_len=s - ctx_len),
        grid=(batch, s // tm),
        in_specs=[pl.BlockSpec(memory_space=pltpu.SMEM),
                  pl.BlockSpec((None, tm, SWA_W), lambda b, j: (b, j, 0)), full, full],
        out_specs=pl.BlockSpec((None, tm, SWA_W), lambda b, j: (b, j, 0)),
        out_shape=jax.ShapeDtypeStruct((batch, s, SWA_W), BF16),
        compiler_params=_params(2),
    )(sink, sq, sk, sv)


def _out_proj_kernel(of_ref, ob_ref, gate_ref, na_ref, swa_ref, x_ref, g1_ref, sh2_ref, sc2_ref,
                     nffn_ref, gn_ref, avg_ref, wg_ref, wn_ref, ws_ref, x1_ref, h2_ref):
    o = of_ref[...] + ob_ref[...]
    hi, lo = _split2(o * o)
    avg = avg_ref[...]
    ms = _dot(hi, avg) + _dot(lo, avg)
    y = o * lax.rsqrt(ms + NORM_EPS) * gn_ref[...] * _silu(gate_ref[...])
    acc = _dot(y.astype(BF16), wg_ref[...]) + _dot(na_ref[...], wn_ref[...]) + _dot(swa_ref[...], ws_ref[...])
    x1 = x_ref[...] + g1_ref[...] * acc
    x1_ref[...] = x1
    h2 = _rms(x1) * nffn_ref[...]
    h2_ref[...] = (h2 * (1.0 + sc2_ref[...]) + sh2_ref[...]).astype(BF16)


def _out_proj(o_f, o_b, gla_in, y_na, y_swa, xs, mods, norm_ffn, gla_norm, avg, w_g, w_n, w_s, skip_ctx):
    batch, s, d = xs.shape
    tm = ROW_BLOCK
    off = 1 if skip_ctx else 0
    nblk = s // tm - off
    row = lambda w, cb=0: pl.BlockSpec((None, tm, w), lambda b, j: (b, j + off, cb))
    out = lambda w: pl.BlockSpec((None, tm, w), lambda b, j: (b, j, 0))
    const = lambda shp: pl.BlockSpec(shp, lambda b, j: (0,) * len(shp))
    ms = lambda chunk: _mod_spec(chunk, batch, not skip_ctx)
    return pl.pallas_call(
        _out_proj_kernel,
        grid=(batch, nblk),
        in_specs=[row(GLA_W), row(GLA_W), row(GLA_W, 3), row(NA_W), row(SWA_W), row(d),
                  ms(2), ms(3), ms(4), const((1, d)), const((1, GLA_W)), const((GLA_W, GLA_W)),
                  const((GLA_W, d)), const((NA_W, d)), const((SWA_W, d))],
        out_specs=[out(d), out(d)],
        out_shape=[jax.ShapeDtypeStruct((batch, nblk * tm, d), F32),
                   jax.ShapeDtypeStruct((batch, nblk * tm, d), BF16)],
        compiler_params=_params(2),
    )(o_f, o_b, gla_in, y_na, y_swa, xs, mods, mods, mods, norm_ffn, gla_norm, avg, w_g, w_n, w_s)


def _ffn_kernel(h_ref, x_ref, g2_ref, wg_ref, wu_ref, wd_ref, fn_ref, o_ref, *, ff_chunk, final):
    h = h_ref[...]
    d_ff = wg_ref.shape[1]
    acc = jnp.zeros(x_ref.shape, F32)
    for c0 in range(0, d_ff, ff_chunk):
        g = _dot(h, wg_ref[:, c0:c0 + ff_chunk])
        u = _dot(h, wu_ref[:, c0:c0 + ff_chunk])
        acc = acc + _dot((_silu(g) * u).astype(BF16), wd_ref[c0:c0 + ff_chunk, :])
    x2 = x_ref[...] + g2_ref[...] * acc
    if final:
        x2 = _rms(x2) * fn_ref[...]
    o_ref[...] = x2


def _ffn(h2, x1, mods, w_gate, w_up, w_down, final_norm, latent_only, final):
    batch, s, d = x1.shape
    d_ff = w_gate.shape[1]
    tm = ROW_BLOCK
    row = lambda: pl.BlockSpec((None, tm, d), lambda b, j: (b, j, 0))
    const = lambda shp: pl.BlockSpec(shp, lambda b, j: (0,) * len(shp))
    return pl.pallas_call(
        functools.partial(_ffn_kernel, ff_chunk=d_ff // 2, final=final),
        grid=(batch, s // tm),
        in_specs=[row(), row(), _mod_spec(5, batch, not latent_only),
                  const((d, d_ff)), const((d, d_ff)), const((d_ff, d)), const((1, d))],
        out_specs=row(),
        out_shape=jax.ShapeDtypeStruct((batch, s, d), F32),
        compiler_params=_params(2),
    )(h2, x1, mods, w_gate, w_up, w_down, final_norm)


def _rope_tables(ctx_len, seq_len):
    t = jnp.arange(seq_len, dtype=jnp.int32)
    row = (t // GRID_W).astype(F32)
    col = (t % GRID_W).astype(F32)
    n_freq = HEAD_DIM // 4
    inv_freq = ROPE_THETA ** (-jnp.arange(n_freq, dtype=F32) / n_freq)
    ang = jnp.concatenate([row[:, None] * inv_freq, col[:, None] * inv_freq], axis=-1)
    cos, sin = jnp.cos(ang), jnp.sin(ang)
    cos_h = jnp.concatenate([cos, cos], axis=-1)
    sin_h = jnp.concatenate([-sin, sin], axis=-1)
    cos_t = jnp.concatenate([jnp.ones((ctx_len, HEAD_DIM), F32), cos_h], axis=0)
    sin_t = jnp.concatenate([jnp.zeros((ctx_len, HEAD_DIM), F32), sin_h], axis=0)
    return jnp.tile(cos_t, (1, 2)), jnp.tile(sin_t, (1, 2))


def _swa_pair_order(a, axis):
    n_pairs = SWA_W // PAIR_W
    heads = [h for p in range(n_pairs) for h in (p, p + n_pairs)]
    return jnp.concatenate([lax.slice_in_dim(a, h * HEAD_DIM, (h + 1) * HEAD_DIM, axis=axis) for h in heads],
                           axis=axis)


def _permute_w_in(w):
    scale = HEAD_DIM ** -0.5
    o = np.cumsum([0, GLA_W, GLA_W, GLA_W, GLA_W, GLA_RANK, GLA_RANK, NA_W, NA_W, NA_W, SWA_W, SWA_KV_W, SWA_KV_W])
    d = w.shape[0]
    parts = [w[:, o[0]:o[1]] * scale, w[:, o[1]:o[4]],
             w[:, o[4]:o[6]], jnp.zeros((d, LANES - 2 * GLA_RANK), w.dtype),
             w[:, o[6]:o[7]] * scale, w[:, o[7]:o[9]],
             _swa_pair_order(w[:, o[9]:o[10]], 1) * scale, w[:, o[10]:o[12]]]
    return jnp.concatenate(parts, axis=1).astype(BF16)


def _na_bias_table(rpb, grid_rows):
    kh = NA_KH
    class_rows = np.array([0, 1, 2, 3, kh // 2, grid_rows - 3, grid_rows - 2, grid_rows - 1])
    key_rows = np.clip(class_rows - kh // 2, 0, grid_rows - kh)[:, None] + np.arange(kh)[None, :]
    dr = key_rows - class_rows[:, None] + kh - 1
    qc = np.arange(GRID_W)
    kc = np.arange(GRID_W)
    win = np.clip(qc - NA_KW // 2, 0, GRID_W - NA_KW)
    valid = (kc[None, :] >= win[:, None]) & (kc[None, :] < win[:, None] + NA_KW)
    dc = kc[None, :] - qc[:, None] + NA_KW - 1
    onehot = ((dc[None] == np.arange(2 * NA_KW - 1)[:, None, None]) & valid[None]).astype(np.float32)
    t = jnp.einsum('hde,eck->hdck', rpb.astype(F32), onehot, precision=lax.Precision.HIGHEST)
    t = jnp.where(valid[None, None], t, -jnp.inf)
    b = jnp.stack([jnp.stack([t[:, dr[a, j]] for j in range(kh)], axis=2) for a in range(NA_CLASSES)], axis=0)
    return b.reshape(NA_CLASSES, NA_HEADS // 2, 2 * GRID_W, kh * GRID_W)


def _pad_rank(wa2, offset):
    return jnp.zeros((LANES, GLA_W), F32).at[offset:offset + GLA_RANK].set(wa2)


def kernel(x, c, ctx, c_ctx, w_mod, b_mod, norm_mix, norm_ffn, w_in, gla_wa2_f, gla_ba_f, gla_wa2_b, gla_ba_b,
           gla_norm, na_rpb, swa_sink, w_out, w_gate, w_up, w_down, final_norm):
    batch, seq_len, d = x.shape
    ctx_len = ctx.shape[1]
    depth = w_mod.shape[0]
    assert d == D_MODEL and ctx_len == ROW_BLOCK and seq_len % ROW_BLOCK == 0 and batch < 8
    assert seq_len % GRID_W == 0 and seq_len // GRID_W >= 2 * NA_KH

    xs = jnp.concatenate([ctx, x], axis=1)
    cc = jnp.concatenate([c, c_ctx[None, :], jnp.zeros((8 - batch - 1, d), F32)], axis=0)
    mods = _modulation(cc, w_mod, b_mod)
    cos_t, sin_t = _rope_tables(ctx_len, seq_len)
    head_avg = jnp.asarray(np.kron(np.eye(GLA_HEADS), np.full((HEAD_DIM, HEAD_DIM), 1.0 / HEAD_DIM)), BF16)

    for i in range(depth):
        last = i == depth - 1
        mods_i = mods[i].reshape(8, 1, 6 * d)
        gla_in, za, na_qkv, sq, sk, sv = _in_proj(xs, mods_i, norm_mix[i].reshape(1, d), _permute_w_in(w_in[i]),
                                                  cos_t, sin_t)
        o_f, o_b = _gla(gla_in, za, _pad_rank(gla_wa2_f[i], 0), _pad_rank(gla_wa2_b[i], GLA_RANK),
                        gla_ba_f[i].reshape(1, GLA_W), gla_ba_b[i].reshape(1, GLA_W))
        y_na = _na(na_qkv, _na_bias_table(na_rpb[i], seq_len // GRID_W), ctx_len)
        y_swa = _swa(swa_sink[i], sq, sk, sv, ctx_len)
        w_o = w_out[i]
        x1, h2 = _out_proj(o_f, o_b, gla_in, y_na, y_swa, xs, mods_i, norm_ffn[i].reshape(1, d),
                           gla_norm[i].reshape(1, GLA_W), head_avg,
                           w_o[:GLA_W].astype(BF16), w_o[GLA_W:GLA_W + NA_W].astype(BF16),
                           _swa_pair_order(w_o[GLA_W + NA_W:], 0).astype(BF16), skip_ctx=last)
        xs = _ffn(h2, x1, mods_i, w_gate[i].astype(BF16), w_up[i].astype(BF16), w_down[i].astype(BF16),
                  final_norm.reshape(1, d), latent_only=last, final=last)
    return xs
```

```python
import functools

import jax
import jax.numpy as jnp
import numpy as np
from jax import lax
from jax.experimental import pallas as pl
from jax.experimental.pallas import tpu as pltpu

F32 = jnp.float32
BF16 = jnp.bfloat16

D_MODEL = 1024
HEAD_DIM = 64
GRID_W = 64
GLA_HEADS = 4
NA_HEADS = 6
SWA_HEADS = 6
SWA_KV_HEADS = 2
GLA_W = GLA_HEADS * HEAD_DIM
NA_W = NA_HEADS * HEAD_DIM
SWA_W = SWA_HEADS * HEAD_DIM
SWA_KV_W = SWA_KV_HEADS * HEAD_DIM
GLA_RANK = 16
GLA_TAU = 16.0
GLA_CHUNK = 64
NA_KH = 8
NA_KW = 16
SWA_WINDOW = 128
ROPE_THETA = 10000.0
NORM_EPS = 1e-6

LANES = 128
ROW_BLOCK = 256
PAIR_W = 2 * HEAD_DIM
NA_CLASSES = 8
VMEM_LIMIT = 56 * 1024 * 1024

C_GLA = 0
C_ZA = C_GLA + 4 * GLA_W
C_NA = C_ZA + LANES
C_SQ = C_NA + 3 * NA_W
C_SK = C_SQ + SWA_W
C_SV = C_SK + SWA_KV_W
IN_PERM_W = C_SV + SWA_KV_W


def _dot(a, b):
    return jnp.dot(a, b, preferred_element_type=F32)


def _dot_nt(a, b):
    return lax.dot_general(a, b, (((1,), (1,)), ((), ())), preferred_element_type=F32)


def _dot_tn(a, b):
    return lax.dot_general(a, b, (((0,), (0,)), ((), ())), preferred_element_type=F32)


def _idiv(a, n):
    assert n & (n - 1) == 0
    return a >> (n.bit_length() - 1)


def _imod(a, n):
    assert n & (n - 1) == 0
    return a & (n - 1)


def _split2(a):
    hi = a.astype(BF16)
    lo = (a - hi.astype(F32)).astype(BF16)
    return hi, lo


def _split3(a):
    hi = a.astype(BF16)
    r = a - hi.astype(F32)
    mid = r.astype(BF16)
    lo = (r - mid.astype(F32)).astype(BF16)
    return hi, mid, lo


def _dot_split(a, b):
    ah, al = _split2(a)
    bh, bl = _split2(b)
    return _dot(ah, bh) + _dot(al, bh) + _dot(ah, bl)


def _silu(a):
    return a * jax.nn.sigmoid(a)


def _rms(x):
    return x * lax.rsqrt(jnp.mean(x * x, axis=-1, keepdims=True) + NORM_EPS)


def _params(n_parallel, n_arbitrary=0):
    return pltpu.CompilerParams(
        dimension_semantics=("parallel",) * n_parallel + ("arbitrary",) * n_arbitrary,
        vmem_limit_bytes=VMEM_LIMIT)


def _mod_kernel(c_ref, w_ref, b_ref, o_ref):
    o_ref[...] = _dot_split(_silu(c_ref[...]), w_ref[...]) + b_ref[...]


def _modulation(cc, w_mod, b_mod):
    depth, d, n = w_mod.shape
    tn = 1536
    return pl.pallas_call(
        _mod_kernel,
        grid=(depth, n // tn),
        in_specs=[
            pl.BlockSpec((8, d), lambda i, k: (0, 0)),
            pl.BlockSpec((None, d, tn), lambda i, k: (i, 0, k)),
            pl.BlockSpec((None, 1, tn), lambda i, k: (i, 0, k)),
        ],
        out_specs=pl.BlockSpec((None, 8, tn), lambda i, k: (i, 0, k)),
        out_shape=jax.ShapeDtypeStruct((depth, 8, n), F32),
        compiler_params=_params(2),
    )(cc, w_mod, b_mod.reshape(depth, 1, n))


def _mod_spec(layer, chunk, batch, ctx_first):
    if ctx_first:
        return pl.BlockSpec((None, None, 1, D_MODEL), lambda b, j: (layer, jnp.where(j == 0, batch, b), 0, chunk))
    return pl.BlockSpec((None, None, 1, D_MODEL), lambda b, j: (layer, b, 0, chunk))


def _layer_spec(layer, shape):
    return pl.BlockSpec((None,) + shape, lambda b, j: (layer,) + (0,) * len(shape))


def _token_rows(src_refs):
    if len(src_refs) == 1:
        return src_refs[0][...]
    ctx_ref, x_ref = src_refs
    return jnp.where(pl.program_id(1) == 0, ctx_ref[...], x_ref[...])


def _token_specs(srcs, tm, off=0):
    d = srcs[0].shape[-1]
    if len(srcs) == 1:
        return [pl.BlockSpec((None, tm, d), lambda b, j: (b, j + off, 0))]
    assert off == 0
    return [pl.BlockSpec((None, tm, d), lambda b, j: (b, 0, 0)),
            pl.BlockSpec((None, tm, d), lambda b, j: (b, jnp.maximum(j - 1, 0), 0))]


def _in_proj_kernel(*refs, n_src):
    src_refs = refs[:n_src]
    sh_ref, sc_ref, nw_ref, w_ref, cos_ref, sin_ref, gla_ref, za_ref, na_ref, sq_ref, sk_ref, sv_ref = refs[n_src:]
    x = _token_rows(src_refs)
    tm = x.shape[0]
    lane = lax.broadcasted_iota(jnp.int32, (tm // 2, LANES), 1)
    first_half = _imod(lane, HEAD_DIM) < (HEAD_DIM // 2)
    n_q = SWA_W // LANES
    for rows in (slice(0, tm // 2), slice(tm // 2, tm)):
        h = _rms(x[rows]) * nw_ref[...]
        h = (h * (1.0 + sc_ref[...]) + sh_ref[...]).astype(BF16)
        qk = _dot(h, w_ref[:, C_SQ:C_SV])
        cos = cos_ref[rows, :]
        sin = sin_ref[rows, :]
        for cb in range(n_q + 1):
            a = qk[:, cb * LANES:(cb + 1) * LANES]
            partner = jnp.where(first_half, pltpu.roll(a, LANES - HEAD_DIM // 2, 1), pltpu.roll(a, HEAD_DIM // 2, 1))
            r = (a * cos + partner * sin).astype(BF16)
            if cb < n_q:
                sq_ref[rows, cb * LANES:(cb + 1) * LANES] = r
            else:
                sk_ref[rows, :] = r
        gla_ref[rows, :] = _dot(h, w_ref[:, C_GLA:C_ZA])
        za_ref[rows, :] = _dot(h, w_ref[:, C_ZA:C_NA])
        na_ref[rows, :] = _dot(h, w_ref[:, C_NA:C_SQ]).astype(BF16)
        sv_ref[rows, :] = _dot(h, w_ref[:, C_SV:IN_PERM_W]).astype(BF16)


def _in_proj(srcs, layer, mods, norm_w, w_perm, cos_t, sin_t):
    batch, d = srcs[0].shape[0], srcs[0].shape[2]
    s = sum(a.shape[1] for a in srcs)
    tm = ROW_BLOCK
    row = lambda w: pl.BlockSpec((None, tm, w), lambda b, j: (b, j, 0))
    tab = pl.BlockSpec((tm, LANES), lambda b, j: (j, 0))
    return pl.pallas_call(
        functools.partial(_in_proj_kernel, n_src=len(srcs)),
        grid=(batch, s // tm),
        in_specs=_token_specs(srcs, tm) + [
            _mod_spec(layer, 0, batch, True), _mod_spec(layer, 1, batch, True), _layer_spec(layer, (1, d)),
            _layer_spec(layer, (d, IN_PERM_W)), tab, tab],
        out_specs=[row(4 * GLA_W), row(LANES), row(3 * NA_W), row(SWA_W), row(SWA_KV_W), row(SWA_KV_W)],
        out_shape=[
            jax.ShapeDtypeStruct((batch, s, 4 * GLA_W), F32),
            jax.ShapeDtypeStruct((batch, s, LANES), F32),
            jax.ShapeDtypeStruct((batch, s, 3 * NA_W), BF16),
            jax.ShapeDtypeStruct((batch, s, SWA_W), BF16),
            jax.ShapeDtypeStruct((batch, s, SWA_KV_W), BF16),
            jax.ShapeDtypeStruct((batch, s, SWA_KV_W), BF16),
        ],
        compiler_params=_params(2),
    )(*srcs, mods, mods, norm_w, w_perm, cos_t, sin_t)


def _gla_kernel(qf_ref, kf_ref, vf_ref, zf_ref, qb_ref, kb_ref, vb_ref, zb_ref,
                waf_ref, wab_ref, baf_ref, bab_ref, of_ref, ob_ref, stf_ref, stb_ref):
    @pl.when(pl.program_id(1) == 0)
    def _():
        stf_ref[...] = jnp.zeros_like(stf_ref)
        stb_ref[...] = jnp.zeros_like(stb_ref)

    c = GLA_CHUNK
    w = GLA_W
    n_chunks = ROW_BLOCK // c
    r64 = lax.broadcasted_iota(jnp.int32, (c, c), 0)
    c64 = lax.broadcasted_iota(jnp.int32, (c, c), 1)
    row_h = _idiv(lax.broadcasted_iota(jnp.int32, (GLA_HEADS * c, w), 0), c)
    col_h = _idiv(lax.broadcasted_iota(jnp.int32, (GLA_HEADS * c, w), 1), c)
    same_head = row_h == col_h
    ar = _imod(lax.broadcasted_iota(jnp.int32, (GLA_HEADS * c, c), 0), c)
    ac = lax.broadcasted_iota(jnp.int32, (GLA_HEADS * c, c), 1)
    out_head = _idiv(lax.broadcasted_iota(jnp.int32, (c, w), 1), c)
    dirs = [
        dict(q=qf_ref, k=kf_ref, v=vf_ref, z=zf_ref, wa=waf_ref, ba=baf_ref, o=of_ref, st=stf_ref, rev=False,
             order=range(n_chunks), cum=(r64 >= c64).astype(BF16), causal=ar >= ac),
        dict(q=qb_ref, k=kb_ref, v=vb_ref, z=zb_ref, wa=wab_ref, ba=bab_ref, o=ob_ref, st=stb_ref, rev=True,
             order=range(n_chunks - 1, -1, -1), cum=(r64 <= c64).astype(BF16), causal=ar <= ac),
    ]
    units = [(d, ci) for d in dirs for ci in d['order']]
    rows = lambda ci: slice(ci * c, (ci + 1) * c)

    log_a = []
    for d in dirs:
        z = _dot_split(d['z'][...], d['wa'][...]) + d['ba'][...]
        log_a.append((jnp.minimum(z, 0.0) - jnp.log1p(jnp.exp(-jnp.abs(z)))) * (1.0 / GLA_TAU))
    b, b_tot = [], []
    for (d, ci), la in zip(units, [la for la in log_a for _ in range(n_chunks)]):
        hi, mid, lo = _split3(la[rows(ci)])
        b.append(_dot(d['cum'], hi) + _dot(d['cum'], mid) + _dot(d['cum'], lo))
        b_tot.append(b[-1][0:1] if d['rev'] else b[-1][c - 1:c])
    k = [d['k'][rows(ci), :] for d, ci in units]
    v = [d['v'][rows(ci), :].astype(BF16) for d, ci in units]
    q_in = [(d['q'][rows(ci), :] * jnp.exp(bu)).astype(BF16) for (d, ci), bu in zip(units, b)]
    k_in = [(ku * jnp.exp(-bu)).astype(BF16) for ku, bu in zip(k, b)]
    k_end = [(ku * jnp.exp(bt - bu)).astype(BF16) for ku, bu, bt in zip(k, b, b_tot)]
    qx = [jnp.where(same_head, jnp.concatenate([qu] * GLA_HEADS, axis=0), jnp.zeros((), BF16)) for qu in q_in]
    a = [jnp.where(d['causal'], _dot_nt(qxu, ku), 0.0).astype(BF16) for (d, _), qxu, ku in zip(units, qx, k_in)]
    r = [_dot(au, vu) for au, vu in zip(a, v)]
    upd = [jnp.where(same_head, _dot_tn(vu, ku), 0.0) for vu, ku in zip(v, k_end)]
    st_enter = []
    for di, d in enumerate(dirs):
        st = d['st'][...]
        for u in range(di * n_chunks, (di + 1) * n_chunks):
            st_enter.append(st.astype(BF16))
            st = st * jnp.exp(b_tot[u]) + upd[u]
        d['st'][...] = st
    for (d, ci), qu, su, ru in zip(units, q_in, st_enter, r):
        o = _dot_nt(qu, su)
        for h in range(GLA_HEADS):
            o = o + jnp.where(out_head == h, ru[h * c:(h + 1) * c], 0.0)
        d['o'][rows(ci), :] = o


def _gla(gla_in, za, layer, wa_f, wa_b, ba_f, ba_b):
    batch, s, _ = gla_in.shape
    tm = ROW_BLOCK
    nblk = s // tm
    fwd = lambda j: j
    bwd = lambda j: jnp.where(j == 0, 0, nblk - j)

    def col(order, cb):
        return pl.BlockSpec((None, tm, GLA_W), lambda b, j: (b, order(j), cb))

    def zspec(order):
        return pl.BlockSpec((None, tm, LANES), lambda b, j: (b, order(j), 0))

    return pl.pallas_call(
        _gla_kernel,
        grid=(batch, nblk),
        in_specs=[col(fwd, 0), col(fwd, 1), col(fwd, 2), zspec(fwd),
                  col(bwd, 0), col(bwd, 1), col(bwd, 2), zspec(bwd),
                  _layer_spec(layer, (LANES, GLA_W)), _layer_spec(layer, (LANES, GLA_W)),
                  _layer_spec(layer, (1, GLA_W)), _layer_spec(layer, (1, GLA_W))],
        out_specs=[col(fwd, 0), col(bwd, 0)],
        out_shape=[jax.ShapeDtypeStruct((batch, s, GLA_W), F32)] * 2,
        scratch_shapes=[pltpu.VMEM((GLA_W, GLA_W), F32)] * 2,
        compiler_params=_params(1, 1),
    )(gla_in, gla_in, gla_in, za, gla_in, gla_in, gla_in, za, wa_f, wa_b, ba_f, ba_b)


def _stack_pair(q):
    left = lax.broadcasted_iota(jnp.int32, q.shape, 1) < HEAD_DIM
    zero = jnp.zeros((), q.dtype)
    return jnp.concatenate([jnp.where(left, q, zero), jnp.where(left, zero, q)], axis=0)


def _unstack_pair(o):
    n = o.shape[0] // 2
    left = lax.broadcasted_iota(jnp.int32, (n, o.shape[1]), 1) < HEAD_DIM
    return jnp.where(left, o[:n], o[n:])


def _softmax_weights(scores, extra_logit=None):
    m = functools.reduce(jnp.maximum, [jnp.max(s, axis=-1, keepdims=True) for s in scores])
    if extra_logit is not None:
        m = jnp.maximum(m, extra_logit)
    ps = [jnp.exp(s - m) for s in scores]
    denom = functools.reduce(jnp.add, [jnp.sum(p, axis=-1, keepdims=True) for p in ps])
    if extra_logit is not None:
        denom = denom + jnp.exp(extra_logit - m)
    return [p.astype(BF16) for p in ps], denom


def _softmax_pv(scores, values, extra_logit=None):
    ps, denom = _softmax_weights(scores, extra_logit)
    return functools.reduce(jnp.add, [_dot(p, v) for p, v in zip(ps, values)]) / denom


def _na_kernel(q_ref, k_ref, v_ref, bias_ref, o_ref, *, ctx_len, grid_rows):
    j = pl.program_id(1)
    n_pairs = NA_W // PAIR_W
    rows_per_block = ROW_BLOCK // GRID_W
    n_loc = NA_KH * GRID_W
    unit = 2 * GRID_W
    pair_cols = [slice(p * PAIR_W, (p + 1) * PAIR_W) for p in range(n_pairs)]
    units = [(rr, p) for p in range(n_pairs) for rr in range(rows_per_block)]
    qx = {(rr, p): _stack_pair(q_ref[rr * GRID_W:(rr + 1) * GRID_W, pair_cols[p]]) for rr, p in units}
    s_ctx = [_dot_nt(jnp.concatenate([qx[rr, p] for rr in range(rows_per_block)], axis=0),
                     k_ref[0:ctx_len, pair_cols[p]]) for p in range(n_pairs)]

    def ctx_out(w):
        return [_dot(jnp.concatenate([w[rr, p][0][-1] for rr in range(rows_per_block)], axis=0),
                     v_ref[0:ctx_len, pair_cols[p]]) for p in range(n_pairs)]

    def finish(o, w):
        for rr, p in units:
            o_ref[rr * GRID_W:(rr + 1) * GRID_W, pair_cols[p]] = _unstack_pair(o[rr, p] / w[rr, p][1]).astype(
                o_ref.dtype)

    @pl.when(j == 0)
    def _():
        w = {(rr, p): _softmax_weights([s_ctx[p][rr * unit:(rr + 1) * unit]]) for rr, p in units}
        o_ctx = ctx_out(w)
        finish({(rr, p): o_ctx[p][rr * unit:(rr + 1) * unit] for rr, p in units}, w)

    @pl.when(j > 0)
    def _():
        cls, k0 = [], []
        for rr in range(rows_per_block):
            r = (j - 1) * rows_per_block + rr
            cls.append(jnp.where(r < NA_KH // 2, r,
                                 jnp.where(r > grid_rows - NA_KH // 2, r - (grid_rows - NA_CLASSES), NA_KH // 2)))
            k0.append(pl.multiple_of(ctx_len + jnp.clip(r - NA_KH // 2, 0, grid_rows - NA_KH) * GRID_W, GRID_W))
        s_loc, w, o_loc = {}, {}, {}
        for t in range(len(units) + 2):
            if t < len(units):
                rr, p = units[t]
                s_loc[rr, p] = _dot_nt(qx[rr, p], k_ref[pl.ds(k0[rr], n_loc), pair_cols[p]]) + bias_ref[cls[rr], p]
            if 0 <= t - 1 < len(units):
                rr, p = units[t - 1]
                w[rr, p] = _softmax_weights([s_loc.pop((rr, p)), s_ctx[p][rr * unit:(rr + 1) * unit]])
            if 0 <= t - 2 < len(units):
                rr, p = units[t - 2]
                o_loc[rr, p] = _dot(w[rr, p][0][0], v_ref[pl.ds(k0[rr], n_loc), pair_cols[p]])
        o_ctx = ctx_out(w)
        finish({(rr, p): o_loc[rr, p] + o_ctx[p][rr * unit:(rr + 1) * unit] for rr, p in units}, w)


def _na(na_qkv, layer, bias, ctx_len):
    batch, s, _ = na_qkv.shape
    tm = ROW_BLOCK
    grid_rows = (s - ctx_len) // GRID_W
    full = lambda cb: pl.BlockSpec((None, s, NA_W), lambda b, j: (b, 0, cb))
    return pl.pallas_call(
        functools.partial(_na_kernel, ctx_len=ctx_len, grid_rows=grid_rows),
        grid=(batch, s // tm),
        in_specs=[pl.BlockSpec((None, tm, NA_W), lambda b, j: (b, j, 0)), full(1), full(2),
                  _layer_spec(layer, bias.shape[1:])],
        out_specs=pl.BlockSpec((None, tm, NA_W), lambda b, j: (b, j, 0)),
        out_shape=jax.ShapeDtypeStruct((batch, s, NA_W), BF16),
        compiler_params=_params(2),
    )(na_qkv, na_qkv, na_qkv, bias)


def _swa_kernel(sinks_ref, q_ref, k_ref, v_ref, o_ref, *, layer, ctx_len, seq_len):
    j = pl.program_id(1)
    tm = ROW_BLOCK
    n_pairs = SWA_W // PAIR_W
    n_loc = tm + 2 * SWA_WINDOW
    pair_cols = [slice(p * PAIR_W, (p + 1) * PAIR_W) for p in range(n_pairs)]
    lane = lax.broadcasted_iota(jnp.int32, (tm, PAIR_W), 1)
    zero = jnp.zeros((), BF16)
    units = [(p + side * n_pairs, p, side) for p in range(n_pairs) for side in (0, 1)]
    qx = [jnp.where((lane < HEAD_DIM) == (side == 0), q_ref[:, pair_cols[p]], zero) for _, p, side in units]
    k_ctx = k_ref[0:ctx_len, :]
    v_ctx = v_ref[0:ctx_len, :]

    def finish(o):
        for p, cols in enumerate(pair_cols):
            o_ref[:, cols] = jnp.where(lane < HEAD_DIM, o[2 * p], o[2 * p + 1]).astype(o_ref.dtype)

    @pl.when(j == 0)
    def _():
        s_ctx = [_dot_nt(q, k_ctx) for q in qx]
        w = [_softmax_weights([s], sinks_ref[layer, h]) for s, (h, _, _) in zip(s_ctx, units)]
        finish([_dot(ps[0], v_ctx) / denom for ps, denom in w])

    @pl.when(j > 0)
    def _():
        q0 = (j - 1) * tm
        k0 = jnp.clip(q0 - SWA_WINDOW, 0, seq_len - n_loc)
        qpos = q0 + lax.broadcasted_iota(jnp.int32, (tm, n_loc), 0)
        kpos = k0 + lax.broadcasted_iota(jnp.int32, (tm, n_loc), 1)
        in_window = jnp.abs(qpos - kpos) <= SWA_WINDOW
        start = pl.multiple_of(ctx_len + k0, SWA_WINDOW)
        k_loc = k_ref[pl.ds(start, n_loc), :]
        v_loc = v_ref[pl.ds(start, n_loc), :]
        s, w, o = {}, {}, {}
        for t in range(len(units) + 2):
            if t < len(units):
                s[t] = [jnp.where(in_window, _dot_nt(qx[t], k_loc), -jnp.inf), _dot_nt(qx[t], k_ctx)]
            if 0 <= t - 1 < len(units):
                w[t - 1] = _softmax_weights(s.pop(t - 1), sinks_ref[layer, units[t - 1][0]])
            if 0 <= t - 2 < len(units):
                ps, denom = w.pop(t - 2)
                o[t - 2] = (_dot(ps[0], v_loc) + _dot(ps[1], v_ctx)) / denom
        finish([o[u] for u in range(len(units))])


def _swa(sink, layer, sq, sk, sv, ctx_len):
    batch, s, _ = sq.shape
    tm = ROW_BLOCK
    full = pl.BlockSpec((None, s, SWA_KV_W), lambda b, j: (b, 0, 0))
    return pl.pallas_call(
        functools.partial(_swa_kernel, layer=layer, ctx_len=ctx_len, seq_len=s - ctx_len),
        grid=(batch, s // tm),
        in_specs=[pl.BlockSpec(memory_space=pltpu.SMEM),
                  pl.BlockSpec((None, tm, SWA_W), lambda b, j: (b, j, 0)), full, full],
        out_specs=pl.BlockSpec((None, tm, SWA_W), lambda b, j: (b, j, 0)),
        out_shape=jax.ShapeDtypeStruct((batch, s, SWA_W), BF16),
        compiler_params=_params(2),
    )(sink, sq, sk, sv)


def _out_proj_kernel(*refs, n_src):
    src_refs = refs[:n_src]
    (of_ref, ob_ref, gate_ref, na_ref, swa_ref, g1_ref, sh2_ref, sc2_ref,
     nffn_ref, gn_ref, avg_ref, wg_ref, wn_ref, ws_ref, x1_ref, h2_ref) = refs[n_src:]
    o = of_ref[...] + ob_ref[...]
    hi, lo = _split2(o * o)
    avg = avg_ref[...]
    ms = _dot(hi, avg) + _dot(lo, avg)
    y = o * lax.rsqrt(ms + NORM_EPS) * gn_ref[...] * _silu(gate_ref[...])
    acc = _dot(y.astype(BF16), wg_ref[...]) + _dot(na_ref[...], wn_ref[...]) + _dot(swa_ref[...], ws_ref[...])
    x1 = _token_rows(src_refs) + g1_ref[...] * acc
    x1_ref[...] = x1
    h2 = _rms(x1) * nffn_ref[...]
    h2_ref[...] = (h2 * (1.0 + sc2_ref[...]) + sh2_ref[...]).astype(BF16)


def _out_proj(srcs, o_f, o_b, gla_in, y_na, y_swa, layer, mods, norm_ffn, gla_norm, avg, w_g, w_n, w_s, skip_ctx):
    batch, s, _ = o_f.shape
    d = srcs[0].shape[2]
    tm = ROW_BLOCK
    off = 1 if skip_ctx else 0
    nblk = s // tm - off
    tok_off = off
    if skip_ctx and len(srcs) == 2:
        srcs, tok_off = srcs[1:], 0
    row = lambda w, cb=0: pl.BlockSpec((None, tm, w), lambda b, j: (b, j + off, cb))
    out = lambda w: pl.BlockSpec((None, tm, w), lambda b, j: (b, j, 0))
    ms = lambda chunk: _mod_spec(layer, chunk, batch, not skip_ctx)
    return pl.pallas_call(
        functools.partial(_out_proj_kernel, n_src=len(srcs)),
        grid=(batch, nblk),
        in_specs=_token_specs(srcs, tm, tok_off) + [
            row(GLA_W), row(GLA_W), row(GLA_W, 3), row(NA_W), row(SWA_W),
            ms(2), ms(3), ms(4), _layer_spec(layer, (1, d)), _layer_spec(layer, (1, GLA_W)),
            pl.BlockSpec((GLA_W, GLA_W), lambda b, j: (0, 0)),
            _layer_spec(layer, (GLA_W, d)), _layer_spec(layer, (NA_W, d)), _layer_spec(layer, (SWA_W, d))],
        out_specs=[out(d), out(d)],
        out_shape=[jax.ShapeDtypeStruct((batch, nblk * tm, d), F32),
                   jax.ShapeDtypeStruct((batch, nblk * tm, d), BF16)],
        compiler_params=_params(2),
    )(*srcs, o_f, o_b, gla_in, y_na, y_swa, mods, mods, mods, norm_ffn, gla_norm, avg, w_g, w_n, w_s)


def _ffn_kernel(h_ref, x_ref, g2_ref, wg_ref, wu_ref, wd_ref, fn_ref, o_ref, *, ff_chunk, final):
    h = h_ref[...]
    d_ff = wg_ref.shape[1]
    acc = jnp.zeros(x_ref.shape, F32)
    for c0 in range(0, d_ff, ff_chunk):
        g = _dot(h, wg_ref[:, c0:c0 + ff_chunk])
        u = _dot(h, wu_ref[:, c0:c0 + ff_chunk])
        acc = acc + _dot((_silu(g) * u).astype(BF16), wd_ref[c0:c0 + ff_chunk, :])
    x2 = x_ref[...] + g2_ref[...] * acc
    if final:
        x2 = _rms(x2) * fn_ref[...]
    o_ref[...] = x2


def _ffn(h2, x1, layer, mods, w_gate, w_up, w_down, final_norm, latent_only, final):
    batch, s, d = x1.shape
    d_ff = w_gate.shape[2]
    tm = ROW_BLOCK
    row = lambda: pl.BlockSpec((None, tm, d), lambda b, j: (b, j, 0))
    return pl.pallas_call(
        functools.partial(_ffn_kernel, ff_chunk=d_ff // 2, final=final),
        grid=(batch, s // tm),
        in_specs=[row(), row(), _mod_spec(layer, 5, batch, not latent_only),
                  _layer_spec(layer, (d, d_ff)), _layer_spec(layer, (d, d_ff)), _layer_spec(layer, (d_ff, d)),
                  pl.BlockSpec((1, d), lambda b, j: (0, 0))],
        out_specs=row(),
        out_shape=jax.ShapeDtypeStruct((batch, s, d), F32),
        compiler_params=_params(2),
    )(h2, x1, mods, w_gate, w_up, w_down, final_norm)


def _rope_tables(ctx_len, seq_len):
    t = np.arange(seq_len)
    n_freq = HEAD_DIM // 4
    inv_freq = ROPE_THETA ** (-np.arange(n_freq) / n_freq)
    ang = np.concatenate([(t // GRID_W)[:, None] * inv_freq, (t % GRID_W)[:, None] * inv_freq], axis=-1)
    cos_h = np.concatenate([np.cos(ang), np.cos(ang)], axis=-1)
    sin_h = np.concatenate([-np.sin(ang), np.sin(ang)], axis=-1)
    cos_t = np.concatenate([np.ones((ctx_len, HEAD_DIM)), cos_h], axis=0)
    sin_t = np.concatenate([np.zeros((ctx_len, HEAD_DIM)), sin_h], axis=0)
    return jnp.asarray(np.tile(cos_t, (1, 2)), F32), jnp.asarray(np.tile(sin_t, (1, 2)), F32)


def _swa_pair_order(a, axis):
    n_pairs = SWA_W // PAIR_W
    heads = [h for p in range(n_pairs) for h in (p, p + n_pairs)]
    return jnp.concatenate([lax.slice_in_dim(a, h * HEAD_DIM, (h + 1) * HEAD_DIM, axis=axis) for h in heads],
                           axis=axis)


def _permute_w_in(w):
    scale = HEAD_DIM ** -0.5
    o = np.cumsum([0, GLA_W, GLA_W, GLA_W, GLA_W, GLA_RANK, GLA_RANK, NA_W, NA_W, NA_W, SWA_W, SWA_KV_W, SWA_KV_W])
    parts = [w[..., o[0]:o[1]] * scale, w[..., o[1]:o[4]],
             w[..., o[4]:o[6]], jnp.zeros(w.shape[:2] + (LANES - 2 * GLA_RANK,), w.dtype),
             w[..., o[6]:o[7]] * scale, w[..., o[7]:o[9]],
             _swa_pair_order(w[..., o[9]:o[10]], 2) * scale, w[..., o[10]:o[12]]]
    return jnp.concatenate(parts, axis=-1).astype(BF16)


def _na_bias_table(rpb, grid_rows):
    kh = NA_KH
    class_rows = np.array([0, 1, 2, 3, kh // 2, grid_rows - 3, grid_rows - 2, grid_rows - 1])
    key_rows = np.clip(class_rows - kh // 2, 0, grid_rows - kh)[:, None] + np.arange(kh)[None, :]
    dr = key_rows - class_rows[:, None] + kh - 1
    qc = np.arange(GRID_W)
    kc = np.arange(GRID_W)
    win = np.clip(qc - NA_KW // 2, 0, GRID_W - NA_KW)
    valid = (kc[None, :] >= win[:, None]) & (kc[None, :] < win[:, None] + NA_KW)
    dc = kc[None, :] - qc[:, None] + NA_KW - 1
    pick_col = ((dc[None] == np.arange(2 * NA_KW - 1)[:, None, None]) & valid[None]).astype(np.float32)
    pick_row = (dr[:, :, None] == np.arange(2 * kh - 1)[None, None, :]).astype(np.float32)
    t = jnp.einsum('lhde,eck->lhdck', rpb.astype(F32), pick_col, precision=lax.Precision.HIGHEST)
    b = jnp.einsum('ajd,lhdck->lahcjk', pick_row, t, precision=lax.Precision.HIGHEST)
    b = jnp.where(valid[:, None, :], b, -jnp.inf)
    return b.reshape(rpb.shape[0], NA_CLASSES, NA_HEADS // 2, 2 * GRID_W, kh * GRID_W)


def _pad_rank(wa2, offset):
    return jnp.pad(wa2, ((0, 0), (offset, LANES - offset - GLA_RANK), (0, 0)))


def kernel(x, c, ctx, c_ctx, w_mod, b_mod, norm_mix, norm_ffn, w_in, gla_wa2_f, gla_ba_f, gla_wa2_b, gla_ba_b,
           gla_norm, na_rpb, swa_sink, w_out, w_gate, w_up, w_down, final_norm):
    batch, seq_len, d = x.shape
    ctx_len = ctx.shape[1]
    depth = w_mod.shape[0]
    assert d == D_MODEL and ctx_len == ROW_BLOCK and seq_len % ROW_BLOCK == 0 and batch < 8
    assert seq_len % GRID_W == 0 and seq_len // GRID_W >= 2 * NA_KH

    cc = jnp.concatenate([c, c_ctx[None, :], jnp.zeros((8 - batch - 1, d), F32)], axis=0)
    mods = _modulation(cc, w_mod, b_mod).reshape(depth, 8, 1, 6 * d)
    cos_t, sin_t = _rope_tables(ctx_len, seq_len)
    head_avg = jnp.asarray(np.kron(np.eye(GLA_HEADS), np.full((HEAD_DIM, HEAD_DIM), 1.0 / HEAD_DIM)), BF16)
    w_in_p = _permute_w_in(w_in)
    wa_f, wa_b = _pad_rank(gla_wa2_f, 0), _pad_rank(gla_wa2_b, GLA_RANK)
    ba_f, ba_b = gla_ba_f.reshape(depth, 1, GLA_W), gla_ba_b.reshape(depth, 1, GLA_W)
    na_bias = _na_bias_table(na_rpb, seq_len // GRID_W)
    w_o_gla = w_out[:, :GLA_W].astype(BF16)
    w_o_na = w_out[:, GLA_W:GLA_W + NA_W].astype(BF16)
    w_o_swa = _swa_pair_order(w_out[:, GLA_W + NA_W:], 1).astype(BF16)
    w_gate_b, w_up_b, w_down_b = w_gate.astype(BF16), w_up.astype(BF16), w_down.astype(BF16)
    norm_mix3, norm_ffn3 = norm_mix.reshape(depth, 1, d), norm_ffn.reshape(depth, 1, d)
    gla_norm3 = gla_norm.reshape(depth, 1, GLA_W)

    srcs = (ctx, x)
    for i in range(depth):
        last = i == depth - 1
        gla_in, za, na_qkv, sq, sk, sv = _in_proj(srcs, i, mods, norm_mix3, w_in_p, cos_t, sin_t)
        o_f, o_b = _gla(gla_in, za, i, wa_f, wa_b, ba_f, ba_b)
        y_na = _na(na_qkv, i, na_bias, ctx_len)
        y_swa = _swa(swa_sink, i, sq, sk, sv, ctx_len)
        x1, h2 = _out_proj(srcs, o_f, o_b, gla_in, y_na, y_swa, i, mods, norm_ffn3, gla_norm3, head_avg,
                           w_o_gla, w_o_na, w_o_swa, skip_ctx=last)
        srcs = (_ffn(h2, x1, i, mods, w_gate_b, w_up_b, w_down_b, final_norm.reshape(1, d),
                     latent_only=last, final=last),)
    return srcs[0]
```

```python
import functools

import jax
import jax.numpy as jnp
import numpy as np
from jax import lax
from jax.experimental import pallas as pl
from jax.experimental.pallas import tpu as pltpu

F32 = jnp.float32
BF16 = jnp.bfloat16

D_MODEL = 1024
HEAD_DIM = 64
GRID_W = 64
GLA_HEADS = 4
NA_HEADS = 6
SWA_HEADS = 6
SWA_KV_HEADS = 2
GLA_W = GLA_HEADS * HEAD_DIM
NA_W = NA_HEADS * HEAD_DIM
SWA_W = SWA_HEADS * HEAD_DIM
SWA_KV_W = SWA_KV_HEADS * HEAD_DIM
GLA_RANK = 16
GLA_TAU = 16.0
GLA_CHUNK = 64
NA_KH = 8
NA_KW = 16
SWA_WINDOW = 128
ROPE_THETA = 10000.0
NORM_EPS = 1e-6

LANES = 128
ROW_BLOCK = 256
PAIR_W = 2 * HEAD_DIM
LOG2E = float(np.log2(np.e))
VMEM_LIMIT = 56 * 1024 * 1024

C_GLA = 0
C_ZA = C_GLA + 4 * GLA_W
C_NA = C_ZA + LANES
C_SQ = C_NA + 3 * NA_W
C_SK = C_SQ + SWA_W
C_SV = C_SK + SWA_KV_W
IN_PERM_W = C_SV + SWA_KV_W


def _dot(a, b):
    return jnp.dot(a, b, preferred_element_type=F32)


def _dot_nt(a, b):
    return lax.dot_general(a, b, (((1,), (1,)), ((), ())), preferred_element_type=F32)


def _dot_tn(a, b):
    return lax.dot_general(a, b, (((0,), (0,)), ((), ())), preferred_element_type=F32)


def _idiv(a, n):
    assert n & (n - 1) == 0
    return a >> (n.bit_length() - 1)


def _imod(a, n):
    assert n & (n - 1) == 0
    return a & (n - 1)


def _split2(a):
    hi = a.astype(BF16)
    lo = (a - hi.astype(F32)).astype(BF16)
    return hi, lo


def _split3(a):
    hi = a.astype(BF16)
    r = a - hi.astype(F32)
    mid = r.astype(BF16)
    lo = (r - mid.astype(F32)).astype(BF16)
    return hi, mid, lo


def _dot_split(a, b):
    ah, al = _split2(a)
    bh, bl = _split2(b)
    return _dot(ah, bh) + _dot(al, bh) + _dot(ah, bl)


def _silu(a):
    return a * jax.nn.sigmoid(a)


def _rms(x):
    return x * lax.rsqrt(jnp.mean(x * x, axis=-1, keepdims=True) + NORM_EPS)


def _params(n_parallel, n_arbitrary=0):
    return pltpu.CompilerParams(
        dimension_semantics=("parallel",) * n_parallel + ("arbitrary",) * n_arbitrary,
        vmem_limit_bytes=VMEM_LIMIT)


def _mod_kernel(c_ref, w_ref, b_ref, o_ref):
    o_ref[...] = _dot_split(_silu(c_ref[...]), w_ref[...]) + b_ref[...]


def _modulation(cc, w_mod, b_mod):
    depth, d, n = w_mod.shape
    tn = 1536
    return pl.pallas_call(
        _mod_kernel,
        grid=(depth, n // tn),
        in_specs=[
            pl.BlockSpec((8, d), lambda i, k: (0, 0)),
            pl.BlockSpec((None, d, tn), lambda i, k: (i, 0, k)),
            pl.BlockSpec((None, 1, tn), lambda i, k: (i, 0, k)),
        ],
        out_specs=pl.BlockSpec((None, 8, tn), lambda i, k: (i, 0, k)),
        out_shape=jax.ShapeDtypeStruct((depth, 8, n), F32),
        compiler_params=_params(2),
    )(cc, w_mod, b_mod.reshape(depth, 1, n))


def _mod_spec(layer, chunk, batch, ctx_first):
    if ctx_first:
        return pl.BlockSpec((None, None, 1, D_MODEL), lambda b, j: (layer, jnp.where(j == 0, batch, b), 0, chunk))
    return pl.BlockSpec((None, None, 1, D_MODEL), lambda b, j: (layer, b, 0, chunk))


def _layer_spec(layer, shape):
    return pl.BlockSpec((None,) + shape, lambda b, j: (layer,) + (0,) * len(shape))


def _token_rows(src_refs):
    if len(src_refs) == 1:
        return src_refs[0][...]
    ctx_ref, x_ref = src_refs
    return jnp.where(pl.program_id(1) == 0, ctx_ref[...], x_ref[...])


def _token_specs(srcs, tm, off=0):
    d = srcs[0].shape[-1]
    if len(srcs) == 1:
        return [pl.BlockSpec((None, tm, d), lambda b, j: (b, j + off, 0))]
    assert off == 0
    return [pl.BlockSpec((None, tm, d), lambda b, j: (b, 0, 0)),
            pl.BlockSpec((None, tm, d), lambda b, j: (b, jnp.maximum(j - 1, 0), 0))]


def _in_proj_kernel(*refs, n_src):
    src_refs = refs[:n_src]
    sh_ref, sc_ref, nw_ref, w_ref, cos_ref, sin_ref, gla_ref, za_ref, na_ref, sq_ref, sk_ref, sv_ref = refs[n_src:]
    x = _token_rows(src_refs)
    tm = x.shape[0]
    lane = lax.broadcasted_iota(jnp.int32, (tm // 2, LANES), 1)
    first_half = _imod(lane, HEAD_DIM) < (HEAD_DIM // 2)
    n_q = SWA_W // LANES
    for rows in (slice(0, tm // 2), slice(tm // 2, tm)):
        h = _rms(x[rows]) * nw_ref[...]
        h = (h * (1.0 + sc_ref[...]) + sh_ref[...]).astype(BF16)
        qk = _dot(h, w_ref[:, C_SQ:C_SV])
        cos = cos_ref[rows, :]
        sin = sin_ref[rows, :]
        for cb in range(n_q + 1):
            a = qk[:, cb * LANES:(cb + 1) * LANES]
            partner = jnp.where(first_half, pltpu.roll(a, LANES - HEAD_DIM // 2, 1), pltpu.roll(a, HEAD_DIM // 2, 1))
            r = (a * cos + partner * sin).astype(BF16)
            if cb < n_q:
                sq_ref[rows, cb * LANES:(cb + 1) * LANES] = r
            else:
                sk_ref[rows, :] = r
        gla_ref[rows, :] = _dot(h, w_ref[:, C_GLA:C_ZA])
        za_ref[rows, :] = _dot(h, w_ref[:, C_ZA:C_NA])
        na_ref[rows, :] = _dot(h, w_ref[:, C_NA:C_SQ]).astype(BF16)
        sv_ref[rows, :] = _dot(h, w_ref[:, C_SV:IN_PERM_W]).astype(BF16)


def _in_proj(srcs, layer, mods, norm_w, w_perm, cos_t, sin_t):
    batch, d = srcs[0].shape[0], srcs[0].shape[2]
    s = sum(a.shape[1] for a in srcs)
    tm = ROW_BLOCK
    row = lambda w: pl.BlockSpec((None, tm, w), lambda b, j: (b, j, 0))
    tab = pl.BlockSpec((tm, LANES), lambda b, j: (j, 0))
    return pl.pallas_call(
        functools.partial(_in_proj_kernel, n_src=len(srcs)),
        grid=(batch, s // tm),
        in_specs=_token_specs(srcs, tm) + [
            _mod_spec(layer, 0, batch, True), _mod_spec(layer, 1, batch, True), _layer_spec(layer, (1, d)),
            _layer_spec(layer, (d, IN_PERM_W)), tab, tab],
        out_specs=[row(4 * GLA_W), row(LANES), row(3 * NA_W), row(SWA_W), row(SWA_KV_W), row(SWA_KV_W)],
        out_shape=[
            jax.ShapeDtypeStruct((batch, s, 4 * GLA_W), F32),
            jax.ShapeDtypeStruct((batch, s, LANES), F32),
            jax.ShapeDtypeStruct((batch, s, 3 * NA_W), BF16),
            jax.ShapeDtypeStruct((batch, s, SWA_W), BF16),
            jax.ShapeDtypeStruct((batch, s, SWA_KV_W), BF16),
            jax.ShapeDtypeStruct((batch, s, SWA_KV_W), BF16),
        ],
        compiler_params=_params(2),
    )(*srcs, mods, mods, norm_w, w_perm, cos_t, sin_t)


def _gla_kernel(qf_ref, kf_ref, vf_ref, zf_ref, qb_ref, kb_ref, vb_ref, zb_ref,
                waf_ref, wab_ref, baf_ref, bab_ref, of_ref, ob_ref, stf_ref, stb_ref):
    @pl.when(pl.program_id(1) == 0)
    def _():
        stf_ref[...] = jnp.zeros_like(stf_ref)
        stb_ref[...] = jnp.zeros_like(stb_ref)

    c = GLA_CHUNK
    w = GLA_W
    n_chunks = ROW_BLOCK // c
    r64 = lax.broadcasted_iota(jnp.int32, (c, c), 0)
    c64 = lax.broadcasted_iota(jnp.int32, (c, c), 1)
    row_h = _idiv(lax.broadcasted_iota(jnp.int32, (GLA_HEADS * c, w), 0), c)
    col_h = _idiv(lax.broadcasted_iota(jnp.int32, (GLA_HEADS * c, w), 1), c)
    same_head = row_h == col_h
    ar = _imod(lax.broadcasted_iota(jnp.int32, (GLA_HEADS * c, c), 0), c)
    ac = lax.broadcasted_iota(jnp.int32, (GLA_HEADS * c, c), 1)
    out_head = _idiv(lax.broadcasted_iota(jnp.int32, (c, w), 1), c)
    dirs = [
        dict(q=qf_ref, k=kf_ref, v=vf_ref, z=zf_ref, wa=waf_ref, ba=baf_ref, o=of_ref, st=stf_ref, rev=False,
             order=range(n_chunks), cum=(r64 >= c64).astype(BF16), causal=ar >= ac),
        dict(q=qb_ref, k=kb_ref, v=vb_ref, z=zb_ref, wa=wab_ref, ba=bab_ref, o=ob_ref, st=stb_ref, rev=True,
             order=range(n_chunks - 1, -1, -1), cum=(r64 <= c64).astype(BF16), causal=ar <= ac),
    ]
    units = [(d, ci) for d in dirs for ci in d['order']]
    rows = lambda ci: slice(ci * c, (ci + 1) * c)

    log_a = []
    for d in dirs:
        z = _dot_split(d['z'][...], d['wa'][...]) + d['ba'][...]
        log_a.append((jnp.minimum(z, 0.0) - jnp.log1p(jnp.exp(-jnp.abs(z)))) * (1.0 / GLA_TAU))
    b, b_tot = [], []
    for (d, ci), la in zip(units, [la for la in log_a for _ in range(n_chunks)]):
        hi, mid, lo = _split3(la[rows(ci)])
        b.append(_dot(d['cum'], hi) + _dot(d['cum'], mid) + _dot(d['cum'], lo))
        b_tot.append(b[-1][0:1] if d['rev'] else b[-1][c - 1:c])
    k = [d['k'][rows(ci), :] for d, ci in units]
    v = [d['v'][rows(ci), :].astype(BF16) for d, ci in units]
    q_in = [(d['q'][rows(ci), :] * jnp.exp(bu)).astype(BF16) for (d, ci), bu in zip(units, b)]
    k_in = [(ku * jnp.exp(-bu)).astype(BF16) for ku, bu in zip(k, b)]
    k_end = [(ku * jnp.exp(bt - bu)).astype(BF16) for ku, bu, bt in zip(k, b, b_tot)]
    qx = [jnp.where(same_head, jnp.concatenate([qu] * GLA_HEADS, axis=0), jnp.zeros((), BF16)) for qu in q_in]
    a = [jnp.where(d['causal'], _dot_nt(qxu, ku), 0.0).astype(BF16) for (d, _), qxu, ku in zip(units, qx, k_in)]
    r = [_dot(au, vu) for au, vu in zip(a, v)]
    upd = [jnp.where(same_head, _dot_tn(vu, ku), 0.0) for vu, ku in zip(v, k_end)]
    st_enter = []
    for di, d in enumerate(dirs):
        st = d['st'][...]
        for u in range(di * n_chunks, (di + 1) * n_chunks):
            st_enter.append(st.astype(BF16))
            st = st * jnp.exp(b_tot[u]) + upd[u]
        d['st'][...] = st
    for (d, ci), qu, su, ru in zip(units, q_in, st_enter, r):
        o = _dot_nt(qu, su)
        for h in range(GLA_HEADS):
            o = o + jnp.where(out_head == h, ru[h * c:(h + 1) * c], 0.0)
        d['o'][rows(ci), :] = o


def _gla(gla_in, za, layer, wa_f, wa_b, ba_f, ba_b):
    batch, s, _ = gla_in.shape
    tm = ROW_BLOCK
    nblk = s // tm
    fwd = lambda j: j
    bwd = lambda j: jnp.where(j == 0, 0, nblk - j)

    def col(order, cb):
        return pl.BlockSpec((None, tm, GLA_W), lambda b, j: (b, order(j), cb))

    def zspec(order):
        return pl.BlockSpec((None, tm, LANES), lambda b, j: (b, order(j), 0))

    return pl.pallas_call(
        _gla_kernel,
        grid=(batch, nblk),
        in_specs=[col(fwd, 0), col(fwd, 1), col(fwd, 2), zspec(fwd),
                  col(bwd, 0), col(bwd, 1), col(bwd, 2), zspec(bwd),
                  _layer_spec(layer, (LANES, GLA_W)), _layer_spec(layer, (LANES, GLA_W)),
                  _layer_spec(layer, (1, GLA_W)), _layer_spec(layer, (1, GLA_W))],
        out_specs=[col(fwd, 0), col(bwd, 0)],
        out_shape=[jax.ShapeDtypeStruct((batch, s, GLA_W), F32)] * 2,
        scratch_shapes=[pltpu.VMEM((GLA_W, GLA_W), F32)] * 2,
        compiler_params=_params(1, 1),
    )(gla_in, gla_in, gla_in, za, gla_in, gla_in, gla_in, za, wa_f, wa_b, ba_f, ba_b)


def _stack_pair(q):
    left = lax.broadcasted_iota(jnp.int32, q.shape, 1) < HEAD_DIM
    zero = jnp.zeros((), q.dtype)
    return jnp.concatenate([jnp.where(left, q, zero), jnp.where(left, zero, q)], axis=0)


def _unstack_pair(o):
    n = o.shape[0] // 2
    left = lax.broadcasted_iota(jnp.int32, (n, o.shape[1]), 1) < HEAD_DIM
    return jnp.where(left, o[:n], o[n:])


def _softmax_weights(scores, extra_logit=None):
    m = functools.reduce(jnp.maximum, [jnp.max(s, axis=-1, keepdims=True) for s in scores])
    if extra_logit is not None:
        m = jnp.maximum(m, extra_logit)
    ps = [jnp.exp2(s - m) for s in scores]
    denom = functools.reduce(jnp.add, [jnp.sum(p, axis=-1, keepdims=True) for p in ps])
    if extra_logit is not None:
        denom = denom + jnp.exp2(extra_logit - m)
    return [p.astype(BF16) for p in ps], denom


def _softmax_pv(scores, values, extra_logit=None):
    ps, denom = _softmax_weights(scores, extra_logit)
    return functools.reduce(jnp.add, [_dot(p, v) for p, v in zip(ps, values)]) / denom


def _na_kernel(q_ref, k_ref, v_ref, bias_ref, o_ref, *, ctx_len, grid_rows, first_block):
    j = pl.program_id(1) + first_block
    n_pairs = NA_W // PAIR_W
    rows_per_block = ROW_BLOCK // GRID_W
    n_loc = NA_KH * GRID_W
    unit = 2 * GRID_W
    pair_cols = [slice(p * PAIR_W, (p + 1) * PAIR_W) for p in range(n_pairs)]
    units = [(rr, p) for p in range(n_pairs) for rr in range(rows_per_block)]
    qx = {(rr, p): _stack_pair(q_ref[rr * GRID_W:(rr + 1) * GRID_W, pair_cols[p]]) for rr, p in units}
    s_ctx = [_dot_nt(jnp.concatenate([qx[rr, p] for rr in range(rows_per_block)], axis=0),
                     k_ref[0:ctx_len, pair_cols[p]]) for p in range(n_pairs)]

    def ctx_out(w):
        return [_dot(jnp.concatenate([w[rr, p][0][-1] for rr in range(rows_per_block)], axis=0),
                     v_ref[0:ctx_len, pair_cols[p]]) for p in range(n_pairs)]

    def finish(o, w):
        for rr, p in units:
            o_ref[rr * GRID_W:(rr + 1) * GRID_W, pair_cols[p]] = _unstack_pair(o[rr, p] / w[rr, p][1]).astype(
                o_ref.dtype)

    def context_block():
        w = {(rr, p): _softmax_weights([s_ctx[p][rr * unit:(rr + 1) * unit]]) for rr, p in units}
        o_ctx = ctx_out(w)
        finish({(rr, p): o_ctx[p][rr * unit:(rr + 1) * unit] for rr, p in units}, w)

    def latent_block():
        k0, bias = [], {}
        for rr in range(rows_per_block):
            r = (j - 1) * rows_per_block + rr
            first_key_row = jnp.clip(r - NA_KH // 2, 0, grid_rows - NA_KH)
            k0.append(pl.multiple_of(ctx_len + first_key_row * GRID_W, GRID_W))
            d0 = first_key_row - r + NA_KH - 1
            for p in range(n_pairs):
                bias[rr, p] = jnp.concatenate(
                    [jnp.concatenate([bias_ref[2 * p + side, d0 + 2 * m] for m in range(NA_KH // 2)], axis=1)
                     for side in (0, 1)], axis=0)
        s_loc = {(rr, p): _dot_nt(qx[rr, p], k_ref[pl.ds(k0[rr], n_loc), pair_cols[p]]) + bias[rr, p]
                 for rr, p in units}
        w = {(rr, p): _softmax_weights([s_loc[rr, p], s_ctx[p][rr * unit:(rr + 1) * unit]]) for rr, p in units}
        o_ctx = ctx_out(w)
        o = {(rr, p): _dot(w[rr, p][0][0], v_ref[pl.ds(k0[rr], n_loc), pair_cols[p]])
             + o_ctx[p][rr * unit:(rr + 1) * unit] for rr, p in units}
        finish(o, w)

    if first_block == 0:
        pl.when(j == 0)(context_block)
        pl.when(j > 0)(latent_block)
    else:
        latent_block()


def _na(na_qkv, layer, bias, ctx_len, skip_ctx):
    batch, s, _ = na_qkv.shape
    tm = ROW_BLOCK
    first = 1 if skip_ctx else 0
    grid_rows = (s - ctx_len) // GRID_W
    full = lambda cb: pl.BlockSpec((None, s, NA_W), lambda b, j: (b, 0, cb))
    return pl.pallas_call(
        functools.partial(_na_kernel, ctx_len=ctx_len, grid_rows=grid_rows, first_block=first),
        grid=(batch, s // tm - first),
        in_specs=[pl.BlockSpec((None, tm, NA_W), lambda b, j: (b, j + first, 0)), full(1), full(2),
                  _layer_spec(layer, bias.shape[1:])],
        out_specs=pl.BlockSpec((None, tm, NA_W), lambda b, j: (b, j, 0)),
        out_shape=jax.ShapeDtypeStruct((batch, s - first * tm, NA_W), BF16),
        compiler_params=_params(2),
    )(na_qkv, na_qkv, na_qkv, bias)


def _swa_kernel(sinks_ref, q_ref, k_ref, v_ref, o_ref, *, layer, ctx_len, seq_len, first_block):
    j = pl.program_id(1) + first_block
    tm = ROW_BLOCK
    n_pairs = SWA_W // PAIR_W
    n_loc = tm + 2 * SWA_WINDOW
    pair_cols = [slice(p * PAIR_W, (p + 1) * PAIR_W) for p in range(n_pairs)]
    lane = lax.broadcasted_iota(jnp.int32, (tm, PAIR_W), 1)
    zero = jnp.zeros((), BF16)
    units = [(p + side * n_pairs, p, side) for p in range(n_pairs) for side in (0, 1)]
    qx = [jnp.where((lane < HEAD_DIM) == (side == 0), q_ref[:, pair_cols[p]], zero) for _, p, side in units]
    sink = [sinks_ref[layer, h] * LOG2E for h, _, _ in units]
    k_ctx = k_ref[0:ctx_len, :]
    v_ctx = v_ref[0:ctx_len, :]

    def finish(o):
        for p, cols in enumerate(pair_cols):
            o_ref[:, cols] = jnp.where(lane < HEAD_DIM, o[2 * p], o[2 * p + 1]).astype(o_ref.dtype)

    def context_block():
        s_ctx = [_dot_nt(q, k_ctx) for q in qx]
        w = [_softmax_weights([s], sk) for s, sk in zip(s_ctx, sink)]
        finish([_dot(ps[0], v_ctx) / denom for ps, denom in w])

    def latent_block():
        q0 = (j - 1) * tm
        k0 = jnp.clip(q0 - SWA_WINDOW, 0, seq_len - n_loc)
        qpos = q0 + lax.broadcasted_iota(jnp.int32, (tm, n_loc), 0)
        kpos = k0 + lax.broadcasted_iota(jnp.int32, (tm, n_loc), 1)
        in_window = jnp.abs(qpos - kpos) <= SWA_WINDOW
        start = pl.multiple_of(ctx_len + k0, SWA_WINDOW)
        k_loc = k_ref[pl.ds(start, n_loc), :]
        v_loc = v_ref[pl.ds(start, n_loc), :]
        s, w, o = {}, {}, {}
        for t in range(len(units) + 2):
            if t < len(units):
                s[t] = [jnp.where(in_window, _dot_nt(qx[t], k_loc), -jnp.inf), _dot_nt(qx[t], k_ctx)]
            if 0 <= t - 1 < len(units):
                w[t - 1] = _softmax_weights(s.pop(t - 1), sink[t - 1])
            if 0 <= t - 2 < len(units):
                ps, denom = w.pop(t - 2)
                o[t - 2] = (_dot(ps[0], v_loc) + _dot(ps[1], v_ctx)) / denom
        finish([o[u] for u in range(len(units))])

    if first_block == 0:
        pl.when(j == 0)(context_block)
        pl.when(j > 0)(latent_block)
    else:
        latent_block()


def _swa(sink, layer, sq, sk, sv, ctx_len, skip_ctx):
    batch, s, _ = sq.shape
    tm = ROW_BLOCK
    first = 1 if skip_ctx else 0
    full = pl.BlockSpec((None, s, SWA_KV_W), lambda b, j: (b, 0, 0))
    return pl.pallas_call(
        functools.partial(_swa_kernel, layer=layer, ctx_len=ctx_len, seq_len=s - ctx_len, first_block=first),
        grid=(batch, s // tm - first),
        in_specs=[pl.BlockSpec(memory_space=pltpu.SMEM),
                  pl.BlockSpec((None, tm, SWA_W), lambda b, j: (b, j + first, 0)), full, full],
        out_specs=pl.BlockSpec((None, tm, SWA_W), lambda b, j: (b, j, 0)),
        out_shape=jax.ShapeDtypeStruct((batch, s - first * tm, SWA_W), BF16),
        compiler_params=_params(2),
    )(sink, sq, sk, sv)


def _out_proj_kernel(*refs, n_src):
    src_refs = refs[:n_src]
    (of_ref, ob_ref, gate_ref, na_ref, swa_ref, g1_ref, sh2_ref, sc2_ref,
     nffn_ref, gn_ref, avg_ref, wg_ref, wn_ref, ws_ref, x1_ref, h2_ref) = refs[n_src:]
    o = of_ref[...] + ob_ref[...]
    hi, lo = _split2(o * o)
    avg = avg_ref[...]
    ms = _dot(hi, avg) + _dot(lo, avg)
    y = o * lax.rsqrt(ms + NORM_EPS) * gn_ref[...] * _silu(gate_ref[...])
    acc = _dot(y.astype(BF16), wg_ref[...]) + _dot(na_ref[...], wn_ref[...]) + _dot(swa_ref[...], ws_ref[...])
    x1 = _token_rows(src_refs) + g1_ref[...] * acc
    x1_ref[...] = x1
    h2 = _rms(x1) * nffn_ref[...]
    h2_ref[...] = (h2 * (1.0 + sc2_ref[...]) + sh2_ref[...]).astype(BF16)


def _out_proj(srcs, o_f, o_b, gla_in, y_na, y_swa, layer, mods, norm_ffn, gla_norm, avg, w_g, w_n, w_s, skip_ctx):
    batch, s, _ = o_f.shape
    d = srcs[0].shape[2]
    tm = ROW_BLOCK
    off = 1 if skip_ctx else 0
    nblk = s // tm - off
    tok_off = off
    if skip_ctx and len(srcs) == 2:
        srcs, tok_off = srcs[1:], 0
    row = lambda w, cb=0: pl.BlockSpec((None, tm, w), lambda b, j: (b, j + off, cb))
    out = lambda w: pl.BlockSpec((None, tm, w), lambda b, j: (b, j, 0))
    ms = lambda chunk: _mod_spec(layer, chunk, batch, not skip_ctx)
    return pl.pallas_call(
        functools.partial(_out_proj_kernel, n_src=len(srcs)),
        grid=(batch, nblk),
        in_specs=_token_specs(srcs, tm, tok_off) + [
            row(GLA_W), row(GLA_W), row(GLA_W, 3), out(NA_W), out(SWA_W),
            ms(2), ms(3), ms(4), _layer_spec(layer, (1, d)), _layer_spec(layer, (1, GLA_W)),
            pl.BlockSpec((GLA_W, GLA_W), lambda b, j: (0, 0)),
            _layer_spec(layer, (GLA_W, d)), _layer_spec(layer, (NA_W, d)), _layer_spec(layer, (SWA_W, d))],
        out_specs=[out(d), out(d)],
        out_shape=[jax.ShapeDtypeStruct((batch, nblk * tm, d), F32),
                   jax.ShapeDtypeStruct((batch, nblk * tm, d), BF16)],
        compiler_params=_params(2),
    )(*srcs, o_f, o_b, gla_in, y_na, y_swa, mods, mods, mods, norm_ffn, gla_norm, avg, w_g, w_n, w_s)


def _ffn_kernel(h_ref, x_ref, g2_ref, wg_ref, wu_ref, wd_ref, fn_ref, o_ref, *, ff_chunk, final):
    h = h_ref[...]
    d_ff = wg_ref.shape[1]
    acc = jnp.zeros(x_ref.shape, F32)
    for c0 in range(0, d_ff, ff_chunk):
        g = _dot(h, wg_ref[:, c0:c0 + ff_chunk])
        u = _dot(h, wu_ref[:, c0:c0 + ff_chunk])
        acc = acc + _dot((_silu(g) * u).astype(BF16), wd_ref[c0:c0 + ff_chunk, :])
    x2 = x_ref[...] + g2_ref[...] * acc
    if final:
        x2 = _rms(x2) * fn_ref[...]
    o_ref[...] = x2


def _ffn(h2, x1, layer, mods, w_gate, w_up, w_down, final_norm, latent_only, final):
    batch, s, d = x1.shape
    d_ff = w_gate.shape[2]
    tm = ROW_BLOCK
    row = lambda: pl.BlockSpec((None, tm, d), lambda b, j: (b, j, 0))
    return pl.pallas_call(
        functools.partial(_ffn_kernel, ff_chunk=d_ff // 2, final=final),
        grid=(batch, s // tm),
        in_specs=[row(), row(), _mod_spec(layer, 5, batch, not latent_only),
                  _layer_spec(layer, (d, d_ff)), _layer_spec(layer, (d, d_ff)), _layer_spec(layer, (d_ff, d)),
                  pl.BlockSpec((1, d), lambda b, j: (0, 0))],
        out_specs=row(),
        out_shape=jax.ShapeDtypeStruct((batch, s, d), F32),
        compiler_params=_params(2),
    )(h2, x1, mods, w_gate, w_up, w_down, final_norm)


def _rope_tables(ctx_len, seq_len):
    t = np.arange(seq_len)
    n_freq = HEAD_DIM // 4
    inv_freq = ROPE_THETA ** (-np.arange(n_freq) / n_freq)
    ang = np.concatenate([(t // GRID_W)[:, None] * inv_freq, (t % GRID_W)[:, None] * inv_freq], axis=-1)
    cos_h = np.concatenate([np.cos(ang), np.cos(ang)], axis=-1)
    sin_h = np.concatenate([-np.sin(ang), np.sin(ang)], axis=-1)
    cos_t = np.concatenate([np.ones((ctx_len, HEAD_DIM)), cos_h], axis=0)
    sin_t = np.concatenate([np.zeros((ctx_len, HEAD_DIM)), sin_h], axis=0)
    return jnp.asarray(np.tile(cos_t, (1, 2)), F32), jnp.asarray(np.tile(sin_t, (1, 2)), F32)


def _swa_pair_order(a, axis):
    n_pairs = SWA_W // PAIR_W
    heads = [h for p in range(n_pairs) for h in (p, p + n_pairs)]
    return jnp.concatenate([lax.slice_in_dim(a, h * HEAD_DIM, (h + 1) * HEAD_DIM, axis=axis) for h in heads],
                           axis=axis)


def _permute_w_in(w):
    scale = HEAD_DIM ** -0.5
    scale2 = scale * LOG2E
    o = np.cumsum([0, GLA_W, GLA_W, GLA_W, GLA_W, GLA_RANK, GLA_RANK, NA_W, NA_W, NA_W, SWA_W, SWA_KV_W, SWA_KV_W])
    parts = [w[..., o[0]:o[1]] * scale, w[..., o[1]:o[4]],
             w[..., o[4]:o[6]], jnp.zeros(w.shape[:2] + (LANES - 2 * GLA_RANK,), w.dtype),
             w[..., o[6]:o[7]] * scale2, w[..., o[7]:o[9]],
             _swa_pair_order(w[..., o[9]:o[10]], 2) * scale2, w[..., o[10]:o[12]]]
    return jnp.concatenate(parts, axis=-1).astype(BF16)


def _na_bias_table(rpb):
    qc = np.arange(GRID_W)
    kc = np.arange(GRID_W)
    win = np.clip(qc - NA_KW // 2, 0, GRID_W - NA_KW)
    valid = (kc[None, :] >= win[:, None]) & (kc[None, :] < win[:, None] + NA_KW)
    dc = kc[None, :] - qc[:, None] + NA_KW - 1
    pick_col = ((dc[None] == np.arange(2 * NA_KW - 1)[:, None, None]) & valid[None]).astype(np.float32)
    t = jnp.einsum('lhde,eck->lhdck', rpb.astype(F32) * LOG2E, pick_col, precision=lax.Precision.HIGHEST)
    t = jnp.where(valid, t, -jnp.inf)
    return jnp.concatenate([t[:, :, :-1], t[:, :, 1:]], axis=-1)


def _pad_rank(wa2, offset):
    return jnp.pad(wa2, ((0, 0), (offset, LANES - offset - GLA_RANK), (0, 0)))


def kernel(x, c, ctx, c_ctx, w_mod, b_mod, norm_mix, norm_ffn, w_in, gla_wa2_f, gla_ba_f, gla_wa2_b, gla_ba_b,
           gla_norm, na_rpb, swa_sink, w_out, w_gate, w_up, w_down, final_norm):
    batch, seq_len, d = x.shape
    ctx_len = ctx.shape[1]
    depth = w_mod.shape[0]
    assert d == D_MODEL and ctx_len == ROW_BLOCK and seq_len % ROW_BLOCK == 0 and batch < 8
    assert seq_len % GRID_W == 0 and seq_len // GRID_W >= 2 * NA_KH

    cc = jnp.concatenate([c, c_ctx[None, :], jnp.zeros((8 - batch - 1, d), F32)], axis=0)
    mods = _modulation(cc, w_mod, b_mod).reshape(depth, 8, 1, 6 * d)
    cos_t, sin_t = _rope_tables(ctx_len, seq_len)
    head_avg = jnp.asarray(np.kron(np.eye(GLA_HEADS), np.full((HEAD_DIM, HEAD_DIM), 1.0 / HEAD_DIM)), BF16)
    w_in_p = _permute_w_in(w_in)
    wa_f, wa_b = _pad_rank(gla_wa2_f, 0), _pad_rank(gla_wa2_b, GLA_RANK)
    ba_f, ba_b = gla_ba_f.reshape(depth, 1, GLA_W), gla_ba_b.reshape(depth, 1, GLA_W)
    na_bias = _na_bias_table(na_rpb)
    w_o_gla = w_out[:, :GLA_W].astype(BF16)
    w_o_na = w_out[:, GLA_W:GLA_W + NA_W].astype(BF16)
    w_o_swa = _swa_pair_order(w_out[:, GLA_W + NA_W:], 1).astype(BF16)
    w_gate_b, w_up_b, w_down_b = w_gate.astype(BF16), w_up.astype(BF16), w_down.astype(BF16)
    norm_mix3, norm_ffn3 = norm_mix.reshape(depth, 1, d), norm_ffn.reshape(depth, 1, d)
    gla_norm3 = gla_norm.reshape(depth, 1, GLA_W)

    srcs = (ctx, x)
    for i in range(depth):
        last = i == depth - 1
        gla_in, za, na_qkv, sq, sk, sv = _in_proj(srcs, i, mods, norm_mix3, w_in_p, cos_t, sin_t)
        o_f, o_b = _gla(gla_in, za, i, wa_f, wa_b, ba_f, ba_b)
        y_na = _na(na_qkv, i, na_bias, ctx_len, skip_ctx=last)
        y_swa = _swa(swa_sink, i, sq, sk, sv, ctx_len, skip_ctx=last)
        x1, h2 = _out_proj(srcs, o_f, o_b, gla_in, y_na, y_swa, i, mods, norm_ffn3, gla_norm3, head_avg,
                           w_o_gla, w_o_na, w_o_swa, skip_ctx=last)
        srcs = (_ffn(h2, x1, i, mods, w_gate_b, w_up_b, w_down_b, final_norm.reshape(1, d),
                     latent_only=last, final=last),)
    return srcs[0]
```

```python
import functools

import jax
import jax.numpy as jnp
import numpy as np
from jax import lax
from jax.experimental import pallas as pl
from jax.experimental.pallas import tpu as pltpu

F32 = jnp.float32
BF16 = jnp.bfloat16

D_MODEL = 1024
HEAD_DIM = 64
GRID_W = 64
GLA_HEADS = 4
NA_HEADS = 6
SWA_HEADS = 6
SWA_KV_HEADS = 2
GLA_W = GLA_HEADS * HEAD_DIM
NA_W = NA_HEADS * HEAD_DIM
SWA_W = SWA_HEADS * HEAD_DIM
SWA_KV_W = SWA_KV_HEADS * HEAD_DIM
GLA_RANK = 16
GLA_TAU = 16.0
GLA_CHUNK = 64
NA_KH = 8
NA_KW = 16
SWA_WINDOW = 128
ROPE_THETA = 10000.0
NORM_EPS = 1e-6

LANES = 128
ROW_BLOCK = 256
PAIR_W = 2 * HEAD_DIM
LOG2E = float(np.log2(np.e))
FF_CHUNK = 256
VMEM_LIMIT = 56 * 1024 * 1024

C_GLA = 0
C_ZA = C_GLA + 4 * GLA_W
C_NA = C_ZA + LANES
C_SQ = C_NA + 3 * NA_W
C_SK = C_SQ + SWA_W
C_SV = C_SK + SWA_KV_W
IN_PERM_W = C_SV + SWA_KV_W


def _dot(a, b):
    return jnp.dot(a, b, preferred_element_type=F32)


def _dot_nt(a, b):
    return lax.dot_general(a, b, (((1,), (1,)), ((), ())), preferred_element_type=F32)


def _dot_tn(a, b):
    return lax.dot_general(a, b, (((0,), (0,)), ((), ())), preferred_element_type=F32)


def _idiv(a, n):
    assert n & (n - 1) == 0
    return a >> (n.bit_length() - 1)


def _imod(a, n):
    assert n & (n - 1) == 0
    return a & (n - 1)


def _split2(a):
    hi = a.astype(BF16)
    lo = (a - hi.astype(F32)).astype(BF16)
    return hi, lo


def _split3(a):
    hi = a.astype(BF16)
    r = a - hi.astype(F32)
    mid = r.astype(BF16)
    lo = (r - mid.astype(F32)).astype(BF16)
    return hi, mid, lo


def _dot_split(a, b):
    ah, al = _split2(a)
    bh, bl = _split2(b)
    return _dot(ah, bh) + _dot(al, bh) + _dot(ah, bl)


def _silu(a):
    return a * jax.nn.sigmoid(a)


def _rms(x):
    return x * lax.rsqrt(jnp.mean(x * x, axis=-1, keepdims=True) + NORM_EPS)


def _params(n_parallel, n_arbitrary=0):
    return pltpu.CompilerParams(
        dimension_semantics=("parallel",) * n_parallel + ("arbitrary",) * n_arbitrary,
        vmem_limit_bytes=VMEM_LIMIT)


def _mod_kernel(c_ref, w_ref, b_ref, o_ref):
    o_ref[...] = _dot_split(_silu(c_ref[...]), w_ref[...]) + b_ref[...]


def _modulation(cc, w_mod, b_mod):
    depth, d, n = w_mod.shape
    tn = 1536
    return pl.pallas_call(
        _mod_kernel,
        grid=(depth, n // tn),
        in_specs=[
            pl.BlockSpec((8, d), lambda i, k: (0, 0)),
            pl.BlockSpec((None, d, tn), lambda i, k: (i, 0, k)),
            pl.BlockSpec((None, 1, tn), lambda i, k: (i, 0, k)),
        ],
        out_specs=pl.BlockSpec((None, 8, tn), lambda i, k: (i, 0, k)),
        out_shape=jax.ShapeDtypeStruct((depth, 8, n), F32),
        compiler_params=_params(2),
    )(cc, w_mod, b_mod.reshape(depth, 1, n))


def _mod_spec(layer, chunk, batch, ctx_first):
    if ctx_first:
        return pl.BlockSpec((None, None, 1, D_MODEL), lambda b, j: (layer, jnp.where(j == 0, batch, b), 0, chunk))
    return pl.BlockSpec((None, None, 1, D_MODEL), lambda b, j: (layer, b, 0, chunk))


def _layer_spec(layer, shape):
    return pl.BlockSpec((None,) + shape, lambda b, j: (layer,) + (0,) * len(shape))


def _token_rows(src_refs):
    if len(src_refs) == 1:
        return src_refs[0][...]
    ctx_ref, x_ref = src_refs
    return jnp.where(pl.program_id(1) == 0, ctx_ref[...], x_ref[...])


def _token_specs(srcs, tm, off=0):
    d = srcs[0].shape[-1]
    if len(srcs) == 1:
        return [pl.BlockSpec((None, tm, d), lambda b, j: (b, j + off, 0))]
    assert off == 0
    return [pl.BlockSpec((None, tm, d), lambda b, j: (b, 0, 0)),
            pl.BlockSpec((None, tm, d), lambda b, j: (b, jnp.maximum(j - 1, 0), 0))]


def _in_proj_kernel(*refs, n_src):
    src_refs = refs[:n_src]
    sh_ref, sc_ref, nw_ref, w_ref, cos_ref, sin_ref, gla_ref, za_ref, na_ref, sq_ref, sk_ref, sv_ref = refs[n_src:]
    x = _token_rows(src_refs)
    tm = x.shape[0]
    lane = lax.broadcasted_iota(jnp.int32, (tm // 2, LANES), 1)
    first_half = _imod(lane, HEAD_DIM) < (HEAD_DIM // 2)
    n_q = SWA_W // LANES
    for rows in (slice(0, tm // 2), slice(tm // 2, tm)):
        h = _rms(x[rows]) * nw_ref[...]
        h = (h * (1.0 + sc_ref[...]) + sh_ref[...]).astype(BF16)
        qk = _dot(h, w_ref[:, C_SQ:C_SV])
        cos = cos_ref[rows, :]
        sin = sin_ref[rows, :]
        for cb in range(n_q + 1):
            a = qk[:, cb * LANES:(cb + 1) * LANES]
            partner = jnp.where(first_half, pltpu.roll(a, LANES - HEAD_DIM // 2, 1), pltpu.roll(a, HEAD_DIM // 2, 1))
            r = (a * cos + partner * sin).astype(BF16)
            if cb < n_q:
                sq_ref[rows, cb * LANES:(cb + 1) * LANES] = r
            else:
                sk_ref[rows, :] = r
        gla_ref[rows, :] = _dot(h, w_ref[:, C_GLA:C_ZA])
        za_ref[rows, :] = _dot(h, w_ref[:, C_ZA:C_NA])
        na_ref[rows, :] = _dot(h, w_ref[:, C_NA:C_SQ]).astype(BF16)
        sv_ref[rows, :] = _dot(h, w_ref[:, C_SV:IN_PERM_W]).astype(BF16)


def _in_proj(srcs, layer, mods, norm_w, w_perm, cos_t, sin_t):
    batch, d = srcs[0].shape[0], srcs[0].shape[2]
    s = sum(a.shape[1] for a in srcs)
    tm = ROW_BLOCK
    row = lambda w: pl.BlockSpec((None, tm, w), lambda b, j: (b, j, 0))
    tab = pl.BlockSpec((tm, LANES), lambda b, j: (j, 0))
    return pl.pallas_call(
        functools.partial(_in_proj_kernel, n_src=len(srcs)),
        grid=(batch, s // tm),
        in_specs=_token_specs(srcs, tm) + [
            _mod_spec(layer, 0, batch, True), _mod_spec(layer, 1, batch, True), _layer_spec(layer, (1, d)),
            _layer_spec(layer, (d, IN_PERM_W)), tab, tab],
        out_specs=[row(4 * GLA_W), row(LANES), row(3 * NA_W), row(SWA_W), row(SWA_KV_W), row(SWA_KV_W)],
        out_shape=[
            jax.ShapeDtypeStruct((batch, s, 4 * GLA_W), F32),
            jax.ShapeDtypeStruct((batch, s, LANES), F32),
            jax.ShapeDtypeStruct((batch, s, 3 * NA_W), BF16),
            jax.ShapeDtypeStruct((batch, s, SWA_W), BF16),
            jax.ShapeDtypeStruct((batch, s, SWA_KV_W), BF16),
            jax.ShapeDtypeStruct((batch, s, SWA_KV_W), BF16),
        ],
        compiler_params=_params(2),
    )(*srcs, mods, mods, norm_w, w_perm, cos_t, sin_t)


def _gla_kernel(qf_ref, kf_ref, vf_ref, zf_ref, qb_ref, kb_ref, vb_ref, zb_ref,
                waf_ref, wab_ref, baf_ref, bab_ref, of_ref, ob_ref, stf_ref, stb_ref):
    @pl.when(pl.program_id(1) == 0)
    def _():
        stf_ref[...] = jnp.zeros_like(stf_ref)
        stb_ref[...] = jnp.zeros_like(stb_ref)

    c = GLA_CHUNK
    w = GLA_W
    n_chunks = ROW_BLOCK // c
    r64 = lax.broadcasted_iota(jnp.int32, (c, c), 0)
    c64 = lax.broadcasted_iota(jnp.int32, (c, c), 1)
    row_h = _idiv(lax.broadcasted_iota(jnp.int32, (GLA_HEADS * c, w), 0), c)
    col_h = _idiv(lax.broadcasted_iota(jnp.int32, (GLA_HEADS * c, w), 1), c)
    same_head = row_h == col_h
    ar = _imod(lax.broadcasted_iota(jnp.int32, (GLA_HEADS * c, c), 0), c)
    ac = lax.broadcasted_iota(jnp.int32, (GLA_HEADS * c, c), 1)
    out_head = _idiv(lax.broadcasted_iota(jnp.int32, (c, w), 1), c)
    dirs = [
        dict(q=qf_ref, k=kf_ref, v=vf_ref, z=zf_ref, wa=waf_ref, ba=baf_ref, o=of_ref, st=stf_ref, rev=False,
             order=range(n_chunks), cum=(r64 >= c64).astype(BF16), causal=ar >= ac),
        dict(q=qb_ref, k=kb_ref, v=vb_ref, z=zb_ref, wa=wab_ref, ba=bab_ref, o=ob_ref, st=stb_ref, rev=True,
             order=range(n_chunks - 1, -1, -1), cum=(r64 <= c64).astype(BF16), causal=ar <= ac),
    ]
    units = [(d, ci) for d in dirs for ci in d['order']]
    rows = lambda ci: slice(ci * c, (ci + 1) * c)

    log_a = []
    for d in dirs:
        z = _dot_split(d['z'][...], d['wa'][...]) + d['ba'][...]
        log_a.append((jnp.minimum(z, 0.0) - jnp.log1p(jnp.exp(-jnp.abs(z)))) * (1.0 / GLA_TAU))
    b, b_tot = [], []
    for (d, ci), la in zip(units, [la for la in log_a for _ in range(n_chunks)]):
        hi, mid, lo = _split3(la[rows(ci)])
        b.append(_dot(d['cum'], hi) + _dot(d['cum'], mid) + _dot(d['cum'], lo))
        b_tot.append(b[-1][0:1] if d['rev'] else b[-1][c - 1:c])
    k = [d['k'][rows(ci), :] for d, ci in units]
    v = [d['v'][rows(ci), :].astype(BF16) for d, ci in units]
    q_in = [(d['q'][rows(ci), :] * jnp.exp(bu)).astype(BF16) for (d, ci), bu in zip(units, b)]
    k_in = [(ku * jnp.exp(-bu)).astype(BF16) for ku, bu in zip(k, b)]
    k_end = [(ku * jnp.exp(bt - bu)).astype(BF16) for ku, bu, bt in zip(k, b, b_tot)]
    qx = [jnp.where(same_head, jnp.concatenate([qu] * GLA_HEADS, axis=0), jnp.zeros((), BF16)) for qu in q_in]
    a = [jnp.where(d['causal'], _dot_nt(qxu, ku), 0.0).astype(BF16) for (d, _), qxu, ku in zip(units, qx, k_in)]
    r = [_dot(au, vu) for au, vu in zip(a, v)]
    upd = [jnp.where(same_head, _dot_tn(vu, ku), 0.0) for vu, ku in zip(v, k_end)]
    st_enter = []
    for di, d in enumerate(dirs):
        st = d['st'][...]
        for u in range(di * n_chunks, (di + 1) * n_chunks):
            st_enter.append(st.astype(BF16))
            st = st * jnp.exp(b_tot[u]) + upd[u]
        d['st'][...] = st
    for (d, ci), qu, su, ru in zip(units, q_in, st_enter, r):
        o = _dot_nt(qu, su)
        for h in range(GLA_HEADS):
            o = o + jnp.where(out_head == h, ru[h * c:(h + 1) * c], 0.0)
        d['o'][rows(ci), :] = o


def _gla(gla_in, za, layer, wa_f, wa_b, ba_f, ba_b):
    batch, s, _ = gla_in.shape
    tm = ROW_BLOCK
    nblk = s // tm
    fwd = lambda j: j
    bwd = lambda j: jnp.where(j == 0, 0, nblk - j)

    def col(order, cb):
        return pl.BlockSpec((None, tm, GLA_W), lambda b, j: (b, order(j), cb))

    def zspec(order):
        return pl.BlockSpec((None, tm, LANES), lambda b, j: (b, order(j), 0))

    return pl.pallas_call(
        _gla_kernel,
        grid=(batch, nblk),
        in_specs=[col(fwd, 0), col(fwd, 1), col(fwd, 2), zspec(fwd),
                  col(bwd, 0), col(bwd, 1), col(bwd, 2), zspec(bwd),
                  _layer_spec(layer, (LANES, GLA_W)), _layer_spec(layer, (LANES, GLA_W)),
                  _layer_spec(layer, (1, GLA_W)), _layer_spec(layer, (1, GLA_W))],
        out_specs=[col(fwd, 0), col(bwd, 0)],
        out_shape=[jax.ShapeDtypeStruct((batch, s, GLA_W), F32)] * 2,
        scratch_shapes=[pltpu.VMEM((GLA_W, GLA_W), F32)] * 2,
        compiler_params=_params(1, 1),
    )(gla_in, gla_in, gla_in, za, gla_in, gla_in, gla_in, za, wa_f, wa_b, ba_f, ba_b)


def _stack_pair(q):
    left = lax.broadcasted_iota(jnp.int32, q.shape, 1) < HEAD_DIM
    zero = jnp.zeros((), q.dtype)
    return jnp.concatenate([jnp.where(left, q, zero), jnp.where(left, zero, q)], axis=0)


def _unstack_pair(o):
    n = o.shape[0] // 2
    left = lax.broadcasted_iota(jnp.int32, (n, o.shape[1]), 1) < HEAD_DIM
    return jnp.where(left, o[:n], o[n:])


def _softmax_weights(scores, extra_logit=None):
    m = functools.reduce(jnp.maximum, [jnp.max(s, axis=-1, keepdims=True) for s in scores])
    if extra_logit is not None:
        m = jnp.maximum(m, extra_logit)
    ps = [jnp.exp2(s - m) for s in scores]
    denom = functools.reduce(jnp.add, [jnp.sum(p, axis=-1, keepdims=True) for p in ps])
    if extra_logit is not None:
        denom = denom + jnp.exp2(extra_logit - m)
    return [p.astype(BF16) for p in ps], denom


def _softmax_pv(scores, values, extra_logit=None):
    ps, denom = _softmax_weights(scores, extra_logit)
    return functools.reduce(jnp.add, [_dot(p, v) for p, v in zip(ps, values)]) / denom


def _na_kernel(q_ref, k_ref, v_ref, bias_ref, o_ref, *, ctx_len, grid_rows, first_block):
    j = pl.program_id(1) + first_block
    n_pairs = NA_W // PAIR_W
    rows_per_block = ROW_BLOCK // GRID_W
    n_loc = NA_KH * GRID_W
    unit = 2 * GRID_W
    pair_cols = [slice(p * PAIR_W, (p + 1) * PAIR_W) for p in range(n_pairs)]
    units = [(rr, p) for p in range(n_pairs) for rr in range(rows_per_block)]
    qx = {(rr, p): _stack_pair(q_ref[rr * GRID_W:(rr + 1) * GRID_W, pair_cols[p]]) for rr, p in units}
    s_ctx = [_dot_nt(jnp.concatenate([qx[rr, p] for rr in range(rows_per_block)], axis=0),
                     k_ref[0:ctx_len, pair_cols[p]]) for p in range(n_pairs)]

    def ctx_out(w):
        return [_dot(jnp.concatenate([w[rr, p][0][-1] for rr in range(rows_per_block)], axis=0),
                     v_ref[0:ctx_len, pair_cols[p]]) for p in range(n_pairs)]

    def finish(o, w):
        for rr, p in units:
            o_ref[rr * GRID_W:(rr + 1) * GRID_W, pair_cols[p]] = _unstack_pair(o[rr, p] / w[rr, p][1]).astype(
                o_ref.dtype)

    def context_block():
        w = {(rr, p): _softmax_weights([s_ctx[p][rr * unit:(rr + 1) * unit]]) for rr, p in units}
        o_ctx = ctx_out(w)
        finish({(rr, p): o_ctx[p][rr * unit:(rr + 1) * unit] for rr, p in units}, w)

    def latent_block():
        k0, bias = [], {}
        for rr in range(rows_per_block):
            r = (j - 1) * rows_per_block + rr
            first_key_row = jnp.clip(r - NA_KH // 2, 0, grid_rows - NA_KH)
            k0.append(pl.multiple_of(ctx_len + first_key_row * GRID_W, GRID_W))
            d0 = first_key_row - r + NA_KH - 1
            for p in range(n_pairs):
                bias[rr, p] = jnp.concatenate(
                    [jnp.concatenate([bias_ref[2 * p + side, d0 + 2 * m] for m in range(NA_KH // 2)], axis=1)
                     for side in (0, 1)], axis=0)
        s_loc = {(rr, p): _dot_nt(qx[rr, p], k_ref[pl.ds(k0[rr], n_loc), pair_cols[p]]) + bias[rr, p]
                 for rr, p in units}
        w = {(rr, p): _softmax_weights([s_loc[rr, p], s_ctx[p][rr * unit:(rr + 1) * unit]]) for rr, p in units}
        o_ctx = ctx_out(w)
        o = {(rr, p): _dot(w[rr, p][0][0], v_ref[pl.ds(k0[rr], n_loc), pair_cols[p]])
             + o_ctx[p][rr * unit:(rr + 1) * unit] for rr, p in units}
        finish(o, w)

    if first_block == 0:
        pl.when(j == 0)(context_block)
        pl.when(j > 0)(latent_block)
    else:
        latent_block()


def _na(na_qkv, layer, bias, ctx_len, skip_ctx):
    batch, s, _ = na_qkv.shape
    tm = ROW_BLOCK
    first = 1 if skip_ctx else 0
    grid_rows = (s - ctx_len) // GRID_W
    full = lambda cb: pl.BlockSpec((None, s, NA_W), lambda b, j: (b, 0, cb))
    return pl.pallas_call(
        functools.partial(_na_kernel, ctx_len=ctx_len, grid_rows=grid_rows, first_block=first),
        grid=(batch, s // tm - first),
        in_specs=[pl.BlockSpec((None, tm, NA_W), lambda b, j: (b, j + first, 0)), full(1), full(2),
                  _layer_spec(layer, bias.shape[1:])],
        out_specs=pl.BlockSpec((None, tm, NA_W), lambda b, j: (b, j, 0)),
        out_shape=jax.ShapeDtypeStruct((batch, s - first * tm, NA_W), BF16),
        compiler_params=_params(2),
    )(na_qkv, na_qkv, na_qkv, bias)


def _swa_kernel(sinks_ref, q_ref, k_ref, v_ref, o_ref, *, layer, ctx_len, seq_len, first_block):
    j = pl.program_id(1) + first_block
    tm = ROW_BLOCK
    n_pairs = SWA_W // PAIR_W
    n_loc = tm + 2 * SWA_WINDOW
    pair_cols = [slice(p * PAIR_W, (p + 1) * PAIR_W) for p in range(n_pairs)]
    lane = lax.broadcasted_iota(jnp.int32, (tm, PAIR_W), 1)
    zero = jnp.zeros((), BF16)
    units = [(p + side * n_pairs, p, side) for p in range(n_pairs) for side in (0, 1)]
    qx = [jnp.where((lane < HEAD_DIM) == (side == 0), q_ref[:, pair_cols[p]], zero) for _, p, side in units]
    sink = [sinks_ref[layer, h] * LOG2E for h, _, _ in units]
    k_ctx = k_ref[0:ctx_len, :]
    v_ctx = v_ref[0:ctx_len, :]

    def finish(o):
        for p, cols in enumerate(pair_cols):
            o_ref[:, cols] = jnp.where(lane < HEAD_DIM, o[2 * p], o[2 * p + 1]).astype(o_ref.dtype)

    def context_block():
        s_ctx = [_dot_nt(q, k_ctx) for q in qx]
        w = [_softmax_weights([s], sk) for s, sk in zip(s_ctx, sink)]
        finish([_dot(ps[0], v_ctx) / denom for ps, denom in w])

    def latent_block():
        q0 = (j - 1) * tm
        k0 = jnp.clip(q0 - SWA_WINDOW, 0, seq_len - n_loc)
        qpos = q0 + lax.broadcasted_iota(jnp.int32, (tm, n_loc), 0)
        kpos = k0 + lax.broadcasted_iota(jnp.int32, (tm, n_loc), 1)
        in_window = jnp.abs(qpos - kpos) <= SWA_WINDOW
        start = pl.multiple_of(ctx_len + k0, SWA_WINDOW)
        k_loc = k_ref[pl.ds(start, n_loc), :]
        v_loc = v_ref[pl.ds(start, n_loc), :]
        s, w, o = {}, {}, {}
        for t in range(len(units) + 2):
            if t < len(units):
                s[t] = [jnp.where(in_window, _dot_nt(qx[t], k_loc), -jnp.inf), _dot_nt(qx[t], k_ctx)]
            if 0 <= t - 1 < len(units):
                w[t - 1] = _softmax_weights(s.pop(t - 1), sink[t - 1])
            if 0 <= t - 2 < len(units):
                ps, denom = w.pop(t - 2)
                o[t - 2] = (_dot(ps[0], v_loc) + _dot(ps[1], v_ctx)) / denom
        finish([o[u] for u in range(len(units))])

    if first_block == 0:
        pl.when(j == 0)(context_block)
        pl.when(j > 0)(latent_block)
    else:
        latent_block()


def _swa(sink, layer, sq, sk, sv, ctx_len, skip_ctx):
    batch, s, _ = sq.shape
    tm = ROW_BLOCK
    first = 1 if skip_ctx else 0
    full = pl.BlockSpec((None, s, SWA_KV_W), lambda b, j: (b, 0, 0))
    return pl.pallas_call(
        functools.partial(_swa_kernel, layer=layer, ctx_len=ctx_len, seq_len=s - ctx_len, first_block=first),
        grid=(batch, s // tm - first),
        in_specs=[pl.BlockSpec(memory_space=pltpu.SMEM),
                  pl.BlockSpec((None, tm, SWA_W), lambda b, j: (b, j + first, 0)), full, full],
        out_specs=pl.BlockSpec((None, tm, SWA_W), lambda b, j: (b, j, 0)),
        out_shape=jax.ShapeDtypeStruct((batch, s - first * tm, SWA_W), BF16),
        compiler_params=_params(2),
    )(sink, sq, sk, sv)


def _out_ffn_kernel(*refs, n_src, final):
    src_refs = refs[:n_src]
    (of_ref, ob_ref, gate_ref, na_ref, swa_ref, g1_ref, sh2_ref, sc2_ref, g2_ref, nffn_ref, gn_ref, avg_ref,
     wg_ref, wn_ref, ws_ref, wgate_ref, wup_ref, wdown_ref, fn_ref, o_ref) = refs[n_src:]
    o = of_ref[...] + ob_ref[...]
    hi, lo = _split2(o * o)
    avg = avg_ref[...]
    ms = _dot(hi, avg) + _dot(lo, avg)
    y = o * lax.rsqrt(ms + NORM_EPS) * gn_ref[...] * _silu(gate_ref[...])
    mix = _dot(y.astype(BF16), wg_ref[...]) + _dot(na_ref[...], wn_ref[...]) + _dot(swa_ref[...], ws_ref[...])
    x1 = _token_rows(src_refs) + g1_ref[...] * mix
    h = _rms(x1) * nffn_ref[...]
    h = (h * (1.0 + sc2_ref[...]) + sh2_ref[...]).astype(BF16)

    d_ff = wgate_ref.shape[1]
    chunks = [slice(c0, min(c0 + FF_CHUNK, d_ff)) for c0 in range(0, d_ff, FF_CHUNK)]
    acc = jnp.zeros(x1.shape, F32)
    gu = {}
    for t in range(len(chunks) + 1):
        if t < len(chunks):
            gu[t] = (_dot(h, wgate_ref[:, chunks[t]]), _dot(h, wup_ref[:, chunks[t]]))
        if t >= 1:
            g, u = gu.pop(t - 1)
            acc = acc + _dot((_silu(g) * u).astype(BF16), wdown_ref[chunks[t - 1], :])
    x2 = x1 + g2_ref[...] * acc
    if final:
        x2 = _rms(x2) * fn_ref[...]
    o_ref[...] = x2


def _out_ffn(srcs, o_f, o_b, gla_in, y_na, y_swa, layer, mods, norm_ffn, gla_norm, avg, w_g, w_n, w_s,
             w_gate, w_up, w_down, final_norm, skip_ctx, final):
    batch, s, _ = o_f.shape
    d = srcs[0].shape[2]
    d_ff = w_gate.shape[2]
    tm = ROW_BLOCK
    off = 1 if skip_ctx else 0
    nblk = s // tm - off
    tok_off = off
    if skip_ctx and len(srcs) == 2:
        srcs, tok_off = srcs[1:], 0
    row = lambda w, cb=0: pl.BlockSpec((None, tm, w), lambda b, j: (b, j + off, cb))
    out = lambda w: pl.BlockSpec((None, tm, w), lambda b, j: (b, j, 0))
    ms = lambda chunk: _mod_spec(layer, chunk, batch, not skip_ctx)
    return pl.pallas_call(
        functools.partial(_out_ffn_kernel, n_src=len(srcs), final=final),
        grid=(batch, nblk),
        in_specs=_token_specs(srcs, tm, tok_off) + [
            row(GLA_W), row(GLA_W), row(GLA_W, 3), out(NA_W), out(SWA_W),
            ms(2), ms(3), ms(4), ms(5), _layer_spec(layer, (1, d)), _layer_spec(layer, (1, GLA_W)),
            pl.BlockSpec((GLA_W, GLA_W), lambda b, j: (0, 0)),
            _layer_spec(layer, (GLA_W, d)), _layer_spec(layer, (NA_W, d)), _layer_spec(layer, (SWA_W, d)),
            _layer_spec(layer, (d, d_ff)), _layer_spec(layer, (d, d_ff)), _layer_spec(layer, (d_ff, d)),
            pl.BlockSpec((1, d), lambda b, j: (0, 0))],
        out_specs=out(d),
        out_shape=jax.ShapeDtypeStruct((batch, nblk * tm, d), F32),
        compiler_params=_params(2),
    )(*srcs, o_f, o_b, gla_in, y_na, y_swa, mods, mods, mods, mods, norm_ffn, gla_norm, avg, w_g, w_n, w_s,
      w_gate, w_up, w_down, final_norm)


def _rope_tables(ctx_len, seq_len):
    t = np.arange(seq_len)
    n_freq = HEAD_DIM // 4
    inv_freq = ROPE_THETA ** (-np.arange(n_freq) / n_freq)
    ang = np.concatenate([(t // GRID_W)[:, None] * inv_freq, (t % GRID_W)[:, None] * inv_freq], axis=-1)
    cos_h = np.concatenate([np.cos(ang), np.cos(ang)], axis=-1)
    sin_h = np.concatenate([-np.sin(ang), np.sin(ang)], axis=-1)
    cos_t = np.concatenate([np.ones((ctx_len, HEAD_DIM)), cos_h], axis=0)
    sin_t = np.concatenate([np.zeros((ctx_len, HEAD_DIM)), sin_h], axis=0)
    return jnp.asarray(np.tile(cos_t, (1, 2)), F32), jnp.asarray(np.tile(sin_t, (1, 2)), F32)


def _swa_pair_order(a, axis):
    n_pairs = SWA_W // PAIR_W
    heads = [h for p in range(n_pairs) for h in (p, p + n_pairs)]
    return jnp.concatenate([lax.slice_in_dim(a, h * HEAD_DIM, (h + 1) * HEAD_DIM, axis=axis) for h in heads],
                           axis=axis)


def _permute_w_in(w):
    scale = HEAD_DIM ** -0.5
    scale2 = scale * LOG2E
    o = np.cumsum([0, GLA_W, GLA_W, GLA_W, GLA_W, GLA_RANK, GLA_RANK, NA_W, NA_W, NA_W, SWA_W, SWA_KV_W, SWA_KV_W])
    parts = [w[..., o[0]:o[1]] * scale, w[..., o[1]:o[4]],
             w[..., o[4]:o[6]], jnp.zeros(w.shape[:2] + (LANES - 2 * GLA_RANK,), w.dtype),
             w[..., o[6]:o[7]] * scale2, w[..., o[7]:o[9]],
             _swa_pair_order(w[..., o[9]:o[10]], 2) * scale2, w[..., o[10]:o[12]]]
    return jnp.concatenate(parts, axis=-1).astype(BF16)


def _na_bias_table(rpb):
    qc = np.arange(GRID_W)
    kc = np.arange(GRID_W)
    win = np.clip(qc - NA_KW // 2, 0, GRID_W - NA_KW)
    valid = (kc[None, :] >= win[:, None]) & (kc[None, :] < win[:, None] + NA_KW)
    dc = kc[None, :] - qc[:, None] + NA_KW - 1
    pick_col = ((dc[None] == np.arange(2 * NA_KW - 1)[:, None, None]) & valid[None]).astype(np.float32)
    t = jnp.einsum('lhde,eck->lhdck', rpb.astype(F32) * LOG2E, pick_col, precision=lax.Precision.HIGHEST)
    t = jnp.where(valid, t, -jnp.inf)
    return jnp.concatenate([t[:, :, :-1], t[:, :, 1:]], axis=-1)


def _pad_rank(wa2, offset):
    return jnp.pad(wa2, ((0, 0), (offset, LANES - offset - GLA_RANK), (0, 0)))


def kernel(x, c, ctx, c_ctx, w_mod, b_mod, norm_mix, norm_ffn, w_in, gla_wa2_f, gla_ba_f, gla_wa2_b, gla_ba_b,
           gla_norm, na_rpb, swa_sink, w_out, w_gate, w_up, w_down, final_norm):
    batch, seq_len, d = x.shape
    ctx_len = ctx.shape[1]
    depth = w_mod.shape[0]
    assert d == D_MODEL and ctx_len == ROW_BLOCK and seq_len % ROW_BLOCK == 0 and batch < 8
    assert seq_len % GRID_W == 0 and seq_len // GRID_W >= 2 * NA_KH

    cc = jnp.concatenate([c, c_ctx[None, :], jnp.zeros((8 - batch - 1, d), F32)], axis=0)
    mods = _modulation(cc, w_mod, b_mod).reshape(depth, 8, 1, 6 * d)
    cos_t, sin_t = _rope_tables(ctx_len, seq_len)
    head_avg = jnp.asarray(np.kron(np.eye(GLA_HEADS), np.full((HEAD_DIM, HEAD_DIM), 1.0 / HEAD_DIM)), BF16)
    w_in_p = _permute_w_in(w_in)
    wa_f, wa_b = _pad_rank(gla_wa2_f, 0), _pad_rank(gla_wa2_b, GLA_RANK)
    ba_f, ba_b = gla_ba_f.reshape(depth, 1, GLA_W), gla_ba_b.reshape(depth, 1, GLA_W)
    na_bias = _na_bias_table(na_rpb)
    w_o_gla = w_out[:, :GLA_W].astype(BF16)
    w_o_na = w_out[:, GLA_W:GLA_W + NA_W].astype(BF16)
    w_o_swa = _swa_pair_order(w_out[:, GLA_W + NA_W:], 1).astype(BF16)
    w_gate_b, w_up_b, w_down_b = w_gate.astype(BF16), w_up.astype(BF16), w_down.astype(BF16)
    norm_mix3, norm_ffn3 = norm_mix.reshape(depth, 1, d), norm_ffn.reshape(depth, 1, d)
    gla_norm3 = gla_norm.reshape(depth, 1, GLA_W)

    srcs = (ctx, x)
    for i in range(depth):
        last = i == depth - 1
        gla_in, za, na_qkv, sq, sk, sv = _in_proj(srcs, i, mods, norm_mix3, w_in_p, cos_t, sin_t)
        o_f, o_b = _gla(gla_in, za, i, wa_f, wa_b, ba_f, ba_b)
        y_na = _na(na_qkv, i, na_bias, ctx_len, skip_ctx=last)
        y_swa = _swa(swa_sink, i, sq, sk, sv, ctx_len, skip_ctx=last)
        srcs = (_out_ffn(srcs, o_f, o_b, gla_in, y_na, y_swa, i, mods, norm_ffn3, gla_norm3, head_avg,
                         w_o_gla, w_o_na, w_o_swa, w_gate_b, w_up_b, w_down_b, final_norm.reshape(1, d),
                         skip_ctx=last, final=last),)
    return srcs[0]
```

```python
import functools

import jax
import jax.numpy as jnp
import numpy as np
from jax import lax
from jax.experimental import pallas as pl
from jax.experimental.pallas import tpu as pltpu

F32 = jnp.float32
BF16 = jnp.bfloat16

D_MODEL = 1024
HEAD_DIM = 64
GRID_W = 64
GLA_HEADS = 4
NA_HEADS = 6
SWA_HEADS = 6
SWA_KV_HEADS = 2
GLA_W = GLA_HEADS * HEAD_DIM
NA_W = NA_HEADS * HEAD_DIM
SWA_W = SWA_HEADS * HEAD_DIM
SWA_KV_W = SWA_KV_HEADS * HEAD_DIM
GLA_RANK = 16
GLA_TAU = 16.0
GLA_CHUNK = 64
NA_KH = 8
NA_KW = 16
SWA_WINDOW = 128
ROPE_THETA = 10000.0
NORM_EPS = 1e-6

LANES = 128
ROW_BLOCK = 256
PAIR_W = 2 * HEAD_DIM
LOG2E = float(np.log2(np.e))
FF_CHUNK = 256
VMEM_LIMIT = 56 * 1024 * 1024

C_GLA = 0
C_ZA = C_GLA + 4 * GLA_W
C_NA = C_ZA + LANES
C_SQ = C_NA + 3 * NA_W
C_SK = C_SQ + SWA_W
C_SV = C_SK + SWA_KV_W
IN_PERM_W = C_SV + SWA_KV_W


def _dot(a, b):
    return jnp.dot(a, b, preferred_element_type=F32)


def _dot_nt(a, b):
    return lax.dot_general(a, b, (((1,), (1,)), ((), ())), preferred_element_type=F32)


def _dot_tn(a, b):
    return lax.dot_general(a, b, (((0,), (0,)), ((), ())), preferred_element_type=F32)


def _idiv(a, n):
    assert n & (n - 1) == 0
    return a >> (n.bit_length() - 1)


def _imod(a, n):
    assert n & (n - 1) == 0
    return a & (n - 1)


def _split2(a):
    hi = a.astype(BF16)
    lo = (a - hi.astype(F32)).astype(BF16)
    return hi, lo


def _split3(a):
    hi = a.astype(BF16)
    r = a - hi.astype(F32)
    mid = r.astype(BF16)
    lo = (r - mid.astype(F32)).astype(BF16)
    return hi, mid, lo


def _dot_split(a, b):
    ah, al = _split2(a)
    bh, bl = _split2(b)
    return _dot(ah, bh) + _dot(al, bh) + _dot(ah, bl)


def _silu(a):
    return a * jax.nn.sigmoid(a)


def _rms(x):
    return x * lax.rsqrt(jnp.mean(x * x, axis=-1, keepdims=True) + NORM_EPS)


def _params(n_parallel, n_arbitrary=0):
    return pltpu.CompilerParams(
        dimension_semantics=("parallel",) * n_parallel + ("arbitrary",) * n_arbitrary,
        vmem_limit_bytes=VMEM_LIMIT)


def _mod_kernel(c_ref, w_ref, b_ref, o_ref):
    o_ref[...] = _dot_split(_silu(c_ref[...]), w_ref[...]) + b_ref[...]


def _modulation(cc, w_mod, b_mod):
    depth, d, n = w_mod.shape
    tn = 1536
    return pl.pallas_call(
        _mod_kernel,
        grid=(depth, n // tn),
        in_specs=[
            pl.BlockSpec((8, d), lambda i, k: (0, 0)),
            pl.BlockSpec((None, d, tn), lambda i, k: (i, 0, k)),
            pl.BlockSpec((None, 1, tn), lambda i, k: (i, 0, k)),
        ],
        out_specs=pl.BlockSpec((None, 8, tn), lambda i, k: (i, 0, k)),
        out_shape=jax.ShapeDtypeStruct((depth, 8, n), F32),
        compiler_params=_params(2),
    )(cc, w_mod, b_mod.reshape(depth, 1, n))


def _grid_pos(b, j):
    return b, j


def _mod_spec(layer, chunk, batch, ctx_first, pos=_grid_pos):
    def index(*g):
        b, j = pos(*g)
        return layer, (jnp.where(j == 0, batch, b) if ctx_first else b), 0, chunk
    return pl.BlockSpec((None, None, 1, D_MODEL), index)


def _layer_spec(layer, shape):
    return pl.BlockSpec((None,) + shape, lambda *g: (layer,) + (0,) * len(shape))


def _token_rows(src_refs, is_ctx):
    if len(src_refs) == 1:
        return src_refs[0][...]
    ctx_ref, x_ref = src_refs
    return jnp.where(is_ctx, ctx_ref[...], x_ref[...])


def _token_specs(srcs, tm, off=0, pos=_grid_pos):
    d = srcs[0].shape[-1]
    if len(srcs) == 1:
        return [pl.BlockSpec((None, tm, d), lambda *g: (pos(*g)[0], pos(*g)[1] + off, 0))]
    assert off == 0
    return [pl.BlockSpec((None, tm, d), lambda *g: (pos(*g)[0], 0, 0)),
            pl.BlockSpec((None, tm, d), lambda *g: (pos(*g)[0], jnp.maximum(pos(*g)[1] - 1, 0), 0))]


def _in_proj_kernel(*refs, n_src):
    src_refs = refs[:n_src]
    sh_ref, sc_ref, nw_ref, w_ref, cos_ref, sin_ref, gla_ref, za_ref, na_ref, sq_ref, sk_ref, sv_ref = refs[n_src:]
    x = _token_rows(src_refs, pl.program_id(1) == 0)
    tm = x.shape[0]
    lane = lax.broadcasted_iota(jnp.int32, (tm // 2, LANES), 1)
    first_half = _imod(lane, HEAD_DIM) < (HEAD_DIM // 2)
    n_q = SWA_W // LANES
    for rows in (slice(0, tm // 2), slice(tm // 2, tm)):
        h = _rms(x[rows]) * nw_ref[...]
        h = (h * (1.0 + sc_ref[...]) + sh_ref[...]).astype(BF16)
        qk = _dot(h, w_ref[:, C_SQ:C_SV])
        cos = cos_ref[rows, :]
        sin = sin_ref[rows, :]
        for cb in range(n_q + 1):
            a = qk[:, cb * LANES:(cb + 1) * LANES]
            partner = jnp.where(first_half, pltpu.roll(a, LANES - HEAD_DIM // 2, 1), pltpu.roll(a, HEAD_DIM // 2, 1))
            r = (a * cos + partner * sin).astype(BF16)
            if cb < n_q:
                sq_ref[rows, cb * LANES:(cb + 1) * LANES] = r
            else:
                sk_ref[rows, :] = r
        gla_ref[rows, :] = _dot(h, w_ref[:, C_GLA:C_ZA])
        za_ref[rows, :] = _dot(h, w_ref[:, C_ZA:C_NA])
        na_ref[rows, :] = _dot(h, w_ref[:, C_NA:C_SQ]).astype(BF16)
        sv_ref[rows, :] = _dot(h, w_ref[:, C_SV:IN_PERM_W]).astype(BF16)


def _in_proj(srcs, layer, mods, norm_w, w_perm, cos_t, sin_t):
    batch, d = srcs[0].shape[0], srcs[0].shape[2]
    s = sum(a.shape[1] for a in srcs)
    tm = ROW_BLOCK
    row = lambda w: pl.BlockSpec((None, tm, w), lambda b, j: (b, j, 0))
    tab = pl.BlockSpec((tm, LANES), lambda b, j: (j, 0))
    return pl.pallas_call(
        functools.partial(_in_proj_kernel, n_src=len(srcs)),
        grid=(batch, s // tm),
        in_specs=_token_specs(srcs, tm) + [
            _mod_spec(layer, 0, batch, True), _mod_spec(layer, 1, batch, True), _layer_spec(layer, (1, d)),
            _layer_spec(layer, (d, IN_PERM_W)), tab, tab],
        out_specs=[row(4 * GLA_W), row(LANES), row(3 * NA_W), row(SWA_W), row(SWA_KV_W), row(SWA_KV_W)],
        out_shape=[
            jax.ShapeDtypeStruct((batch, s, 4 * GLA_W), F32),
            jax.ShapeDtypeStruct((batch, s, LANES), F32),
            jax.ShapeDtypeStruct((batch, s, 3 * NA_W), BF16),
            jax.ShapeDtypeStruct((batch, s, SWA_W), BF16),
            jax.ShapeDtypeStruct((batch, s, SWA_KV_W), BF16),
            jax.ShapeDtypeStruct((batch, s, SWA_KV_W), BF16),
        ],
        compiler_params=_params(2),
    )(*srcs, mods, mods, norm_w, w_perm, cos_t, sin_t)


def _gla_kernel(qf_ref, kf_ref, vf_ref, zf_ref, qb_ref, kb_ref, vb_ref, zb_ref,
                waf_ref, wab_ref, baf_ref, bab_ref, of_ref, ob_ref, stf_ref, stb_ref):
    @pl.when(pl.program_id(1) == 0)
    def _():
        stf_ref[...] = jnp.zeros_like(stf_ref)
        stb_ref[...] = jnp.zeros_like(stb_ref)

    c = GLA_CHUNK
    w = GLA_W
    n_chunks = ROW_BLOCK // c
    r64 = lax.broadcasted_iota(jnp.int32, (c, c), 0)
    c64 = lax.broadcasted_iota(jnp.int32, (c, c), 1)
    row_h = _idiv(lax.broadcasted_iota(jnp.int32, (GLA_HEADS * c, w), 0), c)
    col_h = _idiv(lax.broadcasted_iota(jnp.int32, (GLA_HEADS * c, w), 1), c)
    same_head = row_h == col_h
    ar = _imod(lax.broadcasted_iota(jnp.int32, (GLA_HEADS * c, c), 0), c)
    ac = lax.broadcasted_iota(jnp.int32, (GLA_HEADS * c, c), 1)
    out_head = _idiv(lax.broadcasted_iota(jnp.int32, (c, w), 1), c)
    dirs = [
        dict(q=qf_ref, k=kf_ref, v=vf_ref, z=zf_ref, wa=waf_ref, ba=baf_ref, o=of_ref, st=stf_ref, rev=False,
             order=range(n_chunks), cum=(r64 >= c64).astype(BF16), causal=ar >= ac),
        dict(q=qb_ref, k=kb_ref, v=vb_ref, z=zb_ref, wa=wab_ref, ba=bab_ref, o=ob_ref, st=stb_ref, rev=True,
             order=range(n_chunks - 1, -1, -1), cum=(r64 <= c64).astype(BF16), causal=ar <= ac),
    ]
    units = [(d, ci) for d in dirs for ci in d['order']]
    rows = lambda ci: slice(ci * c, (ci + 1) * c)

    log_a = []
    for d in dirs:
        z = _dot_split(d['z'][...], d['wa'][...]) + d['ba'][...]
        log_a.append((jnp.minimum(z, 0.0) - jnp.log1p(jnp.exp(-jnp.abs(z)))) * (1.0 / GLA_TAU))
    b, b_tot = [], []
    for (d, ci), la in zip(units, [la for la in log_a for _ in range(n_chunks)]):
        hi, mid, lo = _split3(la[rows(ci)])
        b.append(_dot(d['cum'], hi) + _dot(d['cum'], mid) + _dot(d['cum'], lo))
        b_tot.append(b[-1][0:1] if d['rev'] else b[-1][c - 1:c])
    k = [d['k'][rows(ci), :] for d, ci in units]
    v = [d['v'][rows(ci), :].astype(BF16) for d, ci in units]
    q_in = [(d['q'][rows(ci), :] * jnp.exp(bu)).astype(BF16) for (d, ci), bu in zip(units, b)]
    k_in = [(ku * jnp.exp(-bu)).astype(BF16) for ku, bu in zip(k, b)]
    k_end = [(ku * jnp.exp(bt - bu)).astype(BF16) for ku, bu, bt in zip(k, b, b_tot)]
    qx = [jnp.where(same_head, jnp.concatenate([qu] * GLA_HEADS, axis=0), jnp.zeros((), BF16)) for qu in q_in]
    a = [jnp.where(d['causal'], _dot_nt(qxu, ku), 0.0).astype(BF16) for (d, _), qxu, ku in zip(units, qx, k_in)]
    r = [_dot(au, vu) for au, vu in zip(a, v)]
    upd = [jnp.where(same_head, _dot_tn(vu, ku), 0.0) for vu, ku in zip(v, k_end)]
    st_enter = []
    for di, d in enumerate(dirs):
        st = d['st'][...]
        for u in range(di * n_chunks, (di + 1) * n_chunks):
            st_enter.append(st.astype(BF16))
            st = st * jnp.exp(b_tot[u]) + upd[u]
        d['st'][...] = st
    for (d, ci), qu, su, ru in zip(units, q_in, st_enter, r):
        o = _dot_nt(qu, su)
        for h in range(GLA_HEADS):
            o = o + jnp.where(out_head == h, ru[h * c:(h + 1) * c], 0.0)
        d['o'][rows(ci), :] = o


def _gla(gla_in, za, layer, wa_f, wa_b, ba_f, ba_b):
    batch, s, _ = gla_in.shape
    tm = ROW_BLOCK
    nblk = s // tm
    fwd = lambda j: j
    bwd = lambda j: jnp.where(j == 0, 0, nblk - j)

    def col(order, cb):
        return pl.BlockSpec((None, tm, GLA_W), lambda b, j: (b, order(j), cb))

    def zspec(order):
        return pl.BlockSpec((None, tm, LANES), lambda b, j: (b, order(j), 0))

    return pl.pallas_call(
        _gla_kernel,
        grid=(batch, nblk),
        in_specs=[col(fwd, 0), col(fwd, 1), col(fwd, 2), zspec(fwd),
                  col(bwd, 0), col(bwd, 1), col(bwd, 2), zspec(bwd),
                  _layer_spec(layer, (LANES, GLA_W)), _layer_spec(layer, (LANES, GLA_W)),
                  _layer_spec(layer, (1, GLA_W)), _layer_spec(layer, (1, GLA_W))],
        out_specs=[col(fwd, 0), col(bwd, 0)],
        out_shape=[jax.ShapeDtypeStruct((batch, s, GLA_W), F32)] * 2,
        scratch_shapes=[pltpu.VMEM((GLA_W, GLA_W), F32)] * 2,
        compiler_params=_params(1, 1),
    )(gla_in, gla_in, gla_in, za, gla_in, gla_in, gla_in, za, wa_f, wa_b, ba_f, ba_b)


def _stack_pair(q):
    left = lax.broadcasted_iota(jnp.int32, q.shape, 1) < HEAD_DIM
    zero = jnp.zeros((), q.dtype)
    return jnp.concatenate([jnp.where(left, q, zero), jnp.where(left, zero, q)], axis=0)


def _unstack_pair(o):
    n = o.shape[0] // 2
    left = lax.broadcasted_iota(jnp.int32, (n, o.shape[1]), 1) < HEAD_DIM
    return jnp.where(left, o[:n], o[n:])


def _softmax_weights(scores, extra_logit=None):
    m = functools.reduce(jnp.maximum, [jnp.max(s, axis=-1, keepdims=True) for s in scores])
    if extra_logit is not None:
        m = jnp.maximum(m, extra_logit)
    ps = [jnp.exp2(s - m) for s in scores]
    denom = functools.reduce(jnp.add, [jnp.sum(p, axis=-1, keepdims=True) for p in ps])
    if extra_logit is not None:
        denom = denom + jnp.exp2(extra_logit - m)
    return [p.astype(BF16) for p in ps], denom


def _softmax_pv(scores, values, extra_logit=None):
    ps, denom = _softmax_weights(scores, extra_logit)
    return functools.reduce(jnp.add, [_dot(p, v) for p, v in zip(ps, values)]) / denom


def _na_kernel(q_ref, k_ref, v_ref, bias_ref, o_ref, *, ctx_len, grid_rows, first_block):
    j = pl.program_id(1) + first_block
    n_pairs = NA_W // PAIR_W
    rows_per_block = ROW_BLOCK // GRID_W
    n_loc = NA_KH * GRID_W
    unit = 2 * GRID_W
    pair_cols = [slice(p * PAIR_W, (p + 1) * PAIR_W) for p in range(n_pairs)]
    units = [(rr, p) for p in range(n_pairs) for rr in range(rows_per_block)]
    qx = {(rr, p): _stack_pair(q_ref[rr * GRID_W:(rr + 1) * GRID_W, pair_cols[p]]) for rr, p in units}
    s_ctx = [_dot_nt(jnp.concatenate([qx[rr, p] for rr in range(rows_per_block)], axis=0),
                     k_ref[0:ctx_len, pair_cols[p]]) for p in range(n_pairs)]

    def ctx_out(w):
        return [_dot(jnp.concatenate([w[rr, p][0][-1] for rr in range(rows_per_block)], axis=0),
                     v_ref[0:ctx_len, pair_cols[p]]) for p in range(n_pairs)]

    def finish(o, w):
        for rr, p in units:
            o_ref[rr * GRID_W:(rr + 1) * GRID_W, pair_cols[p]] = _unstack_pair(o[rr, p] / w[rr, p][1]).astype(
                o_ref.dtype)

    def context_block():
        w = {(rr, p): _softmax_weights([s_ctx[p][rr * unit:(rr + 1) * unit]]) for rr, p in units}
        o_ctx = ctx_out(w)
        finish({(rr, p): o_ctx[p][rr * unit:(rr + 1) * unit] for rr, p in units}, w)

    def latent_block():
        k0, bias = [], {}
        for rr in range(rows_per_block):
            r = (j - 1) * rows_per_block + rr
            first_key_row = jnp.clip(r - NA_KH // 2, 0, grid_rows - NA_KH)
            k0.append(pl.multiple_of(ctx_len + first_key_row * GRID_W, GRID_W))
            d0 = first_key_row - r + NA_KH - 1
            for p in range(n_pairs):
                bias[rr, p] = jnp.concatenate(
                    [jnp.concatenate([bias_ref[2 * p + side, d0 + 2 * m] for m in range(NA_KH // 2)], axis=1)
                     for side in (0, 1)], axis=0)
        s_loc = {(rr, p): _dot_nt(qx[rr, p], k_ref[pl.ds(k0[rr], n_loc), pair_cols[p]]) + bias[rr, p]
                 for rr, p in units}
        w = {(rr, p): _softmax_weights([s_loc[rr, p], s_ctx[p][rr * unit:(rr + 1) * unit]]) for rr, p in units}
        o_ctx = ctx_out(w)
        o = {(rr, p): _dot(w[rr, p][0][0], v_ref[pl.ds(k0[rr], n_loc), pair_cols[p]])
             + o_ctx[p][rr * unit:(rr + 1) * unit] for rr, p in units}
        finish(o, w)

    if first_block == 0:
        pl.when(j == 0)(context_block)
        pl.when(j > 0)(latent_block)
    else:
        latent_block()


def _na(na_qkv, layer, bias, ctx_len, skip_ctx):
    batch, s, _ = na_qkv.shape
    tm = ROW_BLOCK
    first = 1 if skip_ctx else 0
    grid_rows = (s - ctx_len) // GRID_W
    full = lambda cb: pl.BlockSpec((None, s, NA_W), lambda b, j: (b, 0, cb))
    return pl.pallas_call(
        functools.partial(_na_kernel, ctx_len=ctx_len, grid_rows=grid_rows, first_block=first),
        grid=(batch, s // tm - first),
        in_specs=[pl.BlockSpec((None, tm, NA_W), lambda b, j: (b, j + first, 0)), full(1), full(2),
                  _layer_spec(layer, bias.shape[1:])],
        out_specs=pl.BlockSpec((None, tm, NA_W), lambda b, j: (b, j, 0)),
        out_shape=jax.ShapeDtypeStruct((batch, s - first * tm, NA_W), BF16),
        compiler_params=_params(2),
    )(na_qkv, na_qkv, na_qkv, bias)


def _swa_kernel(sinks_ref, q_ref, k_ref, v_ref, o_ref, *, layer, ctx_len, seq_len, first_block):
    j = pl.program_id(1) + first_block
    tm = ROW_BLOCK
    n_pairs = SWA_W // PAIR_W
    n_loc = tm + 2 * SWA_WINDOW
    pair_cols = [slice(p * PAIR_W, (p + 1) * PAIR_W) for p in range(n_pairs)]
    lane = lax.broadcasted_iota(jnp.int32, (tm, PAIR_W), 1)
    zero = jnp.zeros((), BF16)
    units = [(p + side * n_pairs, p, side) for p in range(n_pairs) for side in (0, 1)]
    qx = [jnp.where((lane < HEAD_DIM) == (side == 0), q_ref[:, pair_cols[p]], zero) for _, p, side in units]
    sink = [sinks_ref[layer, h] * LOG2E for h, _, _ in units]
    k_ctx = k_ref[0:ctx_len, :]
    v_ctx = v_ref[0:ctx_len, :]

    def finish(o):
        for p, cols in enumerate(pair_cols):
            o_ref[:, cols] = jnp.where(lane < HEAD_DIM, o[2 * p], o[2 * p + 1]).astype(o_ref.dtype)

    def context_block():
        s_ctx = [_dot_nt(q, k_ctx) for q in qx]
        w = [_softmax_weights([s], sk) for s, sk in zip(s_ctx, sink)]
        finish([_dot(ps[0], v_ctx) / denom for ps, denom in w])

    def latent_block():
        q0 = (j - 1) * tm
        k0 = jnp.clip(q0 - SWA_WINDOW, 0, seq_len - n_loc)
        qpos = q0 + lax.broadcasted_iota(jnp.int32, (tm, n_loc), 0)
        kpos = k0 + lax.broadcasted_iota(jnp.int32, (tm, n_loc), 1)
        in_window = jnp.abs(qpos - kpos) <= SWA_WINDOW
        start = pl.multiple_of(ctx_len + k0, SWA_WINDOW)
        k_loc = k_ref[pl.ds(start, n_loc), :]
        v_loc = v_ref[pl.ds(start, n_loc), :]
        s, w, o = {}, {}, {}
        for t in range(len(units) + 2):
            if t < len(units):
                s[t] = [jnp.where(in_window, _dot_nt(qx[t], k_loc), -jnp.inf), _dot_nt(qx[t], k_ctx)]
            if 0 <= t - 1 < len(units):
                w[t - 1] = _softmax_weights(s.pop(t - 1), sink[t - 1])
            if 0 <= t - 2 < len(units):
                ps, denom = w.pop(t - 2)
                o[t - 2] = (_dot(ps[0], v_loc) + _dot(ps[1], v_ctx)) / denom
        finish([o[u] for u in range(len(units))])

    if first_block == 0:
        pl.when(j == 0)(context_block)
        pl.when(j > 0)(latent_block)
    else:
        latent_block()


def _swa(sink, layer, sq, sk, sv, ctx_len, skip_ctx):
    batch, s, _ = sq.shape
    tm = ROW_BLOCK
    first = 1 if skip_ctx else 0
    full = pl.BlockSpec((None, s, SWA_KV_W), lambda b, j: (b, 0, 0))
    return pl.pallas_call(
        functools.partial(_swa_kernel, layer=layer, ctx_len=ctx_len, seq_len=s - ctx_len, first_block=first),
        grid=(batch, s // tm - first),
        in_specs=[pl.BlockSpec(memory_space=pltpu.SMEM),
                  pl.BlockSpec((None, tm, SWA_W), lambda b, j: (b, j + first, 0)), full, full],
        out_specs=pl.BlockSpec((None, tm, SWA_W), lambda b, j: (b, j, 0)),
        out_shape=jax.ShapeDtypeStruct((batch, s - first * tm, SWA_W), BF16),
        compiler_params=_params(2),
    )(sink, sq, sk, sv)


def _out_ffn_kernel(*refs, n_src, final):
    src_refs = refs[:n_src]
    (of_ref, ob_ref, gate_ref, na_ref, swa_ref, g1_ref, sh2_ref, sc2_ref, g2_ref, nffn_ref, gn_ref, avg_ref,
     wg_ref, wn_ref, ws_ref, wgate_ref, wup_ref, wdown_ref, fn_ref, o_ref) = refs[n_src:]
    o = of_ref[...] + ob_ref[...]
    hi, lo = _split2(o * o)
    avg = avg_ref[...]
    ms = _dot(hi, avg) + _dot(lo, avg)
    y = o * lax.rsqrt(ms + NORM_EPS) * gn_ref[...] * _silu(gate_ref[...])
    mix = _dot(y.astype(BF16), wg_ref[...]) + _dot(na_ref[...], wn_ref[...]) + _dot(swa_ref[...], ws_ref[...])
    x1 = _token_rows(src_refs, pl.program_id(1) == 0) + g1_ref[...] * mix
    h = _rms(x1) * nffn_ref[...]
    h = (h * (1.0 + sc2_ref[...]) + sh2_ref[...]).astype(BF16)

    d_ff = wgate_ref.shape[1]
    chunks = [slice(c0, min(c0 + FF_CHUNK, d_ff)) for c0 in range(0, d_ff, FF_CHUNK)]
    acc = jnp.zeros(x1.shape, F32)
    gu = {}
    for t in range(len(chunks) + 1):
        if t < len(chunks):
            gu[t] = (_dot(h, wgate_ref[:, chunks[t]]), _dot(h, wup_ref[:, chunks[t]]))
        if t >= 1:
            g, u = gu.pop(t - 1)
            acc = acc + _dot((_silu(g) * u).astype(BF16), wdown_ref[chunks[t - 1], :])
    x2 = x1 + g2_ref[...] * acc
    if final:
        x2 = _rms(x2) * fn_ref[...]
    o_ref[...] = x2


def _out_ffn(srcs, o_f, o_b, gla_in, y_na, y_swa, layer, mods, norm_ffn, gla_norm, avg, w_g, w_n, w_s,
             w_gate, w_up, w_down, final_norm, skip_ctx, final):
    batch, s, _ = o_f.shape
    d = srcs[0].shape[2]
    d_ff = w_gate.shape[2]
    tm = ROW_BLOCK
    off = 1 if skip_ctx else 0
    nblk = s // tm - off
    tok_off = off
    if skip_ctx and len(srcs) == 2:
        srcs, tok_off = srcs[1:], 0
    row = lambda w, cb=0: pl.BlockSpec((None, tm, w), lambda b, j: (b, j + off, cb))
    out = lambda w: pl.BlockSpec((None, tm, w), lambda b, j: (b, j, 0))
    ms = lambda chunk: _mod_spec(layer, chunk, batch, not skip_ctx)
    return pl.pallas_call(
        functools.partial(_out_ffn_kernel, n_src=len(srcs), final=final),
        grid=(batch, nblk),
        in_specs=_token_specs(srcs, tm, tok_off) + [
            row(GLA_W), row(GLA_W), row(GLA_W, 3), out(NA_W), out(SWA_W),
            ms(2), ms(3), ms(4), ms(5), _layer_spec(layer, (1, d)), _layer_spec(layer, (1, GLA_W)),
            pl.BlockSpec((GLA_W, GLA_W), lambda b, j: (0, 0)),
            _layer_spec(layer, (GLA_W, d)), _layer_spec(layer, (NA_W, d)), _layer_spec(layer, (SWA_W, d)),
            _layer_spec(layer, (d, d_ff)), _layer_spec(layer, (d, d_ff)), _layer_spec(layer, (d_ff, d)),
            pl.BlockSpec((1, d), lambda b, j: (0, 0))],
        out_specs=out(d),
        out_shape=jax.ShapeDtypeStruct((batch, nblk * tm, d), F32),
        compiler_params=_params(2),
    )(*srcs, o_f, o_b, gla_in, y_na, y_swa, mods, mods, mods, mods, norm_ffn, gla_norm, avg, w_g, w_n, w_s,
      w_gate, w_up, w_down, final_norm)


def _rope_tables(ctx_len, seq_len):
    t = np.arange(seq_len)
    n_freq = HEAD_DIM // 4
    inv_freq = ROPE_THETA ** (-np.arange(n_freq) / n_freq)
    ang = np.concatenate([(t // GRID_W)[:, None] * inv_freq, (t % GRID_W)[:, None] * inv_freq], axis=-1)
    cos_h = np.concatenate([np.cos(ang), np.cos(ang)], axis=-1)
    sin_h = np.concatenate([-np.sin(ang), np.sin(ang)], axis=-1)
    cos_t = np.concatenate([np.ones((ctx_len, HEAD_DIM)), cos_h], axis=0)
    sin_t = np.concatenate([np.zeros((ctx_len, HEAD_DIM)), sin_h], axis=0)
    return jnp.asarray(np.tile(cos_t, (1, 2)), F32), jnp.asarray(np.tile(sin_t, (1, 2)), F32)


def _swa_pair_order(a, axis):
    n_pairs = SWA_W // PAIR_W
    heads = [h for p in range(n_pairs) for h in (p, p + n_pairs)]
    return jnp.concatenate([lax.slice_in_dim(a, h * HEAD_DIM, (h + 1) * HEAD_DIM, axis=axis) for h in heads],
                           axis=axis)


def _permute_w_in_kernel(w_ref, o_ref):
    scale = HEAD_DIM ** -0.5
    scale2 = scale * LOG2E
    o = np.cumsum([0, GLA_W, GLA_W, GLA_W, GLA_W, GLA_RANK, GLA_RANK, NA_W, NA_W, NA_W, SWA_W, SWA_KV_W, SWA_KV_W])
    n_pairs = SWA_W // PAIR_W
    swa_heads = [h for p in range(n_pairs) for h in (p, p + n_pairs)]
    moves = [(o[0], o[1], C_GLA, scale), (o[1], o[4], C_GLA + GLA_W, 1.0), (o[4], o[6], C_ZA, 1.0),
             (o[6], o[7], C_NA, scale2), (o[7], o[9], C_NA + NA_W, 1.0), (o[10], o[12], C_SK, 1.0)]
    moves += [(o[9] + h * HEAD_DIM, o[9] + (h + 1) * HEAD_DIM, C_SQ + i * HEAD_DIM, scale2)
              for i, h in enumerate(swa_heads)]
    o_ref[:, C_ZA:C_NA] = jnp.zeros((o_ref.shape[0], LANES), o_ref.dtype)
    for lo, hi, dst, mult in moves:
        o_ref[:, dst:dst + (hi - lo)] = (w_ref[:, lo:hi] * mult).astype(o_ref.dtype)


def _permute_w_in(w):
    depth, d, n = w.shape
    tm = ROW_BLOCK
    return pl.pallas_call(
        _permute_w_in_kernel,
        grid=(depth, d // tm),
        in_specs=[pl.BlockSpec((None, tm, n), lambda i, k: (i, k, 0))],
        out_specs=pl.BlockSpec((None, tm, IN_PERM_W), lambda i, k: (i, k, 0)),
        out_shape=jax.ShapeDtypeStruct((depth, d, IN_PERM_W), BF16),
        compiler_params=_params(2),
    )(w)


def _na_bias_table(rpb):
    qc = np.arange(GRID_W)
    kc = np.arange(GRID_W)
    win = np.clip(qc - NA_KW // 2, 0, GRID_W - NA_KW)
    valid = (kc[None, :] >= win[:, None]) & (kc[None, :] < win[:, None] + NA_KW)
    dc = kc[None, :] - qc[:, None] + NA_KW - 1
    pick_col = ((dc[None] == np.arange(2 * NA_KW - 1)[:, None, None]) & valid[None]).astype(np.float32)
    t = jnp.einsum('lhde,eck->lhdck', rpb.astype(F32) * LOG2E, pick_col, precision=lax.Precision.HIGHEST)
    t = jnp.where(valid, t, -jnp.inf)
    return jnp.concatenate([t[:, :, :-1], t[:, :, 1:]], axis=-1)


def _pad_rank(wa2, offset):
    return jnp.pad(wa2, ((0, 0), (offset, LANES - offset - GLA_RANK), (0, 0)))


def kernel(x, c, ctx, c_ctx, w_mod, b_mod, norm_mix, norm_ffn, w_in, gla_wa2_f, gla_ba_f, gla_wa2_b, gla_ba_b,
           gla_norm, na_rpb, swa_sink, w_out, w_gate, w_up, w_down, final_norm):
    batch, seq_len, d = x.shape
    ctx_len = ctx.shape[1]
    depth = w_mod.shape[0]
    assert d == D_MODEL and ctx_len == ROW_BLOCK and seq_len % ROW_BLOCK == 0 and batch < 8
    assert seq_len % GRID_W == 0 and seq_len // GRID_W >= 2 * NA_KH

    cc = jnp.concatenate([c, c_ctx[None, :], jnp.zeros((8 - batch - 1, d), F32)], axis=0)
    mods = _modulation(cc, w_mod, b_mod).reshape(depth, 8, 1, 6 * d)
    cos_t, sin_t = _rope_tables(ctx_len, seq_len)
    head_avg = jnp.asarray(np.kron(np.eye(GLA_HEADS), np.full((HEAD_DIM, HEAD_DIM), 1.0 / HEAD_DIM)), BF16)
    w_in_p = _permute_w_in(w_in)
    wa_f, wa_b = _pad_rank(gla_wa2_f, 0), _pad_rank(gla_wa2_b, GLA_RANK)
    ba_f, ba_b = gla_ba_f.reshape(depth, 1, GLA_W), gla_ba_b.reshape(depth, 1, GLA_W)
    na_bias = _na_bias_table(na_rpb)
    w_o_gla = w_out[:, :GLA_W].astype(BF16)
    w_o_na = w_out[:, GLA_W:GLA_W + NA_W].astype(BF16)
    w_o_swa = _swa_pair_order(w_out[:, GLA_W + NA_W:], 1).astype(BF16)
    w_gate_b, w_up_b, w_down_b = w_gate.astype(BF16), w_up.astype(BF16), w_down.astype(BF16)
    norm_mix3, norm_ffn3 = norm_mix.reshape(depth, 1, d), norm_ffn.reshape(depth, 1, d)
    gla_norm3 = gla_norm.reshape(depth, 1, GLA_W)

    srcs = (ctx, x)
    for i in range(depth):
        last = i == depth - 1
        gla_in, za, na_qkv, sq, sk, sv = _in_proj(srcs, i, mods, norm_mix3, w_in_p, cos_t, sin_t)
        o_f, o_b = _gla(gla_in, za, i, wa_f, wa_b, ba_f, ba_b)
        y_na = _na(na_qkv, i, na_bias, ctx_len, skip_ctx=last)
        y_swa = _swa(swa_sink, i, sq, sk, sv, ctx_len, skip_ctx=last)
        srcs = (_out_ffn(srcs, o_f, o_b, gla_in, y_na, y_swa, i, mods, norm_ffn3, gla_norm3, head_avg,
                         w_o_gla, w_o_na, w_o_swa, w_gate_b, w_up_b, w_down_b, final_norm.reshape(1, d),
                         skip_ctx=last, final=last),)
    return srcs[0]
```

```python
import functools

import jax
import jax.numpy as jnp
import numpy as np
from jax import lax
from jax.experimental import pallas as pl
from jax.experimental.pallas import tpu as pltpu

F32 = jnp.float32
BF16 = jnp.bfloat16

D_MODEL = 1024
HEAD_DIM = 64
GRID_W = 64
GLA_HEADS = 4
NA_HEADS = 6
SWA_HEADS = 6
SWA_KV_HEADS = 2
GLA_W = GLA_HEADS * HEAD_DIM
NA_W = NA_HEADS * HEAD_DIM
SWA_W = SWA_HEADS * HEAD_DIM
SWA_KV_W = SWA_KV_HEADS * HEAD_DIM
GLA_RANK = 16
GLA_TAU = 16.0
GLA_CHUNK = 64
NA_KH = 8
NA_KW = 16
SWA_WINDOW = 128
ROPE_THETA = 10000.0
NORM_EPS = 1e-6

LANES = 128
ROW_BLOCK = 256
PAIR_W = 2 * HEAD_DIM
LOG2E = float(np.log2(np.e))
FF_CHUNK = 256
VMEM_LIMIT = 56 * 1024 * 1024

C_GLA = 0
C_ZA = C_GLA + 4 * GLA_W
C_NA = C_ZA + LANES
C_SQ = C_NA + 3 * NA_W
C_SK = C_SQ + SWA_W
C_SV = C_SK + SWA_KV_W
IN_PERM_W = C_SV + SWA_KV_W


def _dot(a, b):
    return jnp.dot(a, b, preferred_element_type=F32)


def _dot_nt(a, b):
    return lax.dot_general(a, b, (((1,), (1,)), ((), ())), preferred_element_type=F32)


def _dot_tn(a, b):
    return lax.dot_general(a, b, (((0,), (0,)), ((), ())), preferred_element_type=F32)


def _idiv(a, n):
    assert n & (n - 1) == 0
    return a >> (n.bit_length() - 1)


def _imod(a, n):
    assert n & (n - 1) == 0
    return a & (n - 1)


def _split2(a):
    hi = a.astype(BF16)
    lo = (a - hi.astype(F32)).astype(BF16)
    return hi, lo


def _split3(a):
    hi = a.astype(BF16)
    r = a - hi.astype(F32)
    mid = r.astype(BF16)
    lo = (r - mid.astype(F32)).astype(BF16)
    return hi, mid, lo


def _dot_split(a, b):
    ah, al = _split2(a)
    bh, bl = _split2(b)
    return _dot(ah, bh) + _dot(al, bh) + _dot(ah, bl)


def _silu(a):
    return a * jax.nn.sigmoid(a)


def _rms(x):
    return x * lax.rsqrt(jnp.mean(x * x, axis=-1, keepdims=True) + NORM_EPS)


def _params(n_parallel, n_arbitrary=0):
    return pltpu.CompilerParams(
        dimension_semantics=("parallel",) * n_parallel + ("arbitrary",) * n_arbitrary,
        vmem_limit_bytes=VMEM_LIMIT)


def _mod_kernel(c_ref, w_ref, b_ref, o_ref):
    o_ref[...] = _dot_split(_silu(c_ref[...]), w_ref[...]) + b_ref[...]


def _modulation(cc, w_mod, b_mod):
    depth, d, n = w_mod.shape
    tn = 1536
    return pl.pallas_call(
        _mod_kernel,
        grid=(depth, n // tn),
        in_specs=[
            pl.BlockSpec((8, d), lambda i, k: (0, 0)),
            pl.BlockSpec((None, d, tn), lambda i, k: (i, 0, k)),
            pl.BlockSpec((None, 1, tn), lambda i, k: (i, 0, k)),
        ],
        out_specs=pl.BlockSpec((None, 8, tn), lambda i, k: (i, 0, k)),
        out_shape=jax.ShapeDtypeStruct((depth, 8, n), F32),
        compiler_params=_params(2),
    )(cc, w_mod, b_mod.reshape(depth, 1, n))


def _mod_spec(layer, chunk, batch, ctx_first):
    if ctx_first:
        return pl.BlockSpec((None, None, 1, D_MODEL), lambda b, j: (layer, jnp.where(j == 0, batch, b), 0, chunk))
    return pl.BlockSpec((None, None, 1, D_MODEL), lambda b, j: (layer, b, 0, chunk))


def _layer_spec(layer, shape):
    return pl.BlockSpec((None,) + shape, lambda b, j: (layer,) + (0,) * len(shape))


def _token_rows(src_refs):
    if len(src_refs) == 1:
        return src_refs[0][...]
    ctx_ref, x_ref = src_refs
    return jnp.where(pl.program_id(1) == 0, ctx_ref[...], x_ref[...])


def _token_specs(srcs, tm, off=0):
    d = srcs[0].shape[-1]
    if len(srcs) == 1:
        return [pl.BlockSpec((None, tm, d), lambda b, j: (b, j + off, 0))]
    assert off == 0
    return [pl.BlockSpec((None, tm, d), lambda b, j: (b, 0, 0)),
            pl.BlockSpec((None, tm, d), lambda b, j: (b, jnp.maximum(j - 1, 0), 0))]


def _in_proj_kernel(*refs, n_src):
    src_refs = refs[:n_src]
    sh_ref, sc_ref, nw_ref, w_ref, cos_ref, sin_ref, gla_ref, za_ref, na_ref, sq_ref, sk_ref, sv_ref = refs[n_src:]
    x = _token_rows(src_refs)
    tm = x.shape[0]
    lane = lax.broadcasted_iota(jnp.int32, (tm // 2, LANES), 1)
    first_half = _imod(lane, HEAD_DIM) < (HEAD_DIM // 2)
    n_q = SWA_W // LANES
    for rows in (slice(0, tm // 2), slice(tm // 2, tm)):
        h = _rms(x[rows]) * nw_ref[...]
        h = (h * (1.0 + sc_ref[...]) + sh_ref[...]).astype(BF16)
        qk = _dot(h, w_ref[:, C_SQ:C_SV])
        cos = cos_ref[rows, :]
        sin = sin_ref[rows, :]
        for cb in range(n_q + 1):
            a = qk[:, cb * LANES:(cb + 1) * LANES]
            partner = jnp.where(first_half, pltpu.roll(a, LANES - HEAD_DIM // 2, 1), pltpu.roll(a, HEAD_DIM // 2, 1))
            r = (a * cos + partner * sin).astype(BF16)
            if cb < n_q:
                sq_ref[rows, cb * LANES:(cb + 1) * LANES] = r
            else:
                sk_ref[rows, :] = r
        gla_ref[rows, :] = _dot(h, w_ref[:, C_GLA:C_ZA])
        za_ref[rows, :] = _dot(h, w_ref[:, C_ZA:C_NA])
        na_ref[rows, :] = _dot(h, w_ref[:, C_NA:C_SQ]).astype(BF16)
        sv_ref[rows, :] = _dot(h, w_ref[:, C_SV:IN_PERM_W]).astype(BF16)


def _in_proj(srcs, layer, mods, norm_w, w_perm, cos_t, sin_t):
    batch, d = srcs[0].shape[0], srcs[0].shape[2]
    s = sum(a.shape[1] for a in srcs)
    tm = ROW_BLOCK
    row = lambda w: pl.BlockSpec((None, tm, w), lambda b, j: (b, j, 0))
    tab = pl.BlockSpec((tm, LANES), lambda b, j: (j, 0))
    return pl.pallas_call(
        functools.partial(_in_proj_kernel, n_src=len(srcs)),
        grid=(batch, s // tm),
        in_specs=_token_specs(srcs, tm) + [
            _mod_spec(layer, 0, batch, True), _mod_spec(layer, 1, batch, True), _layer_spec(layer, (1, d)),
            _layer_spec(layer, (d, IN_PERM_W)), tab, tab],
        out_specs=[row(4 * GLA_W), row(LANES), row(3 * NA_W), row(SWA_W), row(SWA_KV_W), row(SWA_KV_W)],
        out_shape=[
            jax.ShapeDtypeStruct((batch, s, 4 * GLA_W), F32),
            jax.ShapeDtypeStruct((batch, s, LANES), F32),
            jax.ShapeDtypeStruct((batch, s, 3 * NA_W), BF16),
            jax.ShapeDtypeStruct((batch, s, SWA_W), BF16),
            jax.ShapeDtypeStruct((batch, s, SWA_KV_W), BF16),
            jax.ShapeDtypeStruct((batch, s, SWA_KV_W), BF16),
        ],
        compiler_params=_params(2),
    )(*srcs, mods, mods, norm_w, w_perm, cos_t, sin_t)


def _gla_stages(qf_ref, kf_ref, vf_ref, zf_ref, qb_ref, kb_ref, vb_ref, zb_ref,
                waf_ref, wab_ref, baf_ref, bab_ref, of_ref, ob_ref, stf_ref, stb_ref):
    c = GLA_CHUNK
    w = GLA_W
    n_chunks = ROW_BLOCK // c
    r64 = lax.broadcasted_iota(jnp.int32, (c, c), 0)
    c64 = lax.broadcasted_iota(jnp.int32, (c, c), 1)
    row_h = _idiv(lax.broadcasted_iota(jnp.int32, (GLA_HEADS * c, w), 0), c)
    col_h = _idiv(lax.broadcasted_iota(jnp.int32, (GLA_HEADS * c, w), 1), c)
    same_head = row_h == col_h
    ar = _imod(lax.broadcasted_iota(jnp.int32, (GLA_HEADS * c, c), 0), c)
    ac = lax.broadcasted_iota(jnp.int32, (GLA_HEADS * c, c), 1)
    out_head = _idiv(lax.broadcasted_iota(jnp.int32, (c, w), 1), c)
    dirs = [
        dict(q=qf_ref, k=kf_ref, v=vf_ref, z=zf_ref, wa=waf_ref, ba=baf_ref, o=of_ref, st=stf_ref, rev=False,
             order=range(n_chunks), cum=(r64 >= c64).astype(BF16), causal=ar >= ac),
        dict(q=qb_ref, k=kb_ref, v=vb_ref, z=zb_ref, wa=wab_ref, ba=bab_ref, o=ob_ref, st=stb_ref, rev=True,
             order=range(n_chunks - 1, -1, -1), cum=(r64 <= c64).astype(BF16), causal=ar <= ac),
    ]
    units = [(d, ci) for d in dirs for ci in d['order']]
    rows = lambda ci: slice(ci * c, (ci + 1) * c)
    t = {}

    def decay():
        log_a = []
        for d in dirs:
            z = _dot_split(d['z'][...], d['wa'][...]) + d['ba'][...]
            log_a.append((jnp.minimum(z, 0.0) - jnp.log1p(jnp.exp(-jnp.abs(z)))) * (1.0 / GLA_TAU))
        t['b'], t['b_tot'] = [], []
        for (d, ci), la in zip(units, [la for la in log_a for _ in range(n_chunks)]):
            hi, mid, lo = _split3(la[rows(ci)])
            t['b'].append(_dot(d['cum'], hi) + _dot(d['cum'], mid) + _dot(d['cum'], lo))
            t['b_tot'].append(t['b'][-1][0:1] if d['rev'] else t['b'][-1][c - 1:c])

    def scale():
        b, b_tot = t['b'], t['b_tot']
        k = [d['k'][rows(ci), :] for d, ci in units]
        t['v'] = [d['v'][rows(ci), :].astype(BF16) for d, ci in units]
        t['q_in'] = [(d['q'][rows(ci), :] * jnp.exp(bu)).astype(BF16) for (d, ci), bu in zip(units, b)]
        t['k_in'] = [(ku * jnp.exp(-bu)).astype(BF16) for ku, bu in zip(k, b)]
        t['k_end'] = [(ku * jnp.exp(bt - bu)).astype(BF16) for ku, bu, bt in zip(k, b, b_tot)]
        t['qx'] = [jnp.where(same_head, jnp.concatenate([qu] * GLA_HEADS, axis=0), jnp.zeros((), BF16))
                   for qu in t['q_in']]

    def intra():
        a = [jnp.where(d['causal'], _dot_nt(qxu, ku), 0.0).astype(BF16)
             for (d, _), qxu, ku in zip(units, t['qx'], t['k_in'])]
        t['r'] = [_dot(au, vu) for au, vu in zip(a, t['v'])]
        t['upd'] = [jnp.where(same_head, _dot_tn(vu, ku), 0.0) for vu, ku in zip(t['v'], t['k_end'])]

    def inter():
        st_enter = []
        for di, d in enumerate(dirs):
            st = d['st'][...]
            for u in range(di * n_chunks, (di + 1) * n_chunks):
                st_enter.append(st.astype(BF16))
                st = st * jnp.exp(t['b_tot'][u]) + t['upd'][u]
            d['st'][...] = st
        for (d, ci), qu, su, ru in zip(units, t['q_in'], st_enter, t['r']):
            o = _dot_nt(qu, su)
            for h in range(GLA_HEADS):
                o = o + jnp.where(out_head == h, ru[h * c:(h + 1) * c], 0.0)
            d['o'][rows(ci), :] = o

    return [decay, scale, intra, inter]


def _stack_pair(q):
    left = lax.broadcasted_iota(jnp.int32, q.shape, 1) < HEAD_DIM
    zero = jnp.zeros((), q.dtype)
    return jnp.concatenate([jnp.where(left, q, zero), jnp.where(left, zero, q)], axis=0)


def _unstack_pair(o):
    n = o.shape[0] // 2
    left = lax.broadcasted_iota(jnp.int32, (n, o.shape[1]), 1) < HEAD_DIM
    return jnp.where(left, o[:n], o[n:])


def _softmax_weights(scores, extra_logit=None):
    m = functools.reduce(jnp.maximum, [jnp.max(s, axis=-1, keepdims=True) for s in scores])
    if extra_logit is not None:
        m = jnp.maximum(m, extra_logit)
    ps = [jnp.exp2(s - m) for s in scores]
    denom = functools.reduce(jnp.add, [jnp.sum(p, axis=-1, keepdims=True) for p in ps])
    if extra_logit is not None:
        denom = denom + jnp.exp2(extra_logit - m)
    return [p.astype(BF16) for p in ps], denom


def _na_stages(q_ref, k_ref, v_ref, bias_ref, o_ref, j, ctx_len, grid_rows, is_ctx):
    n_pairs = NA_W // PAIR_W
    rows_per_block = ROW_BLOCK // GRID_W
    n_loc = NA_KH * GRID_W
    unit = 2 * GRID_W
    pair_cols = [slice(p * PAIR_W, (p + 1) * PAIR_W) for p in range(n_pairs)]
    units = [(rr, p) for p in range(n_pairs) for rr in range(rows_per_block)]
    ctx_rows = lambda rr: slice(rr * unit, (rr + 1) * unit)
    t = {}

    def scores():
        qx = {(rr, p): _stack_pair(q_ref[rr * GRID_W:(rr + 1) * GRID_W, pair_cols[p]]) for rr, p in units}
        t['s_ctx'] = [_dot_nt(jnp.concatenate([qx[rr, p] for rr in range(rows_per_block)], axis=0),
                              k_ref[0:ctx_len, pair_cols[p]]) for p in range(n_pairs)]
        if is_ctx:
            return
        t['k0'], bias = [], {}
        for rr in range(rows_per_block):
            r = (j - 1) * rows_per_block + rr
            first_key_row = jnp.clip(r - NA_KH // 2, 0, grid_rows - NA_KH)
            t['k0'].append(pl.multiple_of(ctx_len + first_key_row * GRID_W, GRID_W))
            d0 = first_key_row - r + NA_KH - 1
            for p in range(n_pairs):
                bias[rr, p] = jnp.concatenate(
                    [jnp.concatenate([bias_ref[2 * p + side, d0 + 2 * m] for m in range(NA_KH // 2)], axis=1)
                     for side in (0, 1)], axis=0)
        t['s_loc'] = {(rr, p): _dot_nt(qx[rr, p], k_ref[pl.ds(t['k0'][rr], n_loc), pair_cols[p]]) + bias[rr, p]
                      for rr, p in units}

    def softmax():
        t['w'] = {(rr, p): _softmax_weights(([] if is_ctx else [t['s_loc'][rr, p]]) + [t['s_ctx'][p][ctx_rows(rr)]])
                  for rr, p in units}

    def values():
        w = t['w']
        o_ctx = [_dot(jnp.concatenate([w[rr, p][0][-1] for rr in range(rows_per_block)], axis=0),
                      v_ref[0:ctx_len, pair_cols[p]]) for p in range(n_pairs)]
        for rr, p in units:
            o = o_ctx[p][ctx_rows(rr)]
            if not is_ctx:
                o = o + _dot(w[rr, p][0][0], v_ref[pl.ds(t['k0'][rr], n_loc), pair_cols[p]])
            o_ref[rr * GRID_W:(rr + 1) * GRID_W, pair_cols[p]] = _unstack_pair(o / w[rr, p][1]).astype(o_ref.dtype)

    return [scores, softmax, values]


def _swa_stages(sink, q_ref, k_ref, v_ref, o_ref, j, ctx_len, seq_len, is_ctx):
    tm = ROW_BLOCK
    n_pairs = SWA_W // PAIR_W
    n_loc = tm + 2 * SWA_WINDOW
    pair_cols = [slice(p * PAIR_W, (p + 1) * PAIR_W) for p in range(n_pairs)]
    lane = lax.broadcasted_iota(jnp.int32, (tm, PAIR_W), 1)
    zero = jnp.zeros((), BF16)
    units = [(p + side * n_pairs, p, side) for p in range(n_pairs) for side in (0, 1)]
    t = {}

    def scores():
        qx = [jnp.where((lane < HEAD_DIM) == (side == 0), q_ref[:, pair_cols[p]], zero) for _, p, side in units]
        k_ctx = k_ref[0:ctx_len, :]
        t['s'] = [[_dot_nt(q, k_ctx)] for q in qx]
        if is_ctx:
            return
        q0 = (j - 1) * tm
        k0 = jnp.clip(q0 - SWA_WINDOW, 0, seq_len - n_loc)
        qpos = q0 + lax.broadcasted_iota(jnp.int32, (tm, n_loc), 0)
        kpos = k0 + lax.broadcasted_iota(jnp.int32, (tm, n_loc), 1)
        in_window = jnp.abs(qpos - kpos) <= SWA_WINDOW
        t['start'] = pl.multiple_of(ctx_len + k0, SWA_WINDOW)
        k_loc = k_ref[pl.ds(t['start'], n_loc), :]
        for s, q in zip(t['s'], qx):
            s.insert(0, jnp.where(in_window, _dot_nt(q, k_loc), -jnp.inf))

    def softmax():
        t['w'] = [_softmax_weights(s, sink[h] * LOG2E) for s, (h, _, _) in zip(t['s'], units)]

    def values():
        vs = [v_ref[0:ctx_len, :]]
        if not is_ctx:
            vs.insert(0, v_ref[pl.ds(t['start'], n_loc), :])
        o = [functools.reduce(jnp.add, [_dot(p, v) for p, v in zip(ps, vs)]) / denom for ps, denom in t['w']]
        for p, cols in enumerate(pair_cols):
            o_ref[:, cols] = jnp.where(lane < HEAD_DIM, o[2 * p], o[2 * p + 1]).astype(o_ref.dtype)

    return [scores, softmax, values]


def _mixers_kernel(sinks_ref, qf_ref, kf_ref, vf_ref, zf_ref, qb_ref, kb_ref, vb_ref, zb_ref,
                   waf_ref, wab_ref, baf_ref, bab_ref, nq_ref, nk_ref, nv_ref, nbias_ref, sq_ref, sk_ref, sv_ref,
                   of_ref, ob_ref, yna_ref, yswa_ref, stf_ref, stb_ref, *, layer, ctx_len, seq_len, ctx_out):
    j = pl.program_id(1)

    @pl.when(j == 0)
    def _():
        stf_ref[...] = jnp.zeros_like(stf_ref)
        stb_ref[...] = jnp.zeros_like(stb_ref)

    sink = [sinks_ref[layer, h] for h in range(SWA_HEADS)]

    def block(is_ctx):
        gla = _gla_stages(qf_ref, kf_ref, vf_ref, zf_ref, qb_ref, kb_ref, vb_ref, zb_ref,
                          waf_ref, wab_ref, baf_ref, bab_ref, of_ref, ob_ref, stf_ref, stb_ref)
        if is_ctx and not ctx_out:
            order = gla
        else:
            na = _na_stages(nq_ref, nk_ref, nv_ref, nbias_ref, yna_ref, j, ctx_len, seq_len // GRID_W, is_ctx)
            swa = _swa_stages(sink, sq_ref, sk_ref, sv_ref, yswa_ref, j, ctx_len, seq_len, is_ctx)
            order = [gla[0], na[0], swa[0], gla[1], gla[2], na[1], na[2], swa[1], swa[2], gla[3]]
        for stage in order:
            stage()

    pl.when(j == 0)(functools.partial(block, True))
    pl.when(j > 0)(functools.partial(block, False))


def _mixers(gla_in, za, na_qkv, sq, sk, sv, layer, wa_f, wa_b, ba_f, ba_b, na_bias, sinks, ctx_len, ctx_out):
    batch, s, _ = gla_in.shape
    tm = ROW_BLOCK
    nblk = s // tm
    first = 0 if ctx_out else 1
    fwd = lambda j: j
    bwd = lambda j: jnp.where(j == 0, 0, nblk - j)
    soft_out = lambda j: jnp.maximum(j - first, 0)

    def col(order, cb):
        return pl.BlockSpec((None, tm, GLA_W), lambda b, j: (b, order(j), cb))

    def rows(order, w):
        return pl.BlockSpec((None, tm, w), lambda b, j: (b, order(j), 0))

    def full(w, cb=0):
        return pl.BlockSpec((None, s, w), lambda b, j: (b, 0, cb))

    return pl.pallas_call(
        functools.partial(_mixers_kernel, layer=layer, ctx_len=ctx_len, seq_len=s - ctx_len, ctx_out=ctx_out),
        grid=(batch, nblk),
        in_specs=[pl.BlockSpec(memory_space=pltpu.SMEM),
                  col(fwd, 0), col(fwd, 1), col(fwd, 2), rows(fwd, LANES),
                  col(bwd, 0), col(bwd, 1), col(bwd, 2), rows(bwd, LANES),
                  _layer_spec(layer, (LANES, GLA_W)), _layer_spec(layer, (LANES, GLA_W)),
                  _layer_spec(layer, (1, GLA_W)), _layer_spec(layer, (1, GLA_W)),
                  rows(fwd, NA_W), full(NA_W, 1), full(NA_W, 2), _layer_spec(layer, na_bias.shape[1:]),
                  rows(fwd, SWA_W), full(SWA_KV_W), full(SWA_KV_W)],
        out_specs=[col(fwd, 0), col(bwd, 0), rows(soft_out, NA_W), rows(soft_out, SWA_W)],
        out_shape=[jax.ShapeDtypeStruct((batch, s, GLA_W), F32)] * 2
        + [jax.ShapeDtypeStruct((batch, s - first * tm, w), BF16) for w in (NA_W, SWA_W)],
        scratch_shapes=[pltpu.VMEM((GLA_W, GLA_W), F32)] * 2,
        compiler_params=_params(1, 1),
    )(sinks, gla_in, gla_in, gla_in, za, gla_in, gla_in, gla_in, za, wa_f, wa_b, ba_f, ba_b,
      na_qkv, na_qkv, na_qkv, na_bias, sq, sk, sv)


def _out_ffn_kernel(*refs, n_src, final):
    src_refs = refs[:n_src]
    (of_ref, ob_ref, gate_ref, na_ref, swa_ref, g1_ref, sh2_ref, sc2_ref, g2_ref, nffn_ref, gn_ref, avg_ref,
     wg_ref, wn_ref, ws_ref, wgate_ref, wup_ref, wdown_ref, fn_ref, o_ref) = refs[n_src:]
    o = of_ref[...] + ob_ref[...]
    hi, lo = _split2(o * o)
    avg = avg_ref[...]
    ms = _dot(hi, avg) + _dot(lo, avg)
    y = o * lax.rsqrt(ms + NORM_EPS) * gn_ref[...] * _silu(gate_ref[...])
    mix = _dot(y.astype(BF16), wg_ref[...]) + _dot(na_ref[...], wn_ref[...]) + _dot(swa_ref[...], ws_ref[...])
    x1 = _token_rows(src_refs) + g1_ref[...] * mix
    h = _rms(x1) * nffn_ref[...]
    h = (h * (1.0 + sc2_ref[...]) + sh2_ref[...]).astype(BF16)

    d_ff = wgate_ref.shape[1]
    chunks = [slice(c0, min(c0 + FF_CHUNK, d_ff)) for c0 in range(0, d_ff, FF_CHUNK)]
    acc = jnp.zeros(x1.shape, F32)
    gu = {}
    for t in range(len(chunks) + 1):
        if t < len(chunks):
            gu[t] = (_dot(h, wgate_ref[:, chunks[t]]), _dot(h, wup_ref[:, chunks[t]]))
        if t >= 1:
            g, u = gu.pop(t - 1)
            acc = acc + _dot((_silu(g) * u).astype(BF16), wdown_ref[chunks[t - 1], :])
    x2 = x1 + g2_ref[...] * acc
    if final:
        x2 = _rms(x2) * fn_ref[...]
    o_ref[...] = x2


def _out_ffn(srcs, o_f, o_b, gla_in, y_na, y_swa, layer, mods, norm_ffn, gla_norm, avg, w_g, w_n, w_s,
             w_gate, w_up, w_down, final_norm, skip_ctx, final):
    batch, s, _ = o_f.shape
    d = srcs[0].shape[2]
    d_ff = w_gate.shape[2]
    tm = ROW_BLOCK
    off = 1 if skip_ctx else 0
    nblk = s // tm - off
    tok_off = off
    if skip_ctx and len(srcs) == 2:
        srcs, tok_off = srcs[1:], 0
    row = lambda w, cb=0: pl.BlockSpec((None, tm, w), lambda b, j: (b, j + off, cb))
    out = lambda w: pl.BlockSpec((None, tm, w), lambda b, j: (b, j, 0))
    ms = lambda chunk: _mod_spec(layer, chunk, batch, not skip_ctx)
    return pl.pallas_call(
        functools.partial(_out_ffn_kernel, n_src=len(srcs), final=final),
        grid=(batch, nblk),
        in_specs=_token_specs(srcs, tm, tok_off) + [
            row(GLA_W), row(GLA_W), row(GLA_W, 3), out(NA_W), out(SWA_W),
            ms(2), ms(3), ms(4), ms(5), _layer_spec(layer, (1, d)), _layer_spec(layer, (1, GLA_W)),
            pl.BlockSpec((GLA_W, GLA_W), lambda b, j: (0, 0)),
            _layer_spec(layer, (GLA_W, d)), _layer_spec(layer, (NA_W, d)), _layer_spec(layer, (SWA_W, d)),
            _layer_spec(layer, (d, d_ff)), _layer_spec(layer, (d, d_ff)), _layer_spec(layer, (d_ff, d)),
            pl.BlockSpec((1, d), lambda b, j: (0, 0))],
        out_specs=out(d),
        out_shape=jax.ShapeDtypeStruct((batch, nblk * tm, d), F32),
        compiler_params=_params(2),
    )(*srcs, o_f, o_b, gla_in, y_na, y_swa, mods, mods, mods, mods, norm_ffn, gla_norm, avg, w_g, w_n, w_s,
      w_gate, w_up, w_down, final_norm)


def _rope_tables(ctx_len, seq_len):
    t = np.arange(seq_len)
    n_freq = HEAD_DIM // 4
    inv_freq = ROPE_THETA ** (-np.arange(n_freq) / n_freq)
    ang = np.concatenate([(t // GRID_W)[:, None] * inv_freq, (t % GRID_W)[:, None] * inv_freq], axis=-1)
    cos_h = np.concatenate([np.cos(ang), np.cos(ang)], axis=-1)
    sin_h = np.concatenate([-np.sin(ang), np.sin(ang)], axis=-1)
    cos_t = np.concatenate([np.ones((ctx_len, HEAD_DIM)), cos_h], axis=0)
    sin_t = np.concatenate([np.zeros((ctx_len, HEAD_DIM)), sin_h], axis=0)
    return jnp.asarray(np.tile(cos_t, (1, 2)), F32), jnp.asarray(np.tile(sin_t, (1, 2)), F32)


def _swa_pair_order(a, axis):
    n_pairs = SWA_W // PAIR_W
    heads = [h for p in range(n_pairs) for h in (p, p + n_pairs)]
    return jnp.concatenate([lax.slice_in_dim(a, h * HEAD_DIM, (h + 1) * HEAD_DIM, axis=axis) for h in heads],
                           axis=axis)


def _permute_w_in_kernel(w_ref, o_ref):
    scale = HEAD_DIM ** -0.5
    scale2 = scale * LOG2E
    o = np.cumsum([0, GLA_W, GLA_W, GLA_W, GLA_W, GLA_RANK, GLA_RANK, NA_W, NA_W, NA_W, SWA_W, SWA_KV_W, SWA_KV_W])
    n_pairs = SWA_W // PAIR_W
    swa_heads = [h for p in range(n_pairs) for h in (p, p + n_pairs)]
    moves = [(o[0], o[1], C_GLA, scale), (o[1], o[4], C_GLA + GLA_W, 1.0), (o[4], o[6], C_ZA, 1.0),
             (o[6], o[7], C_NA, scale2), (o[7], o[9], C_NA + NA_W, 1.0), (o[10], o[12], C_SK, 1.0)]
    moves += [(o[9] + h * HEAD_DIM, o[9] + (h + 1) * HEAD_DIM, C_SQ + i * HEAD_DIM, scale2)
              for i, h in enumerate(swa_heads)]
    o_ref[:, C_ZA:C_NA] = jnp.zeros((o_ref.shape[0], LANES), o_ref.dtype)
    for lo, hi, dst, mult in moves:
        o_ref[:, dst:dst + (hi - lo)] = (w_ref[:, lo:hi] * mult).astype(o_ref.dtype)


def _permute_w_in(w):
    depth, d, n = w.shape
    tm = ROW_BLOCK
    return pl.pallas_call(
        _permute_w_in_kernel,
        grid=(depth, d // tm),
        in_specs=[pl.BlockSpec((None, tm, n), lambda i, k: (i, k, 0))],
        out_specs=pl.BlockSpec((None, tm, IN_PERM_W), lambda i, k: (i, k, 0)),
        out_shape=jax.ShapeDtypeStruct((depth, d, IN_PERM_W), BF16),
        compiler_params=_params(2),
    )(w)


def _na_bias_table(rpb):
    qc = np.arange(GRID_W)
    kc = np.arange(GRID_W)
    win = np.clip(qc - NA_KW // 2, 0, GRID_W - NA_KW)
    valid = (kc[None, :] >= win[:, None]) & (kc[None, :] < win[:, None] + NA_KW)
    dc = kc[None, :] - qc[:, None] + NA_KW - 1
    pick_col = ((dc[None] == np.arange(2 * NA_KW - 1)[:, None, None]) & valid[None]).astype(np.float32)
    t = jnp.einsum('lhde,eck->lhdck', rpb.astype(F32) * LOG2E, pick_col, precision=lax.Precision.HIGHEST)
    t = jnp.where(valid, t, -jnp.inf)
    return jnp.concatenate([t[:, :, :-1], t[:, :, 1:]], axis=-1)


def _pad_rank(wa2, offset):
    return jnp.pad(wa2, ((0, 0), (offset, LANES - offset - GLA_RANK), (0, 0)))


def kernel(x, c, ctx, c_ctx, w_mod, b_mod, norm_mix, norm_ffn, w_in, gla_wa2_f, gla_ba_f, gla_wa2_b, gla_ba_b,
           gla_norm, na_rpb, swa_sink, w_out, w_gate, w_up, w_down, final_norm):
    batch, seq_len, d = x.shape
    ctx_len = ctx.shape[1]
    depth = w_mod.shape[0]
    assert d == D_MODEL and ctx_len == ROW_BLOCK and seq_len % ROW_BLOCK == 0 and batch < 8
    assert seq_len % GRID_W == 0 and seq_len // GRID_W >= 2 * NA_KH

    cc = jnp.concatenate([c, c_ctx[None, :], jnp.zeros((8 - batch - 1, d), F32)], axis=0)
    mods = _modulation(cc, w_mod, b_mod).reshape(depth, 8, 1, 6 * d)
    cos_t, sin_t = _rope_tables(ctx_len, seq_len)
    head_avg = jnp.asarray(np.kron(np.eye(GLA_HEADS), np.full((HEAD_DIM, HEAD_DIM), 1.0 / HEAD_DIM)), BF16)
    w_in_p = _permute_w_in(w_in)
    wa_f, wa_b = _pad_rank(gla_wa2_f, 0), _pad_rank(gla_wa2_b, GLA_RANK)
    ba_f, ba_b = gla_ba_f.reshape(depth, 1, GLA_W), gla_ba_b.reshape(depth, 1, GLA_W)
    na_bias = _na_bias_table(na_rpb)
    w_o_gla = w_out[:, :GLA_W].astype(BF16)
    w_o_na = w_out[:, GLA_W:GLA_W + NA_W].astype(BF16)
    w_o_swa = _swa_pair_order(w_out[:, GLA_W + NA_W:], 1).astype(BF16)
    w_gate_b, w_up_b, w_down_b = w_gate.astype(BF16), w_up.astype(BF16), w_down.astype(BF16)
    norm_mix3, norm_ffn3 = norm_mix.reshape(depth, 1, d), norm_ffn.reshape(depth, 1, d)
    gla_norm3 = gla_norm.reshape(depth, 1, GLA_W)

    srcs = (ctx, x)
    for i in range(depth):
        last = i == depth - 1
        gla_in, za, na_qkv, sq, sk, sv = _in_proj(srcs, i, mods, norm_mix3, w_in_p, cos_t, sin_t)
        o_f, o_b, y_na, y_swa = _mixers(gla_in, za, na_qkv, sq, sk, sv, i, wa_f, wa_b, ba_f, ba_b, na_bias,
                                        swa_sink, ctx_len, ctx_out=not last)
        srcs = (_out_ffn(srcs, o_f, o_b, gla_in, y_na, y_swa, i, mods, norm_ffn3, gla_norm3, head_avg,
                         w_o_gla, w_o_na, w_o_swa, w_gate_b, w_up_b, w_down_b, final_norm.reshape(1, d),
                         skip_ctx=last, final=last),)
    return srcs[0]
```

```python
import functools

import jax
import jax.numpy as jnp
import numpy as np
from jax import lax
from jax.experimental import pallas as pl
from jax.experimental.pallas import tpu as pltpu

F32 = jnp.float32
BF16 = jnp.bfloat16

D_MODEL = 1024
HEAD_DIM = 64
GRID_W = 64
GLA_HEADS = 4
NA_HEADS = 6
SWA_HEADS = 6
SWA_KV_HEADS = 2
GLA_W = GLA_HEADS * HEAD_DIM
NA_W = NA_HEADS * HEAD_DIM
SWA_W = SWA_HEADS * HEAD_DIM
SWA_KV_W = SWA_KV_HEADS * HEAD_DIM
GLA_RANK = 16
GLA_TAU = 16.0
GLA_CHUNK = 64
NA_KH = 8
NA_KW = 16
SWA_WINDOW = 128
ROPE_THETA = 10000.0
NORM_EPS = 1e-6

LANES = 128
ROW_BLOCK = 256
PAIR_W = 2 * HEAD_DIM
LOG2E = float(np.log2(np.e))
FF_CHUNK = 256
VMEM_LIMIT = 56 * 1024 * 1024

C_GLA = 0
C_ZA = C_GLA + 4 * GLA_W
C_NA = C_ZA + LANES
C_SQ = C_NA + 3 * NA_W
C_SK = C_SQ + SWA_W
C_SV = C_SK + SWA_KV_W
IN_PERM_W = C_SV + SWA_KV_W


def _dot(a, b):
    return jnp.dot(a, b, preferred_element_type=F32)


def _dot_nt(a, b):
    return lax.dot_general(a, b, (((1,), (1,)), ((), ())), preferred_element_type=F32)


def _dot_tn(a, b):
    return lax.dot_general(a, b, (((0,), (0,)), ((), ())), preferred_element_type=F32)


def _idiv(a, n):
    assert n & (n - 1) == 0
    return a >> (n.bit_length() - 1)


def _imod(a, n):
    assert n & (n - 1) == 0
    return a & (n - 1)


def _split2(a):
    hi = a.astype(BF16)
    lo = (a - hi.astype(F32)).astype(BF16)
    return hi, lo


def _split3(a):
    hi = a.astype(BF16)
    r = a - hi.astype(F32)
    mid = r.astype(BF16)
    lo = (r - mid.astype(F32)).astype(BF16)
    return hi, mid, lo


def _dot_split(a, b):
    ah, al = _split2(a)
    bh, bl = _split2(b)
    return _dot(ah, bh) + _dot(al, bh) + _dot(ah, bl)


def _silu(a):
    return a * jax.nn.sigmoid(a)


def _rms(x):
    return x * lax.rsqrt(jnp.mean(x * x, axis=-1, keepdims=True) + NORM_EPS)


def _params(n_parallel, n_arbitrary=0):
    return pltpu.CompilerParams(
        dimension_semantics=("parallel",) * n_parallel + ("arbitrary",) * n_arbitrary,
        vmem_limit_bytes=VMEM_LIMIT)


def _mod_kernel(c_ref, w_ref, b_ref, o_ref):
    o_ref[...] = _dot_split(_silu(c_ref[...]), w_ref[...]) + b_ref[...]


def _modulation(cc, w_mod, b_mod):
    depth, d, n = w_mod.shape
    tn = 1536
    return pl.pallas_call(
        _mod_kernel,
        grid=(depth, n // tn),
        in_specs=[
            pl.BlockSpec((8, d), lambda i, k: (0, 0)),
            pl.BlockSpec((None, d, tn), lambda i, k: (i, 0, k)),
            pl.BlockSpec((None, 1, tn), lambda i, k: (i, 0, k)),
        ],
        out_specs=pl.BlockSpec((None, 8, tn), lambda i, k: (i, 0, k)),
        out_shape=jax.ShapeDtypeStruct((depth, 8, n), F32),
        compiler_params=_params(2),
    )(cc, w_mod, b_mod.reshape(depth, 1, n))


def _mod_spec(layer, chunk, batch, ctx_first):
    if ctx_first:
        return pl.BlockSpec((None, None, 1, D_MODEL), lambda b, j: (layer, jnp.where(j == 0, batch, b), 0, chunk))
    return pl.BlockSpec((None, None, 1, D_MODEL), lambda b, j: (layer, b, 0, chunk))


def _layer_spec(layer, shape):
    return pl.BlockSpec((None,) + shape, lambda b, j: (layer,) + (0,) * len(shape))


def _token_rows(src_refs):
    if len(src_refs) == 1:
        return src_refs[0][...]
    ctx_ref, x_ref = src_refs
    return jnp.where(pl.program_id(1) == 0, ctx_ref[...], x_ref[...])


def _token_specs(srcs, tm, off=0):
    d = srcs[0].shape[-1]
    if len(srcs) == 1:
        return [pl.BlockSpec((None, tm, d), lambda b, j: (b, j + off, 0))]
    assert off == 0
    return [pl.BlockSpec((None, tm, d), lambda b, j: (b, 0, 0)),
            pl.BlockSpec((None, tm, d), lambda b, j: (b, jnp.maximum(j - 1, 0), 0))]


def _in_proj_kernel(*refs, n_src):
    src_refs = refs[:n_src]
    sh_ref, sc_ref, nw_ref, w_ref, cos_ref, sin_ref, gla_ref, za_ref, na_ref, sq_ref, sk_ref, sv_ref = refs[n_src:]
    x = _token_rows(src_refs)
    tm = x.shape[0]
    lane = lax.broadcasted_iota(jnp.int32, (tm // 2, LANES), 1)
    first_half = _imod(lane, HEAD_DIM) < (HEAD_DIM // 2)
    n_q = SWA_W // LANES
    for rows in (slice(0, tm // 2), slice(tm // 2, tm)):
        h = _rms(x[rows]) * nw_ref[...]
        h = (h * (1.0 + sc_ref[...]) + sh_ref[...]).astype(BF16)
        qk = _dot(h, w_ref[:, C_SQ:C_SV])
        cos = cos_ref[rows, :]
        sin = sin_ref[rows, :]
        for cb in range(n_q + 1):
            a = qk[:, cb * LANES:(cb + 1) * LANES]
            partner = jnp.where(first_half, pltpu.roll(a, LANES - HEAD_DIM // 2, 1), pltpu.roll(a, HEAD_DIM // 2, 1))
            r = (a * cos + partner * sin).astype(BF16)
            if cb < n_q:
                sq_ref[rows, cb * LANES:(cb + 1) * LANES] = r
            else:
                sk_ref[rows, :] = r
        gla_ref[rows, :] = _dot(h, w_ref[:, C_GLA:C_ZA])
        za_ref[rows, :] = _dot(h, w_ref[:, C_ZA:C_NA])
        na_ref[rows, :] = _dot(h, w_ref[:, C_NA:C_SQ]).astype(BF16)
        sv_ref[rows, :] = _dot(h, w_ref[:, C_SV:IN_PERM_W]).astype(BF16)


def _in_proj(srcs, layer, mods, norm_w, w_perm, cos_t, sin_t):
    batch, d = srcs[0].shape[0], srcs[0].shape[2]
    s = sum(a.shape[1] for a in srcs)
    tm = ROW_BLOCK
    row = lambda w: pl.BlockSpec((None, tm, w), lambda b, j: (b, j, 0))
    tab = pl.BlockSpec((tm, LANES), lambda b, j: (j, 0))
    return pl.pallas_call(
        functools.partial(_in_proj_kernel, n_src=len(srcs)),
        grid=(batch, s // tm),
        in_specs=_token_specs(srcs, tm) + [
            _mod_spec(layer, 0, batch, True), _mod_spec(layer, 1, batch, True), _layer_spec(layer, (1, d)),
            _layer_spec(layer, (d, IN_PERM_W)), tab, tab],
        out_specs=[row(4 * GLA_W), row(LANES), row(3 * NA_W), row(SWA_W), row(SWA_KV_W), row(SWA_KV_W)],
        out_shape=[
            jax.ShapeDtypeStruct((batch, s, 4 * GLA_W), F32),
            jax.ShapeDtypeStruct((batch, s, LANES), F32),
            jax.ShapeDtypeStruct((batch, s, 3 * NA_W), BF16),
            jax.ShapeDtypeStruct((batch, s, SWA_W), BF16),
            jax.ShapeDtypeStruct((batch, s, SWA_KV_W), BF16),
            jax.ShapeDtypeStruct((batch, s, SWA_KV_W), BF16),
        ],
        compiler_params=_params(2),
    )(*srcs, mods, mods, norm_w, w_perm, cos_t, sin_t)


def _gla_stages(qf_ref, kf_ref, vf_ref, zf_ref, qb_ref, kb_ref, vb_ref, zb_ref,
                waf_ref, wab_ref, baf_ref, bab_ref, of_ref, ob_ref, stf_ref, stb_ref):
    c = GLA_CHUNK
    w = GLA_W
    n_chunks = ROW_BLOCK // c
    r64 = lax.broadcasted_iota(jnp.int32, (c, c), 0)
    c64 = lax.broadcasted_iota(jnp.int32, (c, c), 1)
    row_h = _idiv(lax.broadcasted_iota(jnp.int32, (GLA_HEADS * c, w), 0), c)
    col_h = _idiv(lax.broadcasted_iota(jnp.int32, (GLA_HEADS * c, w), 1), c)
    same_head = row_h == col_h
    ar = _imod(lax.broadcasted_iota(jnp.int32, (GLA_HEADS * c, c), 0), c)
    ac = lax.broadcasted_iota(jnp.int32, (GLA_HEADS * c, c), 1)
    out_head = _idiv(lax.broadcasted_iota(jnp.int32, (c, w), 1), c)
    dirs = [
        dict(q=qf_ref, k=kf_ref, v=vf_ref, z=zf_ref, wa=waf_ref, ba=baf_ref, o=of_ref, st=stf_ref, rev=False,
             order=range(n_chunks), cum=(r64 >= c64).astype(BF16), causal=ar >= ac),
        dict(q=qb_ref, k=kb_ref, v=vb_ref, z=zb_ref, wa=wab_ref, ba=bab_ref, o=ob_ref, st=stb_ref, rev=True,
             order=range(n_chunks - 1, -1, -1), cum=(r64 <= c64).astype(BF16), causal=ar <= ac),
    ]
    units = [(d, ci) for d in dirs for ci in d['order']]
    rows = lambda ci: slice(ci * c, (ci + 1) * c)
    t = {}

    def decay():
        log_a = []
        for d in dirs:
            z = _dot_split(d['z'][...], d['wa'][...]) + d['ba'][...]
            log_a.append((jnp.minimum(z, 0.0) - jnp.log1p(jnp.exp(-jnp.abs(z)))) * (1.0 / GLA_TAU))
        t['b'], t['b_tot'] = [], []
        for (d, ci), la in zip(units, [la for la in log_a for _ in range(n_chunks)]):
            hi, mid, lo = _split3(la[rows(ci)])
            t['b'].append(_dot(d['cum'], hi) + _dot(d['cum'], mid) + _dot(d['cum'], lo))
            t['b_tot'].append(t['b'][-1][0:1] if d['rev'] else t['b'][-1][c - 1:c])

    def scale():
        b, b_tot = t['b'], t['b_tot']
        k = [d['k'][rows(ci), :] for d, ci in units]
        t['v'] = [d['v'][rows(ci), :].astype(BF16) for d, ci in units]
        t['q_in'] = [(d['q'][rows(ci), :] * jnp.exp(bu)).astype(BF16) for (d, ci), bu in zip(units, b)]
        t['k_in'] = [(ku * jnp.exp(-bu)).astype(BF16) for ku, bu in zip(k, b)]
        t['k_end'] = [(ku * jnp.exp(bt - bu)).astype(BF16) for ku, bu, bt in zip(k, b, b_tot)]
        t['qx'] = [jnp.where(same_head, jnp.concatenate([qu] * GLA_HEADS, axis=0), jnp.zeros((), BF16))
                   for qu in t['q_in']]

    def intra():
        a = [jnp.where(d['causal'], _dot_nt(qxu, ku), 0.0).astype(BF16)
             for (d, _), qxu, ku in zip(units, t['qx'], t['k_in'])]
        t['r'] = [_dot(au, vu) for au, vu in zip(a, t['v'])]
        t['upd'] = [jnp.where(same_head, _dot_tn(vu, ku), 0.0) for vu, ku in zip(t['v'], t['k_end'])]

    def inter():
        st_enter = []
        for di, d in enumerate(dirs):
            st = d['st'][...]
            for u in range(di * n_chunks, (di + 1) * n_chunks):
                st_enter.append(st.astype(BF16))
                st = st * jnp.exp(t['b_tot'][u]) + t['upd'][u]
            d['st'][...] = st
        for (d, ci), qu, su, ru in zip(units, t['q_in'], st_enter, t['r']):
            o = _dot_nt(qu, su)
            for h in range(GLA_HEADS):
                o = o + jnp.where(out_head == h, ru[h * c:(h + 1) * c], 0.0)
            d['o'][rows(ci), :] = o

    return [decay, scale, intra, inter]


def _stack_pair(q):
    left = lax.broadcasted_iota(jnp.int32, q.shape, 1) < HEAD_DIM
    zero = jnp.zeros((), q.dtype)
    return jnp.concatenate([jnp.where(left, q, zero), jnp.where(left, zero, q)], axis=0)


def _unstack_pair(o):
    n = o.shape[0] // 2
    left = lax.broadcasted_iota(jnp.int32, (n, o.shape[1]), 1) < HEAD_DIM
    return jnp.where(left, o[:n], o[n:])


def _softmax_weights(scores, extra_logit=None):
    m = functools.reduce(jnp.maximum, [jnp.max(s, axis=-1, keepdims=True) for s in scores])
    if extra_logit is not None:
        m = jnp.maximum(m, extra_logit)
    ps = [jnp.exp2(s - m) for s in scores]
    denom = functools.reduce(jnp.add, [jnp.sum(p, axis=-1, keepdims=True) for p in ps])
    if extra_logit is not None:
        denom = denom + jnp.exp2(extra_logit - m)
    return [p.astype(BF16) for p in ps], denom


def _na_stages(q_ref, k_ref, v_ref, bias_ref, o_ref, j, ctx_len, grid_rows, is_ctx):
    n_pairs = NA_W // PAIR_W
    rows_per_block = ROW_BLOCK // GRID_W
    n_loc = NA_KH * GRID_W
    unit = 2 * GRID_W
    pair_cols = [slice(p * PAIR_W, (p + 1) * PAIR_W) for p in range(n_pairs)]
    units = [(rr, p) for p in range(n_pairs) for rr in range(rows_per_block)]
    ctx_rows = lambda rr: slice(rr * unit, (rr + 1) * unit)
    t = {}

    def scores():
        qx = {(rr, p): _stack_pair(q_ref[rr * GRID_W:(rr + 1) * GRID_W, pair_cols[p]]) for rr, p in units}
        t['s_ctx'] = [_dot_nt(jnp.concatenate([qx[rr, p] for rr in range(rows_per_block)], axis=0),
                              k_ref[0:ctx_len, pair_cols[p]]) for p in range(n_pairs)]
        if is_ctx:
            return
        t['k0'], bias = [], {}
        for rr in range(rows_per_block):
            r = (j - 1) * rows_per_block + rr
            first_key_row = jnp.clip(r - NA_KH // 2, 0, grid_rows - NA_KH)
            t['k0'].append(pl.multiple_of(ctx_len + first_key_row * GRID_W, GRID_W))
            d0 = first_key_row - r + NA_KH - 1
            for p in range(n_pairs):
                bias[rr, p] = jnp.concatenate(
                    [jnp.concatenate([bias_ref[2 * p + side, d0 + 2 * m] for m in range(NA_KH // 2)], axis=1)
                     for side in (0, 1)], axis=0)
        t['s_loc'] = {(rr, p): _dot_nt(qx[rr, p], k_ref[pl.ds(t['k0'][rr], n_loc), pair_cols[p]]) + bias[rr, p]
                      for rr, p in units}

    def softmax():
        t['w'] = {(rr, p): _softmax_weights(([] if is_ctx else [t['s_loc'][rr, p]]) + [t['s_ctx'][p][ctx_rows(rr)]])
                  for rr, p in units}

    def values():
        w = t['w']
        o_ctx = [_dot(jnp.concatenate([w[rr, p][0][-1] for rr in range(rows_per_block)], axis=0),
                      v_ref[0:ctx_len, pair_cols[p]]) for p in range(n_pairs)]
        for rr, p in units:
            o = o_ctx[p][ctx_rows(rr)]
            if not is_ctx:
                o = o + _dot(w[rr, p][0][0], v_ref[pl.ds(t['k0'][rr], n_loc), pair_cols[p]])
            o_ref[rr * GRID_W:(rr + 1) * GRID_W, pair_cols[p]] = _unstack_pair(o / w[rr, p][1]).astype(o_ref.dtype)

    return [scores, softmax, values]


def _swa_stages(sink, q_ref, k_ref, v_ref, o_ref, j, ctx_len, seq_len, is_ctx):
    tm = ROW_BLOCK
    n_pairs = SWA_W // PAIR_W
    n_loc = tm + 2 * SWA_WINDOW
    pair_cols = [slice(p * PAIR_W, (p + 1) * PAIR_W) for p in range(n_pairs)]
    lane = lax.broadcasted_iota(jnp.int32, (tm, PAIR_W), 1)
    zero = jnp.zeros((), BF16)
    units = [(p + side * n_pairs, p, side) for p in range(n_pairs) for side in (0, 1)]
    t = {}

    def scores():
        qx = [jnp.where((lane < HEAD_DIM) == (side == 0), q_ref[:, pair_cols[p]], zero) for _, p, side in units]
        k_ctx = k_ref[0:ctx_len, :]
        t['s'] = [[_dot_nt(q, k_ctx)] for q in qx]
        if is_ctx:
            return
        q0 = (j - 1) * tm
        k0 = jnp.clip(q0 - SWA_WINDOW, 0, seq_len - n_loc)
        qpos = q0 + lax.broadcasted_iota(jnp.int32, (tm, n_loc), 0)
        kpos = k0 + lax.broadcasted_iota(jnp.int32, (tm, n_loc), 1)
        in_window = jnp.abs(qpos - kpos) <= SWA_WINDOW
        t['start'] = pl.multiple_of(ctx_len + k0, SWA_WINDOW)
        k_loc = k_ref[pl.ds(t['start'], n_loc), :]
        for s, q in zip(t['s'], qx):
            s.insert(0, jnp.where(in_window, _dot_nt(q, k_loc), -jnp.inf))

    def softmax():
        t['w'] = [_softmax_weights(s, sink[h] * LOG2E) for s, (h, _, _) in zip(t['s'], units)]

    def values():
        vs = [v_ref[0:ctx_len, :]]
        if not is_ctx:
            vs.insert(0, v_ref[pl.ds(t['start'], n_loc), :])
        o = [functools.reduce(jnp.add, [_dot(p, v) for p, v in zip(ps, vs)]) / denom for ps, denom in t['w']]
        for p, cols in enumerate(pair_cols):
            o_ref[:, cols] = jnp.where(lane < HEAD_DIM, o[2 * p], o[2 * p + 1]).astype(o_ref.dtype)

    return [scores, softmax, values]


def _mixers_kernel(sinks_ref, qf_ref, kf_ref, vf_ref, zf_ref, qb_ref, kb_ref, vb_ref, zb_ref,
                   waf_ref, wab_ref, baf_ref, bab_ref, nq_ref, nk_ref, nv_ref, nbias_ref, sq_ref, sk_ref, sv_ref,
                   of_ref, ob_ref, yna_ref, yswa_ref, stf_ref, stb_ref, *, layer, ctx_len, seq_len, ctx_out):
    j = pl.program_id(1)

    @pl.when(j == 0)
    def _():
        stf_ref[...] = jnp.zeros_like(stf_ref)
        stb_ref[...] = jnp.zeros_like(stb_ref)

    sink = [sinks_ref[layer, h] for h in range(SWA_HEADS)]

    def block(is_ctx):
        gla = _gla_stages(qf_ref, kf_ref, vf_ref, zf_ref, qb_ref, kb_ref, vb_ref, zb_ref,
                          waf_ref, wab_ref, baf_ref, bab_ref, of_ref, ob_ref, stf_ref, stb_ref)
        if is_ctx and not ctx_out:
            order = gla
        else:
            na = _na_stages(nq_ref, nk_ref, nv_ref, nbias_ref, yna_ref, j, ctx_len, seq_len // GRID_W, is_ctx)
            swa = _swa_stages(sink, sq_ref, sk_ref, sv_ref, yswa_ref, j, ctx_len, seq_len, is_ctx)
            order = [gla[0], na[0], swa[0], gla[1], gla[2], na[1], na[2], swa[1], swa[2], gla[3]]
        for stage in order:
            stage()

    pl.when(j == 0)(functools.partial(block, True))
    pl.when(j > 0)(functools.partial(block, False))


def _mixers(gla_in, za, na_qkv, sq, sk, sv, layer, wa_f, wa_b, ba_f, ba_b, na_bias, sinks, ctx_len, ctx_out):
    batch, s, _ = gla_in.shape
    tm = ROW_BLOCK
    nblk = s // tm
    first = 0 if ctx_out else 1
    fwd = lambda j: j
    bwd = lambda j: jnp.where(j == 0, 0, nblk - j)
    soft_out = lambda j: jnp.maximum(j - first, 0)

    def col(order, cb):
        return pl.BlockSpec((None, tm, GLA_W), lambda b, j: (b, order(j), cb))

    def rows(order, w):
        return pl.BlockSpec((None, tm, w), lambda b, j: (b, order(j), 0))

    def full(w, cb=0):
        return pl.BlockSpec((None, s, w), lambda b, j: (b, 0, cb))

    return pl.pallas_call(
        functools.partial(_mixers_kernel, layer=layer, ctx_len=ctx_len, seq_len=s - ctx_len, ctx_out=ctx_out),
        grid=(batch, nblk),
        in_specs=[pl.BlockSpec(memory_space=pltpu.SMEM),
                  col(fwd, 0), col(fwd, 1), col(fwd, 2), rows(fwd, LANES),
                  col(bwd, 0), col(bwd, 1), col(bwd, 2), rows(bwd, LANES),
                  _layer_spec(layer, (LANES, GLA_W)), _layer_spec(layer, (LANES, GLA_W)),
                  _layer_spec(layer, (1, GLA_W)), _layer_spec(layer, (1, GLA_W)),
                  rows(fwd, NA_W), full(NA_W, 1), full(NA_W, 2), _layer_spec(layer, na_bias.shape[1:]),
                  rows(fwd, SWA_W), full(SWA_KV_W), full(SWA_KV_W)],
        out_specs=[col(fwd, 0), col(bwd, 0), rows(soft_out, NA_W), rows(soft_out, SWA_W)],
        out_shape=[jax.ShapeDtypeStruct((batch, s, GLA_W), F32)] * 2
        + [jax.ShapeDtypeStruct((batch, s - first * tm, w), BF16) for w in (NA_W, SWA_W)],
        scratch_shapes=[pltpu.VMEM((GLA_W, GLA_W), F32)] * 2,
        compiler_params=_params(1, 1),
    )(sinks, gla_in, gla_in, gla_in, za, gla_in, gla_in, gla_in, za, wa_f, wa_b, ba_f, ba_b,
      na_qkv, na_qkv, na_qkv, na_bias, sq, sk, sv)


def _out_ffn_kernel(*refs, n_src, final):
    src_refs = refs[:n_src]
    (of_ref, ob_ref, gate_ref, na_ref, swa_ref, g1_ref, sh2_ref, sc2_ref, g2_ref, nffn_ref, gn_ref, avg_ref,
     wg_ref, wn_ref, ws_ref, wgate_ref, wup_ref, wdown_ref, fn_ref, o_ref) = refs[n_src:]
    x_in = _token_rows(src_refs)
    tm = x_in.shape[0]
    avg = avg_ref[...]
    halves = (slice(0, tm // 2), slice(tm // 2, tm))
    o = [of_ref[rows, :] + ob_ref[rows, :] for rows in halves]
    sq = [_split2(oh * oh) for oh in o]
    ms = [_dot(hi, avg) + _dot(lo, avg) for hi, lo in sq]
    y = [(oh * lax.rsqrt(msh + NORM_EPS) * gn_ref[...] * _silu(gate_ref[rows, :])).astype(BF16)
         for oh, msh, rows in zip(o, ms, halves)]
    mix = [_dot(yh, wg_ref[...]) + _dot(na_ref[rows, :], wn_ref[...]) + _dot(swa_ref[rows, :], ws_ref[...])
           for yh, rows in zip(y, halves)]
    x1 = jnp.concatenate([x_in[rows] + g1_ref[...] * mh for mh, rows in zip(mix, halves)], axis=0)
    h = _rms(x1) * nffn_ref[...]
    h = (h * (1.0 + sc2_ref[...]) + sh2_ref[...]).astype(BF16)

    d_ff = wgate_ref.shape[1]
    chunks = [slice(c0, min(c0 + FF_CHUNK, d_ff)) for c0 in range(0, d_ff, FF_CHUNK)]
    acc = jnp.zeros(x1.shape, F32)
    gu = {}
    for t in range(len(chunks) + 1):
        if t < len(chunks):
            gu[t] = (_dot(h, wgate_ref[:, chunks[t]]), _dot(h, wup_ref[:, chunks[t]]))
        if t >= 1:
            g, u = gu.pop(t - 1)
            acc = acc + _dot((_silu(g) * u).astype(BF16), wdown_ref[chunks[t - 1], :])
    x2 = x1 + g2_ref[...] * acc
    if final:
        x2 = _rms(x2) * fn_ref[...]
    o_ref[...] = x2


def _out_ffn(srcs, o_f, o_b, gla_in, y_na, y_swa, layer, mods, norm_ffn, gla_norm, avg, w_g, w_n, w_s,
             w_gate, w_up, w_down, final_norm, skip_ctx, final):
    batch, s, _ = o_f.shape
    d = srcs[0].shape[2]
    d_ff = w_gate.shape[2]
    tm = ROW_BLOCK
    off = 1 if skip_ctx else 0
    nblk = s // tm - off
    tok_off = off
    if skip_ctx and len(srcs) == 2:
        srcs, tok_off = srcs[1:], 0
    row = lambda w, cb=0: pl.BlockSpec((None, tm, w), lambda b, j: (b, j + off, cb))
    out = lambda w: pl.BlockSpec((None, tm, w), lambda b, j: (b, j, 0))
    ms = lambda chunk: _mod_spec(layer, chunk, batch, not skip_ctx)
    return pl.pallas_call(
        functools.partial(_out_ffn_kernel, n_src=len(srcs), final=final),
        grid=(batch, nblk),
        in_specs=_token_specs(srcs, tm, tok_off) + [
            row(GLA_W), row(GLA_W), row(GLA_W, 3), out(NA_W), out(SWA_W),
            ms(2), ms(3), ms(4), ms(5), _layer_spec(layer, (1, d)), _layer_spec(layer, (1, GLA_W)),
            pl.BlockSpec((GLA_W, GLA_W), lambda b, j: (0, 0)),
            _layer_spec(layer, (GLA_W, d)), _layer_spec(layer, (NA_W, d)), _layer_spec(layer, (SWA_W, d)),
            _layer_spec(layer, (d, d_ff)), _layer_spec(layer, (d, d_ff)), _layer_spec(layer, (d_ff, d)),
            pl.BlockSpec((1, d), lambda b, j: (0, 0))],
        out_specs=out(d),
        out_shape=jax.ShapeDtypeStruct((batch, nblk * tm, d), F32),
        compiler_params=_params(2),
    )(*srcs, o_f, o_b, gla_in, y_na, y_swa, mods, mods, mods, mods, norm_ffn, gla_norm, avg, w_g, w_n, w_s,
      w_gate, w_up, w_down, final_norm)


def _rope_tables(ctx_len, seq_len):
    t = np.arange(seq_len)
    n_freq = HEAD_DIM // 4
    inv_freq = ROPE_THETA ** (-np.arange(n_freq) / n_freq)
    ang = np.concatenate([(t // GRID_W)[:, None] * inv_freq, (t % GRID_W)[:, None] * inv_freq], axis=-1)
    cos_h = np.concatenate([np.cos(ang), np.cos(ang)], axis=-1)
    sin_h = np.concatenate([-np.sin(ang), np.sin(ang)], axis=-1)
    cos_t = np.concatenate([np.ones((ctx_len, HEAD_DIM)), cos_h], axis=0)
    sin_t = np.concatenate([np.zeros((ctx_len, HEAD_DIM)), sin_h], axis=0)
    return jnp.asarray(np.tile(cos_t, (1, 2)), F32), jnp.asarray(np.tile(sin_t, (1, 2)), F32)


def _swa_pair_order(a, axis):
    n_pairs = SWA_W // PAIR_W
    heads = [h for p in range(n_pairs) for h in (p, p + n_pairs)]
    return jnp.concatenate([lax.slice_in_dim(a, h * HEAD_DIM, (h + 1) * HEAD_DIM, axis=axis) for h in heads],
                           axis=axis)


def _permute_w_in_kernel(w_ref, o_ref):
    scale = HEAD_DIM ** -0.5
    scale2 = scale * LOG2E
    o = np.cumsum([0, GLA_W, GLA_W, GLA_W, GLA_W, GLA_RANK, GLA_RANK, NA_W, NA_W, NA_W, SWA_W, SWA_KV_W, SWA_KV_W])
    n_pairs = SWA_W // PAIR_W
    swa_heads = [h for p in range(n_pairs) for h in (p, p + n_pairs)]
    moves = [(o[0], o[1], C_GLA, scale), (o[1], o[4], C_GLA + GLA_W, 1.0), (o[4], o[6], C_ZA, 1.0),
             (o[6], o[7], C_NA, scale2), (o[7], o[9], C_NA + NA_W, 1.0), (o[10], o[12], C_SK, 1.0)]
    moves += [(o[9] + h * HEAD_DIM, o[9] + (h + 1) * HEAD_DIM, C_SQ + i * HEAD_DIM, scale2)
              for i, h in enumerate(swa_heads)]
    o_ref[:, C_ZA:C_NA] = jnp.zeros((o_ref.shape[0], LANES), o_ref.dtype)
    for lo, hi, dst, mult in moves:
        o_ref[:, dst:dst + (hi - lo)] = (w_ref[:, lo:hi] * mult).astype(o_ref.dtype)


def _permute_w_in(w):
    depth, d, n = w.shape
    tm = ROW_BLOCK
    return pl.pallas_call(
        _permute_w_in_kernel,
        grid=(depth, d // tm),
        in_specs=[pl.BlockSpec((None, tm, n), lambda i, k: (i, k, 0))],
        out_specs=pl.BlockSpec((None, tm, IN_PERM_W), lambda i, k: (i, k, 0)),
        out_shape=jax.ShapeDtypeStruct((depth, d, IN_PERM_W), BF16),
        compiler_params=_params(2),
    )(w)


def _na_bias_table(rpb):
    qc = np.arange(GRID_W)
    kc = np.arange(GRID_W)
    win = np.clip(qc - NA_KW // 2, 0, GRID_W - NA_KW)
    valid = (kc[None, :] >= win[:, None]) & (kc[None, :] < win[:, None] + NA_KW)
    dc = kc[None, :] - qc[:, None] + NA_KW - 1
    pick_col = ((dc[None] == np.arange(2 * NA_KW - 1)[:, None, None]) & valid[None]).astype(np.float32)
    t = jnp.einsum('lhde,eck->lhdck', rpb.astype(F32) * LOG2E, pick_col, precision=lax.Precision.HIGHEST)
    t = jnp.where(valid, t, -jnp.inf)
    return jnp.concatenate([t[:, :, :-1], t[:, :, 1:]], axis=-1)


def _pad_rank(wa2, offset):
    return jnp.pad(wa2, ((0, 0), (offset, LANES - offset - GLA_RANK), (0, 0)))


def kernel(x, c, ctx, c_ctx, w_mod, b_mod, norm_mix, norm_ffn, w_in, gla_wa2_f, gla_ba_f, gla_wa2_b, gla_ba_b,
           gla_norm, na_rpb, swa_sink, w_out, w_gate, w_up, w_down, final_norm):
    batch, seq_len, d = x.shape
    ctx_len = ctx.shape[1]
    depth = w_mod.shape[0]
    assert d == D_MODEL and ctx_len == ROW_BLOCK and seq_len % ROW_BLOCK == 0 and batch < 8
    assert seq_len % GRID_W == 0 and seq_len // GRID_W >= 2 * NA_KH

    cc = jnp.concatenate([c, c_ctx[None, :], jnp.zeros((8 - batch - 1, d), F32)], axis=0)
    mods = _modulation(cc, w_mod, b_mod).reshape(depth, 8, 1, 6 * d)
    cos_t, sin_t = _rope_tables(ctx_len, seq_len)
    head_avg = jnp.asarray(np.kron(np.eye(GLA_HEADS), np.full((HEAD_DIM, HEAD_DIM), 1.0 / HEAD_DIM)), BF16)
    w_in_p = _permute_w_in(w_in)
    wa_f, wa_b = _pad_rank(gla_wa2_f, 0), _pad_rank(gla_wa2_b, GLA_RANK)
    ba_f, ba_b = gla_ba_f.reshape(depth, 1, GLA_W), gla_ba_b.reshape(depth, 1, GLA_W)
    na_bias = _na_bias_table(na_rpb)
    w_o_gla = w_out[:, :GLA_W].astype(BF16)
    w_o_na = w_out[:, GLA_W:GLA_W + NA_W].astype(BF16)
    w_o_swa = _swa_pair_order(w_out[:, GLA_W + NA_W:], 1).astype(BF16)
    w_gate_b, w_up_b, w_down_b = w_gate.astype(BF16), w_up.astype(BF16), w_down.astype(BF16)
    norm_mix3, norm_ffn3 = norm_mix.reshape(depth, 1, d), norm_ffn.reshape(depth, 1, d)
    gla_norm3 = gla_norm.reshape(depth, 1, GLA_W)

    srcs = (ctx, x)
    for i in range(depth):
        last = i == depth - 1
        gla_in, za, na_qkv, sq, sk, sv = _in_proj(srcs, i, mods, norm_mix3, w_in_p, cos_t, sin_t)
        o_f, o_b, y_na, y_swa = _mixers(gla_in, za, na_qkv, sq, sk, sv, i, wa_f, wa_b, ba_f, ba_b, na_bias,
                                        swa_sink, ctx_len, ctx_out=not last)
        srcs = (_out_ffn(srcs, o_f, o_b, gla_in, y_na, y_swa, i, mods, norm_ffn3, gla_norm3, head_avg,
                         w_o_gla, w_o_na, w_o_swa, w_gate_b, w_up_b, w_down_b, final_norm.reshape(1, d),
                         skip_ctx=last, final=last),)
    return srcs[0]
```

```python
import functools

import jax
import jax.numpy as jnp
import numpy as np
from jax import lax
from jax.experimental import pallas as pl
from jax.experimental.pallas import tpu as pltpu

F32 = jnp.float32
BF16 = jnp.bfloat16

D_MODEL = 1024
HEAD_DIM = 64
GRID_W = 64
GLA_HEADS = 4
NA_HEADS = 6
SWA_HEADS = 6
SWA_KV_HEADS = 2
GLA_W = GLA_HEADS * HEAD_DIM
NA_W = NA_HEADS * HEAD_DIM
SWA_W = SWA_HEADS * HEAD_DIM
SWA_KV_W = SWA_KV_HEADS * HEAD_DIM
GLA_RANK = 16
GLA_TAU = 16.0
GLA_CHUNK = 64
NA_KH = 8
NA_KW = 16
SWA_WINDOW = 128
ROPE_THETA = 10000.0
NORM_EPS = 1e-6

LANES = 128
ROW_BLOCK = 256
PAIR_W = 2 * HEAD_DIM
LOG2E = float(np.log2(np.e))
FF_CHUNK = 256
VMEM_LIMIT = 56 * 1024 * 1024

C_GLA = 0
C_NA = C_GLA + 4 * GLA_W
C_SV = C_NA + 3 * NA_W
C_SQ = C_SV + SWA_KV_W
C_SK = C_SQ + SWA_W
C_ZA = C_SK + SWA_KV_W
IN_PERM_W = C_ZA + LANES


def _dot(a, b):
    return jnp.dot(a, b, preferred_element_type=F32)


def _dot_nt(a, b):
    return lax.dot_general(a, b, (((1,), (1,)), ((), ())), preferred_element_type=F32)


def _dot_tn(a, b):
    return lax.dot_general(a, b, (((0,), (0,)), ((), ())), preferred_element_type=F32)


def _idiv(a, n):
    assert n & (n - 1) == 0
    return a >> (n.bit_length() - 1)


def _imod(a, n):
    assert n & (n - 1) == 0
    return a & (n - 1)


def _split2(a):
    hi = a.astype(BF16)
    lo = (a - hi.astype(F32)).astype(BF16)
    return hi, lo


def _split3(a):
    hi = a.astype(BF16)
    r = a - hi.astype(F32)
    mid = r.astype(BF16)
    lo = (r - mid.astype(F32)).astype(BF16)
    return hi, mid, lo


def _dot_split(a, b):
    ah, al = _split2(a)
    bh, bl = _split2(b)
    return _dot(ah, bh) + _dot(al, bh) + _dot(ah, bl)


def _silu(a):
    return a * jax.nn.sigmoid(a)


def _rms(x):
    return x * lax.rsqrt(jnp.mean(x * x, axis=-1, keepdims=True) + NORM_EPS)


def _params(n_parallel, n_arbitrary=0):
    return pltpu.CompilerParams(
        dimension_semantics=("parallel",) * n_parallel + ("arbitrary",) * n_arbitrary,
        vmem_limit_bytes=VMEM_LIMIT)


def _mod_kernel(c_ref, w_ref, b_ref, o_ref):
    o_ref[...] = _dot_split(_silu(c_ref[...]), w_ref[...]) + b_ref[...]


def _modulation(cc, w_mod, b_mod):
    depth, d, n = w_mod.shape
    tn = 1536
    return pl.pallas_call(
        _mod_kernel,
        grid=(depth, n // tn),
        in_specs=[
            pl.BlockSpec((8, d), lambda i, k: (0, 0)),
            pl.BlockSpec((None, d, tn), lambda i, k: (i, 0, k)),
            pl.BlockSpec((None, 1, tn), lambda i, k: (i, 0, k)),
        ],
        out_specs=pl.BlockSpec((None, 8, tn), lambda i, k: (i, 0, k)),
        out_shape=jax.ShapeDtypeStruct((depth, 8, n), F32),
        compiler_params=_params(2),
    )(cc, w_mod, b_mod.reshape(depth, 1, n))


def _mod_spec(layer, chunk, batch, ctx_first):
    if ctx_first:
        return pl.BlockSpec((None, None, 1, D_MODEL), lambda b, j: (layer, jnp.where(j == 0, batch, b), 0, chunk))
    return pl.BlockSpec((None, None, 1, D_MODEL), lambda b, j: (layer, b, 0, chunk))


def _layer_spec(layer, shape):
    return pl.BlockSpec((None,) + shape, lambda b, j: (layer,) + (0,) * len(shape))


def _token_rows(src_refs):
    if len(src_refs) == 1:
        return src_refs[0][...]
    ctx_ref, x_ref = src_refs
    return jnp.where(pl.program_id(1) == 0, ctx_ref[...], x_ref[...])


def _token_specs(srcs, tm, off=0):
    d = srcs[0].shape[-1]
    if len(srcs) == 1:
        return [pl.BlockSpec((None, tm, d), lambda b, j: (b, j + off, 0))]
    assert off == 0
    return [pl.BlockSpec((None, tm, d), lambda b, j: (b, 0, 0)),
            pl.BlockSpec((None, tm, d), lambda b, j: (b, jnp.maximum(j - 1, 0), 0))]


def _in_proj_kernel(*refs, n_src):
    src_refs = refs[:n_src]
    sh_ref, sc_ref, nw_ref, w_ref, cos_ref, sin_ref, gla_ref, za_ref, na_ref, sq_ref, sk_ref, sv_ref = refs[n_src:]
    x = _token_rows(src_refs)
    tm = x.shape[0]
    lane = lax.broadcasted_iota(jnp.int32, (tm // 2, LANES), 1)
    first_half = _imod(lane, HEAD_DIM) < (HEAD_DIM // 2)
    n_q = SWA_W // LANES
    for rows in (slice(0, tm // 2), slice(tm // 2, tm)):
        h = _rms(x[rows]) * nw_ref[...]
        h = (h * (1.0 + sc_ref[...]) + sh_ref[...]).astype(BF16)
        qkz = _dot(h, w_ref[:, C_SQ:IN_PERM_W])
        qk = qkz[:, :C_ZA - C_SQ]
        cos = cos_ref[rows, :]
        sin = sin_ref[rows, :]
        for cb in range(n_q + 1):
            a = qk[:, cb * LANES:(cb + 1) * LANES]
            partner = jnp.where(first_half, pltpu.roll(a, LANES - HEAD_DIM // 2, 1), pltpu.roll(a, HEAD_DIM // 2, 1))
            r = (a * cos + partner * sin).astype(BF16)
            if cb < n_q:
                sq_ref[rows, cb * LANES:(cb + 1) * LANES] = r
            else:
                sk_ref[rows, :] = r
        za_ref[rows, :] = qkz[:, C_ZA - C_SQ:]
        gla_ref[rows, :] = _dot(h, w_ref[:, C_GLA:C_NA])
        nv = _dot(h, w_ref[:, C_NA:C_SQ]).astype(BF16)
        na_ref[rows, :] = nv[:, :C_SV - C_NA]
        sv_ref[rows, :] = nv[:, C_SV - C_NA:]


def _in_proj(srcs, layer, mods, norm_w, w_perm, cos_t, sin_t):
    batch, d = srcs[0].shape[0], srcs[0].shape[2]
    s = sum(a.shape[1] for a in srcs)
    tm = ROW_BLOCK
    row = lambda w: pl.BlockSpec((None, tm, w), lambda b, j: (b, j, 0))
    tab = pl.BlockSpec((tm, LANES), lambda b, j: (j, 0))
    return pl.pallas_call(
        functools.partial(_in_proj_kernel, n_src=len(srcs)),
        grid=(batch, s // tm),
        in_specs=_token_specs(srcs, tm) + [
            _mod_spec(layer, 0, batch, True), _mod_spec(layer, 1, batch, True), _layer_spec(layer, (1, d)),
            _layer_spec(layer, (d, IN_PERM_W)), tab, tab],
        out_specs=[row(4 * GLA_W), row(LANES), row(3 * NA_W), row(SWA_W), row(SWA_KV_W), row(SWA_KV_W)],
        out_shape=[
            jax.ShapeDtypeStruct((batch, s, 4 * GLA_W), F32),
            jax.ShapeDtypeStruct((batch, s, LANES), F32),
            jax.ShapeDtypeStruct((batch, s, 3 * NA_W), BF16),
            jax.ShapeDtypeStruct((batch, s, SWA_W), BF16),
            jax.ShapeDtypeStruct((batch, s, SWA_KV_W), BF16),
            jax.ShapeDtypeStruct((batch, s, SWA_KV_W), BF16),
        ],
        compiler_params=_params(2),
    )(*srcs, mods, mods, norm_w, w_perm, cos_t, sin_t)


def _gla_stages(qf_ref, kf_ref, vf_ref, zf_ref, qb_ref, kb_ref, vb_ref, zb_ref,
                waf_ref, wab_ref, baf_ref, bab_ref, of_ref, ob_ref, stf_ref, stb_ref):
    c = GLA_CHUNK
    w = GLA_W
    n_chunks = ROW_BLOCK // c
    r64 = lax.broadcasted_iota(jnp.int32, (c, c), 0)
    c64 = lax.broadcasted_iota(jnp.int32, (c, c), 1)
    row_h = _idiv(lax.broadcasted_iota(jnp.int32, (GLA_HEADS * c, w), 0), c)
    col_h = _idiv(lax.broadcasted_iota(jnp.int32, (GLA_HEADS * c, w), 1), c)
    same_head = row_h == col_h
    ar = _imod(lax.broadcasted_iota(jnp.int32, (GLA_HEADS * c, c), 0), c)
    ac = lax.broadcasted_iota(jnp.int32, (GLA_HEADS * c, c), 1)
    out_head = _idiv(lax.broadcasted_iota(jnp.int32, (c, w), 1), c)
    dirs = [
        dict(q=qf_ref, k=kf_ref, v=vf_ref, z=zf_ref, wa=waf_ref, ba=baf_ref, o=of_ref, st=stf_ref, rev=False,
             order=range(n_chunks), cum=(r64 >= c64).astype(BF16), causal=ar >= ac),
        dict(q=qb_ref, k=kb_ref, v=vb_ref, z=zb_ref, wa=wab_ref, ba=bab_ref, o=ob_ref, st=stb_ref, rev=True,
             order=range(n_chunks - 1, -1, -1), cum=(r64 <= c64).astype(BF16), causal=ar <= ac),
    ]
    units = [(d, ci) for d in dirs for ci in d['order']]
    rows = lambda ci: slice(ci * c, (ci + 1) * c)
    t = {}

    def decay():
        log_a = []
        for d in dirs:
            z = _dot_split(d['z'][...], d['wa'][...]) + d['ba'][...]
            log_a.append((jnp.minimum(z, 0.0) - jnp.log1p(jnp.exp(-jnp.abs(z)))) * (1.0 / GLA_TAU))
        t['b'], t['b_tot'] = [], []
        for (d, ci), la in zip(units, [la for la in log_a for _ in range(n_chunks)]):
            hi, mid, lo = _split3(la[rows(ci)])
            t['b'].append(_dot(d['cum'], hi) + _dot(d['cum'], mid) + _dot(d['cum'], lo))
            t['b_tot'].append(t['b'][-1][0:1] if d['rev'] else t['b'][-1][c - 1:c])

    def scale():
        b, b_tot = t['b'], t['b_tot']
        k = [d['k'][rows(ci), :] for d, ci in units]
        t['v'] = [d['v'][rows(ci), :].astype(BF16) for d, ci in units]
        t['q_in'] = [(d['q'][rows(ci), :] * jnp.exp(bu)).astype(BF16) for (d, ci), bu in zip(units, b)]
        t['k_in'] = [(ku * jnp.exp(-bu)).astype(BF16) for ku, bu in zip(k, b)]
        t['k_end'] = [(ku * jnp.exp(bt - bu)).astype(BF16) for ku, bu, bt in zip(k, b, b_tot)]
        t['qx'] = [jnp.where(same_head, jnp.concatenate([qu] * GLA_HEADS, axis=0), jnp.zeros((), BF16))
                   for qu in t['q_in']]

    def intra():
        a = [jnp.where(d['causal'], _dot_nt(qxu, ku), 0.0).astype(BF16)
             for (d, _), qxu, ku in zip(units, t['qx'], t['k_in'])]
        t['r'] = [_dot(au, vu) for au, vu in zip(a, t['v'])]
        t['upd'] = [jnp.where(same_head, _dot_tn(vu, ku), 0.0) for vu, ku in zip(t['v'], t['k_end'])]

    def inter():
        st_enter = []
        for di, d in enumerate(dirs):
            st = d['st'][...]
            for u in range(di * n_chunks, (di + 1) * n_chunks):
                st_enter.append(st.astype(BF16))
                st = st * jnp.exp(t['b_tot'][u]) + t['upd'][u]
            d['st'][...] = st
        for (d, ci), qu, su, ru in zip(units, t['q_in'], st_enter, t['r']):
            o = _dot_nt(qu, su)
            for h in range(GLA_HEADS):
                o = o + jnp.where(out_head == h, ru[h * c:(h + 1) * c], 0.0)
            d['o'][rows(ci), :] = o

    return [decay, scale, intra, inter]


def _stack_pair(q):
    left = lax.broadcasted_iota(jnp.int32, q.shape, 1) < HEAD_DIM
    zero = jnp.zeros((), q.dtype)
    return jnp.concatenate([jnp.where(left, q, zero), jnp.where(left, zero, q)], axis=0)


def _unstack_pair(o):
    n = o.shape[0] // 2
    left = lax.broadcasted_iota(jnp.int32, (n, o.shape[1]), 1) < HEAD_DIM
    return jnp.where(left, o[:n], o[n:])


def _softmax_weights(scores, extra_logit=None):
    m = functools.reduce(jnp.maximum, [jnp.max(s, axis=-1, keepdims=True) for s in scores])
    if extra_logit is not None:
        m = jnp.maximum(m, extra_logit)
    ps = [jnp.exp2(s - m) for s in scores]
    denom = functools.reduce(jnp.add, [jnp.sum(p, axis=-1, keepdims=True) for p in ps])
    if extra_logit is not None:
        denom = denom + jnp.exp2(extra_logit - m)
    return [p.astype(BF16) for p in ps], denom


def _na_stages(q_ref, k_ref, v_ref, bias_ref, o_ref, j, ctx_len, grid_rows, is_ctx):
    n_pairs = NA_W // PAIR_W
    rows_per_block = ROW_BLOCK // GRID_W
    n_loc = NA_KH * GRID_W
    unit = 2 * GRID_W
    pair_cols = [slice(p * PAIR_W, (p + 1) * PAIR_W) for p in range(n_pairs)]
    units = [(rr, p) for p in range(n_pairs) for rr in range(rows_per_block)]
    ctx_rows = lambda rr: slice(rr * unit, (rr + 1) * unit)
    t = {}

    def scores():
        qx = {(rr, p): _stack_pair(q_ref[rr * GRID_W:(rr + 1) * GRID_W, pair_cols[p]]) for rr, p in units}
        t['s_ctx'] = [_dot_nt(jnp.concatenate([qx[rr, p] for rr in range(rows_per_block)], axis=0),
                              k_ref[0:ctx_len, pair_cols[p]]) for p in range(n_pairs)]
        if is_ctx:
            return
        t['k0'], bias = [], {}
        for rr in range(rows_per_block):
            r = (j - 1) * rows_per_block + rr
            first_key_row = jnp.clip(r - NA_KH // 2, 0, grid_rows - NA_KH)
            t['k0'].append(pl.multiple_of(ctx_len + first_key_row * GRID_W, GRID_W))
            d0 = first_key_row - r + NA_KH - 1
            for p in range(n_pairs):
                bias[rr, p] = jnp.concatenate(
                    [jnp.concatenate([bias_ref[2 * p + side, d0 + 2 * m] for m in range(NA_KH // 2)], axis=1)
                     for side in (0, 1)], axis=0)
        t['s_loc'] = {(rr, p): _dot_nt(qx[rr, p], k_ref[pl.ds(t['k0'][rr], n_loc), pair_cols[p]]) + bias[rr, p]
                      for rr, p in units}

    def softmax():
        t['w'] = {(rr, p): _softmax_weights(([] if is_ctx else [t['s_loc'][rr, p]]) + [t['s_ctx'][p][ctx_rows(rr)]])
                  for rr, p in units}

    def values():
        w = t['w']
        o_ctx = [_dot(jnp.concatenate([w[rr, p][0][-1] for rr in range(rows_per_block)], axis=0),
                      v_ref[0:ctx_len, pair_cols[p]]) for p in range(n_pairs)]
        for rr, p in units:
            o = o_ctx[p][ctx_rows(rr)]
            if not is_ctx:
                o = o + _dot(w[rr, p][0][0], v_ref[pl.ds(t['k0'][rr], n_loc), pair_cols[p]])
            o_ref[rr * GRID_W:(rr + 1) * GRID_W, pair_cols[p]] = _unstack_pair(o / w[rr, p][1]).astype(o_ref.dtype)

    return [scores, softmax, values]


def _swa_stages(sink, q_ref, k_ref, v_ref, o_ref, j, ctx_len, seq_len, is_ctx):
    tm = ROW_BLOCK
    n_pairs = SWA_W // PAIR_W
    n_loc = tm + 2 * SWA_WINDOW
    pair_cols = [slice(p * PAIR_W, (p + 1) * PAIR_W) for p in range(n_pairs)]
    lane = lax.broadcasted_iota(jnp.int32, (tm, PAIR_W), 1)
    zero = jnp.zeros((), BF16)
    units = [(p + side * n_pairs, p, side) for p in range(n_pairs) for side in (0, 1)]
    t = {}

    def scores():
        qx = [jnp.where((lane < HEAD_DIM) == (side == 0), q_ref[:, pair_cols[p]], zero) for _, p, side in units]
        k_ctx = k_ref[0:ctx_len, :]
        t['s'] = [[_dot_nt(q, k_ctx)] for q in qx]
        if is_ctx:
            return
        q0 = (j - 1) * tm
        k0 = jnp.clip(q0 - SWA_WINDOW, 0, seq_len - n_loc)
        qpos = q0 + lax.broadcasted_iota(jnp.int32, (tm, n_loc), 0)
        kpos = k0 + lax.broadcasted_iota(jnp.int32, (tm, n_loc), 1)
        in_window = jnp.abs(qpos - kpos) <= SWA_WINDOW
        t['start'] = pl.multiple_of(ctx_len + k0, SWA_WINDOW)
        k_loc = k_ref[pl.ds(t['start'], n_loc), :]
        for s, q in zip(t['s'], qx):
            s.insert(0, jnp.where(in_window, _dot_nt(q, k_loc), -jnp.inf))

    def softmax():
        t['w'] = [_softmax_weights(s, sink[h] * LOG2E) for s, (h, _, _) in zip(t['s'], units)]

    def values():
        vs = [v_ref[0:ctx_len, :]]
        if not is_ctx:
            vs.insert(0, v_ref[pl.ds(t['start'], n_loc), :])
        o = [functools.reduce(jnp.add, [_dot(p, v) for p, v in zip(ps, vs)]) / denom for ps, denom in t['w']]
        for p, cols in enumerate(pair_cols):
            o_ref[:, cols] = jnp.where(lane < HEAD_DIM, o[2 * p], o[2 * p + 1]).astype(o_ref.dtype)

    return [scores, softmax, values]


def _mixers_kernel(sinks_ref, qf_ref, kf_ref, vf_ref, zf_ref, qb_ref, kb_ref, vb_ref, zb_ref,
                   waf_ref, wab_ref, baf_ref, bab_ref, nq_ref, nk_ref, nv_ref, nbias_ref, sq_ref, sk_ref, sv_ref,
                   of_ref, ob_ref, yna_ref, yswa_ref, stf_ref, stb_ref, *, layer, ctx_len, seq_len, ctx_out):
    j = pl.program_id(1)

    @pl.when(j == 0)
    def _():
        stf_ref[...] = jnp.zeros_like(stf_ref)
        stb_ref[...] = jnp.zeros_like(stb_ref)

    sink = [sinks_ref[layer, h] for h in range(SWA_HEADS)]

    def block(is_ctx):
        gla = _gla_stages(qf_ref, kf_ref, vf_ref, zf_ref, qb_ref, kb_ref, vb_ref, zb_ref,
                          waf_ref, wab_ref, baf_ref, bab_ref, of_ref, ob_ref, stf_ref, stb_ref)
        if is_ctx and not ctx_out:
            order = gla
        else:
            na = _na_stages(nq_ref, nk_ref, nv_ref, nbias_ref, yna_ref, j, ctx_len, seq_len // GRID_W, is_ctx)
            swa = _swa_stages(sink, sq_ref, sk_ref, sv_ref, yswa_ref, j, ctx_len, seq_len, is_ctx)
            order = [gla[0], na[0], swa[0], gla[1], gla[2], na[1], na[2], swa[1], swa[2], gla[3]]
        for stage in order:
            stage()

    pl.when(j == 0)(functools.partial(block, True))
    pl.when(j > 0)(functools.partial(block, False))


def _mixers(gla_in, za, na_qkv, sq, sk, sv, layer, wa_f, wa_b, ba_f, ba_b, na_bias, sinks, ctx_len, ctx_out):
    batch, s, _ = gla_in.shape
    tm = ROW_BLOCK
    nblk = s // tm
    first = 0 if ctx_out else 1
    fwd = lambda j: j
    bwd = lambda j: jnp.where(j == 0, 0, nblk - j)
    soft_out = lambda j: jnp.maximum(j - first, 0)

    def col(order, cb):
        return pl.BlockSpec((None, tm, GLA_W), lambda b, j: (b, order(j), cb))

    def rows(order, w):
        return pl.BlockSpec((None, tm, w), lambda b, j: (b, order(j), 0))

    def full(w, cb=0):
        return pl.BlockSpec((None, s, w), lambda b, j: (b, 0, cb))

    return pl.pallas_call(
        functools.partial(_mixers_kernel, layer=layer, ctx_len=ctx_len, seq_len=s - ctx_len, ctx_out=ctx_out),
        grid=(batch, nblk),
        in_specs=[pl.BlockSpec(memory_space=pltpu.SMEM),
                  col(fwd, 0), col(fwd, 1), col(fwd, 2), rows(fwd, LANES),
                  col(bwd, 0), col(bwd, 1), col(bwd, 2), rows(bwd, LANES),
                  _layer_spec(layer, (LANES, GLA_W)), _layer_spec(layer, (LANES, GLA_W)),
                  _layer_spec(layer, (1, GLA_W)), _layer_spec(layer, (1, GLA_W)),
                  rows(fwd, NA_W), full(NA_W, 1), full(NA_W, 2), _layer_spec(layer, na_bias.shape[1:]),
                  rows(fwd, SWA_W), full(SWA_KV_W), full(SWA_KV_W)],
        out_specs=[col(fwd, 0), col(bwd, 0), rows(soft_out, NA_W), rows(soft_out, SWA_W)],
        out_shape=[jax.ShapeDtypeStruct((batch, s, GLA_W), F32)] * 2
        + [jax.ShapeDtypeStruct((batch, s - first * tm, w), BF16) for w in (NA_W, SWA_W)],
        scratch_shapes=[pltpu.VMEM((GLA_W, GLA_W), F32)] * 2,
        compiler_params=_params(1, 1),
    )(sinks, gla_in, gla_in, gla_in, za, gla_in, gla_in, gla_in, za, wa_f, wa_b, ba_f, ba_b,
      na_qkv, na_qkv, na_qkv, na_bias, sq, sk, sv)


def _out_ffn_kernel(*refs, n_src, final):
    src_refs = refs[:n_src]
    (of_ref, ob_ref, gate_ref, na_ref, swa_ref, g1_ref, sh2_ref, sc2_ref, g2_ref, nffn_ref, gn_ref, avg_ref,
     wo_ref, wgate_ref, wup_ref, wdown_ref, fn_ref, o_ref) = refs[n_src:]
    x_in = _token_rows(src_refs)
    tm = x_in.shape[0]
    avg = avg_ref[...]
    halves = (slice(0, tm // 2), slice(tm // 2, tm))
    o = [of_ref[rows, :] + ob_ref[rows, :] for rows in halves]
    sq = [_split2(oh * oh) for oh in o]
    ms = [_dot(hi, avg) + _dot(lo, avg) for hi, lo in sq]
    y = [(oh * lax.rsqrt(msh + NORM_EPS) * gn_ref[...] * _silu(gate_ref[rows, :])).astype(BF16)
         for oh, msh, rows in zip(o, ms, halves)]
    mix = [_dot(jnp.concatenate([yh, na_ref[rows, :], swa_ref[rows, :]], axis=1), wo_ref[...])
           for yh, rows in zip(y, halves)]
    x1 = jnp.concatenate([x_in[rows] + g1_ref[...] * mh for mh, rows in zip(mix, halves)], axis=0)
    h = _rms(x1) * nffn_ref[...]
    h = (h * (1.0 + sc2_ref[...]) + sh2_ref[...]).astype(BF16)

    d_ff = wgate_ref.shape[1]
    chunks = [slice(c0, min(c0 + FF_CHUNK, d_ff)) for c0 in range(0, d_ff, FF_CHUNK)]
    acc = jnp.zeros(x1.shape, F32)
    gu = {}
    for t in range(len(chunks) + 1):
        if t < len(chunks):
            gu[t] = (_dot(h, wgate_ref[:, chunks[t]]), _dot(h, wup_ref[:, chunks[t]]))
        if t >= 1:
            g, u = gu.pop(t - 1)
            acc = acc + _dot((_silu(g) * u).astype(BF16), wdown_ref[chunks[t - 1], :])
    x2 = x1 + g2_ref[...] * acc
    if final:
        x2 = _rms(x2) * fn_ref[...]
    o_ref[...] = x2


def _out_ffn(srcs, o_f, o_b, gla_in, y_na, y_swa, layer, mods, norm_ffn, gla_norm, avg, w_o,
             w_gate, w_up, w_down, final_norm, skip_ctx, final):
    batch, s, _ = o_f.shape
    d = srcs[0].shape[2]
    d_ff = w_gate.shape[2]
    tm = ROW_BLOCK
    off = 1 if skip_ctx else 0
    nblk = s // tm - off
    tok_off = off
    if skip_ctx and len(srcs) == 2:
        srcs, tok_off = srcs[1:], 0
    row = lambda w, cb=0: pl.BlockSpec((None, tm, w), lambda b, j: (b, j + off, cb))
    out = lambda w: pl.BlockSpec((None, tm, w), lambda b, j: (b, j, 0))
    ms = lambda chunk: _mod_spec(layer, chunk, batch, not skip_ctx)
    return pl.pallas_call(
        functools.partial(_out_ffn_kernel, n_src=len(srcs), final=final),
        grid=(batch, nblk),
        in_specs=_token_specs(srcs, tm, tok_off) + [
            row(GLA_W), row(GLA_W), row(GLA_W, 3), out(NA_W), out(SWA_W),
            ms(2), ms(3), ms(4), ms(5), _layer_spec(layer, (1, d)), _layer_spec(layer, (1, GLA_W)),
            pl.BlockSpec((GLA_W, GLA_W), lambda b, j: (0, 0)),
            _layer_spec(layer, (GLA_W + NA_W + SWA_W, d)),
            _layer_spec(layer, (d, d_ff)), _layer_spec(layer, (d, d_ff)), _layer_spec(layer, (d_ff, d)),
            pl.BlockSpec((1, d), lambda b, j: (0, 0))],
        out_specs=out(d),
        out_shape=jax.ShapeDtypeStruct((batch, nblk * tm, d), F32),
        compiler_params=_params(2),
    )(*srcs, o_f, o_b, gla_in, y_na, y_swa, mods, mods, mods, mods, norm_ffn, gla_norm, avg, w_o,
      w_gate, w_up, w_down, final_norm)


def _rope_tables(ctx_len, seq_len):
    t = np.arange(seq_len)
    n_freq = HEAD_DIM // 4
    inv_freq = ROPE_THETA ** (-np.arange(n_freq) / n_freq)
    ang = np.concatenate([(t // GRID_W)[:, None] * inv_freq, (t % GRID_W)[:, None] * inv_freq], axis=-1)
    cos_h = np.concatenate([np.cos(ang), np.cos(ang)], axis=-1)
    sin_h = np.concatenate([-np.sin(ang), np.sin(ang)], axis=-1)
    cos_t = np.concatenate([np.ones((ctx_len, HEAD_DIM)), cos_h], axis=0)
    sin_t = np.concatenate([np.zeros((ctx_len, HEAD_DIM)), sin_h], axis=0)
    return jnp.asarray(np.tile(cos_t, (1, 2)), F32), jnp.asarray(np.tile(sin_t, (1, 2)), F32)


def _swa_pair_order(a, axis):
    n_pairs = SWA_W // PAIR_W
    heads = [h for p in range(n_pairs) for h in (p, p + n_pairs)]
    return jnp.concatenate([lax.slice_in_dim(a, h * HEAD_DIM, (h + 1) * HEAD_DIM, axis=axis) for h in heads],
                           axis=axis)


def _permute_w_in_kernel(w_ref, o_ref):
    scale = HEAD_DIM ** -0.5
    scale2 = scale * LOG2E
    o = np.cumsum([0, GLA_W, GLA_W, GLA_W, GLA_W, GLA_RANK, GLA_RANK, NA_W, NA_W, NA_W, SWA_W, SWA_KV_W, SWA_KV_W])
    n_pairs = SWA_W // PAIR_W
    swa_heads = [h for p in range(n_pairs) for h in (p, p + n_pairs)]
    moves = [(o[0], o[1], C_GLA, scale), (o[1], o[4], C_GLA + GLA_W, 1.0), (o[4], o[6], C_ZA, 1.0),
             (o[6], o[7], C_NA, scale2), (o[7], o[9], C_NA + NA_W, 1.0), (o[10], o[11], C_SK, 1.0),
             (o[11], o[12], C_SV, 1.0)]
    moves += [(o[9] + h * HEAD_DIM, o[9] + (h + 1) * HEAD_DIM, C_SQ + i * HEAD_DIM, scale2)
              for i, h in enumerate(swa_heads)]
    o_ref[:, C_ZA:IN_PERM_W] = jnp.zeros((o_ref.shape[0], LANES), o_ref.dtype)
    for lo, hi, dst, mult in moves:
        o_ref[:, dst:dst + (hi - lo)] = (w_ref[:, lo:hi] * mult).astype(o_ref.dtype)


def _permute_w_in(w):
    depth, d, n = w.shape
    tm = ROW_BLOCK
    return pl.pallas_call(
        _permute_w_in_kernel,
        grid=(depth, d // tm),
        in_specs=[pl.BlockSpec((None, tm, n), lambda i, k: (i, k, 0))],
        out_specs=pl.BlockSpec((None, tm, IN_PERM_W), lambda i, k: (i, k, 0)),
        out_shape=jax.ShapeDtypeStruct((depth, d, IN_PERM_W), BF16),
        compiler_params=_params(2),
    )(w)


def _na_bias_table(rpb):
    qc = np.arange(GRID_W)
    kc = np.arange(GRID_W)
    win = np.clip(qc - NA_KW // 2, 0, GRID_W - NA_KW)
    valid = (kc[None, :] >= win[:, None]) & (kc[None, :] < win[:, None] + NA_KW)
    dc = kc[None, :] - qc[:, None] + NA_KW - 1
    pick_col = ((dc[None] == np.arange(2 * NA_KW - 1)[:, None, None]) & valid[None]).astype(np.float32)
    t = jnp.einsum('lhde,eck->lhdck', rpb.astype(F32) * LOG2E, pick_col, precision=lax.Precision.HIGHEST)
    t = jnp.where(valid, t, -jnp.inf)
    return jnp.concatenate([t[:, :, :-1], t[:, :, 1:]], axis=-1)


def _pad_rank(wa2, offset):
    return jnp.pad(wa2, ((0, 0), (offset, LANES - offset - GLA_RANK), (0, 0)))


def kernel(x, c, ctx, c_ctx, w_mod, b_mod, norm_mix, norm_ffn, w_in, gla_wa2_f, gla_ba_f, gla_wa2_b, gla_ba_b,
           gla_norm, na_rpb, swa_sink, w_out, w_gate, w_up, w_down, final_norm):
    batch, seq_len, d = x.shape
    ctx_len = ctx.shape[1]
    depth = w_mod.shape[0]
    assert d == D_MODEL and ctx_len == ROW_BLOCK and seq_len % ROW_BLOCK == 0 and batch < 8
    assert seq_len % GRID_W == 0 and seq_len // GRID_W >= 2 * NA_KH

    cc = jnp.concatenate([c, c_ctx[None, :], jnp.zeros((8 - batch - 1, d), F32)], axis=0)
    mods = _modulation(cc, w_mod, b_mod).reshape(depth, 8, 1, 6 * d)
    cos_t, sin_t = _rope_tables(ctx_len, seq_len)
    head_avg = jnp.asarray(np.kron(np.eye(GLA_HEADS), np.full((HEAD_DIM, HEAD_DIM), 1.0 / HEAD_DIM)), BF16)
    w_in_p = _permute_w_in(w_in)
    wa_f, wa_b = _pad_rank(gla_wa2_f, 0), _pad_rank(gla_wa2_b, GLA_RANK)
    ba_f, ba_b = gla_ba_f.reshape(depth, 1, GLA_W), gla_ba_b.reshape(depth, 1, GLA_W)
    na_bias = _na_bias_table(na_rpb)
    w_o = jnp.concatenate([w_out[:, :GLA_W + NA_W], _swa_pair_order(w_out[:, GLA_W + NA_W:], 1)],
                          axis=1).astype(BF16)
    w_gate_b, w_up_b, w_down_b = w_gate.astype(BF16), w_up.astype(BF16), w_down.astype(BF16)
    norm_mix3, norm_ffn3 = norm_mix.reshape(depth, 1, d), norm_ffn.reshape(depth, 1, d)
    gla_norm3 = gla_norm.reshape(depth, 1, GLA_W)

    srcs = (ctx, x)
    for i in range(depth):
        last = i == depth - 1
        gla_in, za, na_qkv, sq, sk, sv = _in_proj(srcs, i, mods, norm_mix3, w_in_p, cos_t, sin_t)
        o_f, o_b, y_na, y_swa = _mixers(gla_in, za, na_qkv, sq, sk, sv, i, wa_f, wa_b, ba_f, ba_b, na_bias,
                                        swa_sink, ctx_len, ctx_out=not last)
        srcs = (_out_ffn(srcs, o_f, o_b, gla_in, y_na, y_swa, i, mods, norm_ffn3, gla_norm3, head_avg,
                         w_o, w_gate_b, w_up_b, w_down_b, final_norm.reshape(1, d),
                         skip_ctx=last, final=last),)
    return srcs[0]
```

```python
import functools

import jax
import jax.numpy as jnp
import numpy as np
from jax import lax
from jax.experimental import pallas as pl
from jax.experimental.pallas import tpu as pltpu

F32 = jnp.float32
BF16 = jnp.bfloat16

D_MODEL = 1024
HEAD_DIM = 64
GRID_W = 64
GLA_HEADS = 4
NA_HEADS = 6
SWA_HEADS = 6
SWA_KV_HEADS = 2
GLA_W = GLA_HEADS * HEAD_DIM
NA_W = NA_HEADS * HEAD_DIM
SWA_W = SWA_HEADS * HEAD_DIM
SWA_KV_W = SWA_KV_HEADS * HEAD_DIM
GLA_RANK = 16
GLA_TAU = 16.0
GLA_CHUNK = 64
NA_KH = 8
NA_KW = 16
SWA_WINDOW = 128
ROPE_THETA = 10000.0
NORM_EPS = 1e-6

LANES = 128
ROW_BLOCK = 256
PAIR_W = 2 * HEAD_DIM
LOG2E = float(np.log2(np.e))
FF_CHUNK = 256
VMEM_LIMIT = 56 * 1024 * 1024

C_GLA = 0
C_NA = C_GLA + 4 * GLA_W
C_SV = C_NA + 3 * NA_W
C_SQ = C_SV + SWA_KV_W
C_SK = C_SQ + SWA_W
C_ZA = C_SK + SWA_KV_W
IN_PERM_W = C_ZA + LANES


def _dot(a, b):
    return jnp.dot(a, b, preferred_element_type=F32)


def _dot_nt(a, b):
    return lax.dot_general(a, b, (((1,), (1,)), ((), ())), preferred_element_type=F32)


def _dot_tn(a, b):
    return lax.dot_general(a, b, (((0,), (0,)), ((), ())), preferred_element_type=F32)


def _idiv(a, n):
    assert n & (n - 1) == 0
    return a >> (n.bit_length() - 1)


def _imod(a, n):
    assert n & (n - 1) == 0
    return a & (n - 1)


def _split2(a):
    hi = a.astype(BF16)
    lo = (a - hi.astype(F32)).astype(BF16)
    return hi, lo


def _split3(a):
    hi = a.astype(BF16)
    r = a - hi.astype(F32)
    mid = r.astype(BF16)
    lo = (r - mid.astype(F32)).astype(BF16)
    return hi, mid, lo


def _dot_split(a, b):
    ah, al = _split2(a)
    bh, bl = _split2(b)
    return _dot(ah, bh) + _dot(al, bh) + _dot(ah, bl)


def _silu(a):
    return a * jax.nn.sigmoid(a)


def _rms(x):
    return x * lax.rsqrt(jnp.mean(x * x, axis=-1, keepdims=True) + NORM_EPS)


def _params(n_parallel, n_arbitrary=0):
    return pltpu.CompilerParams(
        dimension_semantics=("parallel",) * n_parallel + ("arbitrary",) * n_arbitrary,
        vmem_limit_bytes=VMEM_LIMIT)


def _mod_kernel(c_ref, w_ref, b_ref, o_ref):
    o_ref[...] = _dot_split(_silu(c_ref[...]), w_ref[...]) + b_ref[...]


def _modulation(cc, w_mod, b_mod):
    depth, d, n = w_mod.shape
    tn = 1536
    return pl.pallas_call(
        _mod_kernel,
        grid=(depth, n // tn),
        in_specs=[
            pl.BlockSpec((8, d), lambda i, k: (0, 0)),
            pl.BlockSpec((None, d, tn), lambda i, k: (i, 0, k)),
            pl.BlockSpec((None, 1, tn), lambda i, k: (i, 0, k)),
        ],
        out_specs=pl.BlockSpec((None, 8, tn), lambda i, k: (i, 0, k)),
        out_shape=jax.ShapeDtypeStruct((depth, 8, n), F32),
        compiler_params=_params(2),
    )(cc, w_mod, b_mod.reshape(depth, 1, n))


def _mod_spec(layer, chunk, batch, ctx_first):
    if ctx_first:
        return pl.BlockSpec((None, None, 1, D_MODEL), lambda b, j: (layer, jnp.where(j == 0, batch, b), 0, chunk))
    return pl.BlockSpec((None, None, 1, D_MODEL), lambda b, j: (layer, b, 0, chunk))


def _layer_spec(layer, shape):
    return pl.BlockSpec((None,) + shape, lambda b, j: (layer,) + (0,) * len(shape))


def _token_rows(src_refs):
    if len(src_refs) == 1:
        return src_refs[0][...]
    ctx_ref, x_ref = src_refs
    return jnp.where(pl.program_id(1) == 0, ctx_ref[...], x_ref[...])


def _token_specs(srcs, tm, off=0):
    d = srcs[0].shape[-1]
    if len(srcs) == 1:
        return [pl.BlockSpec((None, tm, d), lambda b, j: (b, j + off, 0))]
    assert off == 0
    return [pl.BlockSpec((None, tm, d), lambda b, j: (b, 0, 0)),
            pl.BlockSpec((None, tm, d), lambda b, j: (b, jnp.maximum(j - 1, 0), 0))]


def _in_proj_kernel(*refs, n_src):
    src_refs = refs[:n_src]
    sh_ref, sc_ref, nw_ref, w_ref, cos_ref, sin_ref, gla_ref, za_ref, na_ref, sq_ref, sk_ref, sv_ref = refs[n_src:]
    x = _token_rows(src_refs)
    tm = x.shape[0]
    lane = lax.broadcasted_iota(jnp.int32, (tm // 2, LANES), 1)
    first_half = _imod(lane, HEAD_DIM) < (HEAD_DIM // 2)
    n_q = SWA_W // LANES
    for rows in (slice(0, tm // 2), slice(tm // 2, tm)):
        h = _rms(x[rows]) * nw_ref[...]
        h = (h * (1.0 + sc_ref[...]) + sh_ref[...]).astype(BF16)
        qkz = _dot(h, w_ref[:, C_SQ:IN_PERM_W])
        qk = qkz[:, :C_ZA - C_SQ]
        cos = cos_ref[rows, :]
        sin = sin_ref[rows, :]
        for cb in range(n_q + 1):
            a = qk[:, cb * LANES:(cb + 1) * LANES]
            partner = jnp.where(first_half, pltpu.roll(a, LANES - HEAD_DIM // 2, 1), pltpu.roll(a, HEAD_DIM // 2, 1))
            r = (a * cos + partner * sin).astype(BF16)
            if cb < n_q:
                sq_ref[rows, cb * LANES:(cb + 1) * LANES] = r
            else:
                sk_ref[rows, :] = r
        za_ref[rows, :] = qkz[:, C_ZA - C_SQ:]
        gla_ref[rows, :] = _dot(h, w_ref[:, C_GLA:C_NA])
        nv = _dot(h, w_ref[:, C_NA:C_SQ]).astype(BF16)
        na_ref[rows, :] = nv[:, :C_SV - C_NA]
        sv_ref[rows, :] = nv[:, C_SV - C_NA:]


def _in_proj(srcs, layer, mods, norm_w, w_perm, cos_t, sin_t):
    batch, d = srcs[0].shape[0], srcs[0].shape[2]
    s = sum(a.shape[1] for a in srcs)
    tm = ROW_BLOCK
    row = lambda w: pl.BlockSpec((None, tm, w), lambda b, j: (b, j, 0))
    tab = pl.BlockSpec((tm, LANES), lambda b, j: (j, 0))
    return pl.pallas_call(
        functools.partial(_in_proj_kernel, n_src=len(srcs)),
        grid=(batch, s // tm),
        in_specs=_token_specs(srcs, tm) + [
            _mod_spec(layer, 0, batch, True), _mod_spec(layer, 1, batch, True), _layer_spec(layer, (1, d)),
            _layer_spec(layer, (d, IN_PERM_W)), tab, tab],
        out_specs=[row(4 * GLA_W), row(LANES), row(3 * NA_W), row(SWA_W), row(SWA_KV_W), row(SWA_KV_W)],
        out_shape=[
            jax.ShapeDtypeStruct((batch, s, 4 * GLA_W), F32),
            jax.ShapeDtypeStruct((batch, s, LANES), F32),
            jax.ShapeDtypeStruct((batch, s, 3 * NA_W), BF16),
            jax.ShapeDtypeStruct((batch, s, SWA_W), BF16),
            jax.ShapeDtypeStruct((batch, s, SWA_KV_W), BF16),
            jax.ShapeDtypeStruct((batch, s, SWA_KV_W), BF16),
        ],
        compiler_params=_params(2),
    )(*srcs, mods, mods, norm_w, w_perm, cos_t, sin_t)


def _gla_stages(qf_ref, kf_ref, vf_ref, zf_ref, qb_ref, kb_ref, vb_ref, zb_ref,
                waf_ref, wab_ref, baf_ref, bab_ref, of_ref, ob_ref, stf_ref, stb_ref):
    c = GLA_CHUNK
    w = GLA_W
    n_chunks = ROW_BLOCK // c
    r64 = lax.broadcasted_iota(jnp.int32, (c, c), 0)
    c64 = lax.broadcasted_iota(jnp.int32, (c, c), 1)
    row_h = _idiv(lax.broadcasted_iota(jnp.int32, (GLA_HEADS * c, w), 0), c)
    col_h = _idiv(lax.broadcasted_iota(jnp.int32, (GLA_HEADS * c, w), 1), c)
    same_head = row_h == col_h
    ar = lax.broadcasted_iota(jnp.int32, (c, w), 0)
    ac = _imod(lax.broadcasted_iota(jnp.int32, (c, w), 1), c)
    dirs = [
        dict(q=qf_ref, k=kf_ref, v=vf_ref, z=zf_ref, wa=waf_ref, ba=baf_ref, o=of_ref, st=stf_ref, rev=False,
             order=range(n_chunks), cum=(r64 >= c64).astype(BF16), causal=ar >= ac),
        dict(q=qb_ref, k=kb_ref, v=vb_ref, z=zb_ref, wa=wab_ref, ba=bab_ref, o=ob_ref, st=stb_ref, rev=True,
             order=range(n_chunks - 1, -1, -1), cum=(r64 <= c64).astype(BF16), causal=ar <= ac),
    ]
    units = [(d, ci) for d in dirs for ci in d['order']]
    rows = lambda ci: slice(ci * c, (ci + 1) * c)
    t = {}

    def decay():
        log_a = []
        for d in dirs:
            z = _dot_split(d['z'][...], d['wa'][...]) + d['ba'][...]
            log_a.append((jnp.minimum(z, 0.0) - jnp.log1p(jnp.exp(-jnp.abs(z)))) * (1.0 / GLA_TAU))
        t['b'], t['b_tot'] = [], []
        for (d, ci), la in zip(units, [la for la in log_a for _ in range(n_chunks)]):
            hi, mid, lo = _split3(la[rows(ci)])
            t['b'].append(_dot(d['cum'], hi) + _dot(d['cum'], mid) + _dot(d['cum'], lo))
            t['b_tot'].append(t['b'][-1][0:1] if d['rev'] else t['b'][-1][c - 1:c])

    def scale():
        b, b_tot = t['b'], t['b_tot']
        k = [d['k'][rows(ci), :] for d, ci in units]
        t['v'] = [d['v'][rows(ci), :].astype(BF16) for d, ci in units]
        t['q_in'] = [(d['q'][rows(ci), :] * jnp.exp(bu)).astype(BF16) for (d, ci), bu in zip(units, b)]
        t['k_in'] = [(ku * jnp.exp(-bu)).astype(BF16) for ku, bu in zip(k, b)]
        t['k_end'] = [(ku * jnp.exp(bt - bu)).astype(BF16) for ku, bu, bt in zip(k, b, b_tot)]
        expand = lambda a: jnp.where(same_head, jnp.concatenate([a] * GLA_HEADS, axis=0), jnp.zeros((), BF16))
        t['kx'] = [expand(ku) for ku in t['k_in']]
        t['vx'] = [expand(vu) for vu in t['v']]

    def intra():
        a = [jnp.where(d['causal'], _dot_nt(qu, kxu), 0.0).astype(BF16)
             for (d, _), qu, kxu in zip(units, t['q_in'], t['kx'])]
        t['o_intra'] = [_dot(au, vxu) for au, vxu in zip(a, t['vx'])]
        t['upd'] = [jnp.where(same_head, _dot_tn(vu, ku), 0.0) for vu, ku in zip(t['v'], t['k_end'])]

    def inter():
        st_enter = []
        for di, d in enumerate(dirs):
            st = d['st'][...]
            for u in range(di * n_chunks, (di + 1) * n_chunks):
                st_enter.append(st.astype(BF16))
                st = st * jnp.exp(t['b_tot'][u]) + t['upd'][u]
            d['st'][...] = st
        for (d, ci), qu, su, ou in zip(units, t['q_in'], st_enter, t['o_intra']):
            d['o'][rows(ci), :] = _dot_nt(qu, su) + ou

    return [decay, scale, intra, inter]


def _stack_pair(q):
    left = lax.broadcasted_iota(jnp.int32, q.shape, 1) < HEAD_DIM
    zero = jnp.zeros((), q.dtype)
    return jnp.concatenate([jnp.where(left, q, zero), jnp.where(left, zero, q)], axis=0)


def _unstack_pair(o):
    n = o.shape[0] // 2
    left = lax.broadcasted_iota(jnp.int32, (n, o.shape[1]), 1) < HEAD_DIM
    return jnp.where(left, o[:n], o[n:])


def _softmax_weights(scores, extra_logit=None):
    m = functools.reduce(jnp.maximum, [jnp.max(s, axis=-1, keepdims=True) for s in scores])
    if extra_logit is not None:
        m = jnp.maximum(m, extra_logit)
    ps = [jnp.exp2(s - m) for s in scores]
    denom = functools.reduce(jnp.add, [jnp.sum(p, axis=-1, keepdims=True) for p in ps])
    if extra_logit is not None:
        denom = denom + jnp.exp2(extra_logit - m)
    return [p.astype(BF16) for p in ps], denom


def _na_stages(q_ref, k_ref, v_ref, bias_ref, o_ref, j, ctx_len, grid_rows, is_ctx):
    n_pairs = NA_W // PAIR_W
    rows_per_block = ROW_BLOCK // GRID_W
    n_loc = NA_KH * GRID_W
    unit = 2 * GRID_W
    pair_cols = [slice(p * PAIR_W, (p + 1) * PAIR_W) for p in range(n_pairs)]
    units = [(rr, p) for p in range(n_pairs) for rr in range(rows_per_block)]
    ctx_rows = lambda rr: slice(rr * unit, (rr + 1) * unit)
    t = {}

    def scores():
        qx = {(rr, p): _stack_pair(q_ref[rr * GRID_W:(rr + 1) * GRID_W, pair_cols[p]]) for rr, p in units}
        t['s_ctx'] = [_dot_nt(jnp.concatenate([qx[rr, p] for rr in range(rows_per_block)], axis=0),
                              k_ref[0:ctx_len, pair_cols[p]]) for p in range(n_pairs)]
        if is_ctx:
            return
        t['k0'], bias = [], {}
        for rr in range(rows_per_block):
            r = (j - 1) * rows_per_block + rr
            first_key_row = jnp.clip(r - NA_KH // 2, 0, grid_rows - NA_KH)
            t['k0'].append(pl.multiple_of(ctx_len + first_key_row * GRID_W, GRID_W))
            d0 = first_key_row - r + NA_KH - 1
            for p in range(n_pairs):
                bias[rr, p] = jnp.concatenate(
                    [jnp.concatenate([bias_ref[2 * p + side, d0 + 2 * m] for m in range(NA_KH // 2)], axis=1)
                     for side in (0, 1)], axis=0)
        t['s_loc'] = {(rr, p): _dot_nt(qx[rr, p], k_ref[pl.ds(t['k0'][rr], n_loc), pair_cols[p]]) + bias[rr, p]
                      for rr, p in units}

    def softmax():
        t['w'] = {(rr, p): _softmax_weights(([] if is_ctx else [t['s_loc'][rr, p]]) + [t['s_ctx'][p][ctx_rows(rr)]])
                  for rr, p in units}

    def values():
        w = t['w']
        o_ctx = [_dot(jnp.concatenate([w[rr, p][0][-1] for rr in range(rows_per_block)], axis=0),
                      v_ref[0:ctx_len, pair_cols[p]]) for p in range(n_pairs)]
        for rr, p in units:
            o = o_ctx[p][ctx_rows(rr)]
            if not is_ctx:
                o = o + _dot(w[rr, p][0][0], v_ref[pl.ds(t['k0'][rr], n_loc), pair_cols[p]])
            o_ref[rr * GRID_W:(rr + 1) * GRID_W, pair_cols[p]] = _unstack_pair(o / w[rr, p][1]).astype(o_ref.dtype)

    return [scores, softmax, values]


def _swa_stages(sink, q_ref, k_ref, v_ref, o_ref, j, ctx_len, seq_len, is_ctx):
    tm = ROW_BLOCK
    n_pairs = SWA_W // PAIR_W
    n_loc = tm + 2 * SWA_WINDOW
    pair_cols = [slice(p * PAIR_W, (p + 1) * PAIR_W) for p in range(n_pairs)]
    lane = lax.broadcasted_iota(jnp.int32, (tm, PAIR_W), 1)
    zero = jnp.zeros((), BF16)
    units = [(p + side * n_pairs, p, side) for p in range(n_pairs) for side in (0, 1)]
    t = {}

    def scores():
        qx = [jnp.where((lane < HEAD_DIM) == (side == 0), q_ref[:, pair_cols[p]], zero) for _, p, side in units]
        k_ctx = k_ref[0:ctx_len, :]
        t['s'] = [[_dot_nt(q, k_ctx)] for q in qx]
        if is_ctx:
            return
        q0 = (j - 1) * tm
        k0 = jnp.clip(q0 - SWA_WINDOW, 0, seq_len - n_loc)
        qpos = q0 + lax.broadcasted_iota(jnp.int32, (tm, n_loc), 0)
        kpos = k0 + lax.broadcasted_iota(jnp.int32, (tm, n_loc), 1)
        in_window = jnp.abs(qpos - kpos) <= SWA_WINDOW
        t['start'] = pl.multiple_of(ctx_len + k0, SWA_WINDOW)
        k_loc = k_ref[pl.ds(t['start'], n_loc), :]
        for s, q in zip(t['s'], qx):
            s.insert(0, jnp.where(in_window, _dot_nt(q, k_loc), -jnp.inf))

    def softmax():
        t['w'] = [_softmax_weights(s, sink[h] * LOG2E) for s, (h, _, _) in zip(t['s'], units)]

    def values():
        vs = [v_ref[0:ctx_len, :]]
        if not is_ctx:
            vs.insert(0, v_ref[pl.ds(t['start'], n_loc), :])
        o = [functools.reduce(jnp.add, [_dot(p, v) for p, v in zip(ps, vs)]) / denom for ps, denom in t['w']]
        for p, cols in enumerate(pair_cols):
            o_ref[:, cols] = jnp.where(lane < HEAD_DIM, o[2 * p], o[2 * p + 1]).astype(o_ref.dtype)

    return [scores, softmax, values]


def _mixers_kernel(sinks_ref, qf_ref, kf_ref, vf_ref, zf_ref, qb_ref, kb_ref, vb_ref, zb_ref,
                   waf_ref, wab_ref, baf_ref, bab_ref, nq_ref, nk_ref, nv_ref, nbias_ref, sq_ref, sk_ref, sv_ref,
                   of_ref, ob_ref, yna_ref, yswa_ref, stf_ref, stb_ref, *, layer, ctx_len, seq_len, ctx_out):
    j = pl.program_id(1)

    @pl.when(j == 0)
    def _():
        stf_ref[...] = jnp.zeros_like(stf_ref)
        stb_ref[...] = jnp.zeros_like(stb_ref)

    sink = [sinks_ref[layer, h] for h in range(SWA_HEADS)]

    def block(is_ctx):
        gla = _gla_stages(qf_ref, kf_ref, vf_ref, zf_ref, qb_ref, kb_ref, vb_ref, zb_ref,
                          waf_ref, wab_ref, baf_ref, bab_ref, of_ref, ob_ref, stf_ref, stb_ref)
        if is_ctx and not ctx_out:
            order = gla
        else:
            na = _na_stages(nq_ref, nk_ref, nv_ref, nbias_ref, yna_ref, j, ctx_len, seq_len // GRID_W, is_ctx)
            swa = _swa_stages(sink, sq_ref, sk_ref, sv_ref, yswa_ref, j, ctx_len, seq_len, is_ctx)
            order = [gla[0], na[0], swa[0], gla[1], gla[2], na[1], na[2], swa[1], swa[2], gla[3]]
        for stage in order:
            stage()

    pl.when(j == 0)(functools.partial(block, True))
    pl.when(j > 0)(functools.partial(block, False))


def _mixers(gla_in, za, na_qkv, sq, sk, sv, layer, wa_f, wa_b, ba_f, ba_b, na_bias, sinks, ctx_len, ctx_out):
    batch, s, _ = gla_in.shape
    tm = ROW_BLOCK
    nblk = s // tm
    first = 0 if ctx_out else 1
    fwd = lambda j: j
    bwd = lambda j: jnp.where(j == 0, 0, nblk - j)
    soft_out = lambda j: jnp.maximum(j - first, 0)

    def col(order, cb):
        return pl.BlockSpec((None, tm, GLA_W), lambda b, j: (b, order(j), cb))

    def rows(order, w):
        return pl.BlockSpec((None, tm, w), lambda b, j: (b, order(j), 0))

    def full(w, cb=0):
        return pl.BlockSpec((None, s, w), lambda b, j: (b, 0, cb))

    return pl.pallas_call(
        functools.partial(_mixers_kernel, layer=layer, ctx_len=ctx_len, seq_len=s - ctx_len, ctx_out=ctx_out),
        grid=(batch, nblk),
        in_specs=[pl.BlockSpec(memory_space=pltpu.SMEM),
                  col(fwd, 0), col(fwd, 1), col(fwd, 2), rows(fwd, LANES),
                  col(bwd, 0), col(bwd, 1), col(bwd, 2), rows(bwd, LANES),
                  _layer_spec(layer, (LANES, GLA_W)), _layer_spec(layer, (LANES, GLA_W)),
                  _layer_spec(layer, (1, GLA_W)), _layer_spec(layer, (1, GLA_W)),
                  rows(fwd, NA_W), full(NA_W, 1), full(NA_W, 2), _layer_spec(layer, na_bias.shape[1:]),
                  rows(fwd, SWA_W), full(SWA_KV_W), full(SWA_KV_W)],
        out_specs=[col(fwd, 0), col(bwd, 0), rows(soft_out, NA_W), rows(soft_out, SWA_W)],
        out_shape=[jax.ShapeDtypeStruct((batch, s, GLA_W), F32)] * 2
        + [jax.ShapeDtypeStruct((batch, s - first * tm, w), BF16) for w in (NA_W, SWA_W)],
        scratch_shapes=[pltpu.VMEM((GLA_W, GLA_W), F32)] * 2,
        compiler_params=_params(1, 1),
    )(sinks, gla_in, gla_in, gla_in, za, gla_in, gla_in, gla_in, za, wa_f, wa_b, ba_f, ba_b,
      na_qkv, na_qkv, na_qkv, na_bias, sq, sk, sv)


def _out_ffn_kernel(*refs, n_src, final):
    src_refs = refs[:n_src]
    (of_ref, ob_ref, gate_ref, na_ref, swa_ref, g1_ref, sh2_ref, sc2_ref, g2_ref, nffn_ref, gn_ref, avg_ref,
     wo_ref, wgate_ref, wup_ref, wdown_ref, fn_ref, o_ref) = refs[n_src:]
    x_in = _token_rows(src_refs)
    tm = x_in.shape[0]
    avg = avg_ref[...]
    halves = (slice(0, tm // 2), slice(tm // 2, tm))
    o = [of_ref[rows, :] + ob_ref[rows, :] for rows in halves]
    sq = [_split2(oh * oh) for oh in o]
    ms = [_dot(hi, avg) + _dot(lo, avg) for hi, lo in sq]
    y = [(oh * lax.rsqrt(msh + NORM_EPS) * gn_ref[...] * _silu(gate_ref[rows, :])).astype(BF16)
         for oh, msh, rows in zip(o, ms, halves)]
    mix = [_dot(jnp.concatenate([yh, na_ref[rows, :], swa_ref[rows, :]], axis=1), wo_ref[...])
           for yh, rows in zip(y, halves)]
    x1 = jnp.concatenate([x_in[rows] + g1_ref[...] * mh for mh, rows in zip(mix, halves)], axis=0)
    h = _rms(x1) * nffn_ref[...]
    h = (h * (1.0 + sc2_ref[...]) + sh2_ref[...]).astype(BF16)

    d_ff = wgate_ref.shape[1]
    chunks = [slice(c0, min(c0 + FF_CHUNK, d_ff)) for c0 in range(0, d_ff, FF_CHUNK)]
    acc = jnp.zeros(x1.shape, F32)
    gu = {}
    for t in range(len(chunks) + 1):
        if t < len(chunks):
            gu[t] = (_dot(h, wgate_ref[:, chunks[t]]), _dot(h, wup_ref[:, chunks[t]]))
        if t >= 1:
            g, u = gu.pop(t - 1)
            acc = acc + _dot((_silu(g) * u).astype(BF16), wdown_ref[chunks[t - 1], :])
    x2 = x1 + g2_ref[...] * acc
    if final:
        x2 = _rms(x2) * fn_ref[...]
    o_ref[...] = x2


def _out_ffn(srcs, o_f, o_b, gla_in, y_na, y_swa, layer, mods, norm_ffn, gla_norm, avg, w_o,
             w_gate, w_up, w_down, final_norm, skip_ctx, final):
    batch, s, _ = o_f.shape
    d = srcs[0].shape[2]
    d_ff = w_gate.shape[2]
    tm = ROW_BLOCK
    off = 1 if skip_ctx else 0
    nblk = s // tm - off
    tok_off = off
    if skip_ctx and len(srcs) == 2:
        srcs, tok_off = srcs[1:], 0
    row = lambda w, cb=0: pl.BlockSpec((None, tm, w), lambda b, j: (b, j + off, cb))
    out = lambda w: pl.BlockSpec((None, tm, w), lambda b, j: (b, j, 0))
    ms = lambda chunk: _mod_spec(layer, chunk, batch, not skip_ctx)
    return pl.pallas_call(
        functools.partial(_out_ffn_kernel, n_src=len(srcs), final=final),
        grid=(batch, nblk),
        in_specs=_token_specs(srcs, tm, tok_off) + [
            row(GLA_W), row(GLA_W), row(GLA_W, 3), out(NA_W), out(SWA_W),
            ms(2), ms(3), ms(4), ms(5), _layer_spec(layer, (1, d)), _layer_spec(layer, (1, GLA_W)),
            pl.BlockSpec((GLA_W, GLA_W), lambda b, j: (0, 0)),
            _layer_spec(layer, (GLA_W + NA_W + SWA_W, d)),
            _layer_spec(layer, (d, d_ff)), _layer_spec(layer, (d, d_ff)), _layer_spec(layer, (d_ff, d)),
            pl.BlockSpec((1, d), lambda b, j: (0, 0))],
        out_specs=out(d),
        out_shape=jax.ShapeDtypeStruct((batch, nblk * tm, d), F32),
        compiler_params=_params(2),
    )(*srcs, o_f, o_b, gla_in, y_na, y_swa, mods, mods, mods, mods, norm_ffn, gla_norm, avg, w_o,
      w_gate, w_up, w_down, final_norm)


def _rope_tables(ctx_len, seq_len):
    t = np.arange(seq_len)
    n_freq = HEAD_DIM // 4
    inv_freq = ROPE_THETA ** (-np.arange(n_freq) / n_freq)
    ang = np.concatenate([(t // GRID_W)[:, None] * inv_freq, (t % GRID_W)[:, None] * inv_freq], axis=-1)
    cos_h = np.concatenate([np.cos(ang), np.cos(ang)], axis=-1)
    sin_h = np.concatenate([-np.sin(ang), np.sin(ang)], axis=-1)
    cos_t = np.concatenate([np.ones((ctx_len, HEAD_DIM)), cos_h], axis=0)
    sin_t = np.concatenate([np.zeros((ctx_len, HEAD_DIM)), sin_h], axis=0)
    return jnp.asarray(np.tile(cos_t, (1, 2)), F32), jnp.asarray(np.tile(sin_t, (1, 2)), F32)


def _swa_pair_order(a, axis):
    n_pairs = SWA_W // PAIR_W
    heads = [h for p in range(n_pairs) for h in (p, p + n_pairs)]
    return jnp.concatenate([lax.slice_in_dim(a, h * HEAD_DIM, (h + 1) * HEAD_DIM, axis=axis) for h in heads],
                           axis=axis)


def _permute_w_in_kernel(w_ref, o_ref):
    scale = HEAD_DIM ** -0.5
    scale2 = scale * LOG2E
    o = np.cumsum([0, GLA_W, GLA_W, GLA_W, GLA_W, GLA_RANK, GLA_RANK, NA_W, NA_W, NA_W, SWA_W, SWA_KV_W, SWA_KV_W])
    n_pairs = SWA_W // PAIR_W
    swa_heads = [h for p in range(n_pairs) for h in (p, p + n_pairs)]
    moves = [(o[0], o[1], C_GLA, scale), (o[1], o[4], C_GLA + GLA_W, 1.0), (o[4], o[6], C_ZA, 1.0),
             (o[6], o[7], C_NA, scale2), (o[7], o[9], C_NA + NA_W, 1.0), (o[10], o[11], C_SK, 1.0),
             (o[11], o[12], C_SV, 1.0)]
    moves += [(o[9] + h * HEAD_DIM, o[9] + (h + 1) * HEAD_DIM, C_SQ + i * HEAD_DIM, scale2)
              for i, h in enumerate(swa_heads)]
    o_ref[:, C_ZA:IN_PERM_W] = jnp.zeros((o_ref.shape[0], LANES), o_ref.dtype)
    for lo, hi, dst, mult in moves:
        o_ref[:, dst:dst + (hi - lo)] = (w_ref[:, lo:hi] * mult).astype(o_ref.dtype)


def _permute_w_in(w):
    depth, d, n = w.shape
    tm = ROW_BLOCK
    return pl.pallas_call(
        _permute_w_in_kernel,
        grid=(depth, d // tm),
        in_specs=[pl.BlockSpec((None, tm, n), lambda i, k: (i, k, 0))],
        out_specs=pl.BlockSpec((None, tm, IN_PERM_W), lambda i, k: (i, k, 0)),
        out_shape=jax.ShapeDtypeStruct((depth, d, IN_PERM_W), BF16),
        compiler_params=_params(2),
    )(w)


def _na_bias_table(rpb):
    qc = np.arange(GRID_W)
    kc = np.arange(GRID_W)
    win = np.clip(qc - NA_KW // 2, 0, GRID_W - NA_KW)
    valid = (kc[None, :] >= win[:, None]) & (kc[None, :] < win[:, None] + NA_KW)
    dc = kc[None, :] - qc[:, None] + NA_KW - 1
    pick_col = ((dc[None] == np.arange(2 * NA_KW - 1)[:, None, None]) & valid[None]).astype(np.float32)
    t = jnp.einsum('lhde,eck->lhdck', rpb.astype(F32) * LOG2E, pick_col, precision=lax.Precision.HIGHEST)
    t = jnp.where(valid, t, -jnp.inf)
    return jnp.concatenate([t[:, :, :-1], t[:, :, 1:]], axis=-1)


def _pad_rank(wa2, offset):
    return jnp.pad(wa2, ((0, 0), (offset, LANES - offset - GLA_RANK), (0, 0)))


def kernel(x, c, ctx, c_ctx, w_mod, b_mod, norm_mix, norm_ffn, w_in, gla_wa2_f, gla_ba_f, gla_wa2_b, gla_ba_b,
           gla_norm, na_rpb, swa_sink, w_out, w_gate, w_up, w_down, final_norm):
    batch, seq_len, d = x.shape
    ctx_len = ctx.shape[1]
    depth = w_mod.shape[0]
    assert d == D_MODEL and ctx_len == ROW_BLOCK and seq_len % ROW_BLOCK == 0 and batch < 8
    assert seq_len % GRID_W == 0 and seq_len // GRID_W >= 2 * NA_KH

    cc = jnp.concatenate([c, c_ctx[None, :], jnp.zeros((8 - batch - 1, d), F32)], axis=0)
    mods = _modulation(cc, w_mod, b_mod).reshape(depth, 8, 1, 6 * d)
    cos_t, sin_t = _rope_tables(ctx_len, seq_len)
    head_avg = jnp.asarray(np.kron(np.eye(GLA_HEADS), np.full((HEAD_DIM, HEAD_DIM), 1.0 / HEAD_DIM)), BF16)
    w_in_p = _permute_w_in(w_in)
    wa_f, wa_b = _pad_rank(gla_wa2_f, 0), _pad_rank(gla_wa2_b, GLA_RANK)
    ba_f, ba_b = gla_ba_f.reshape(depth, 1, GLA_W), gla_ba_b.reshape(depth, 1, GLA_W)
    na_bias = _na_bias_table(na_rpb)
    w_o = jnp.concatenate([w_out[:, :GLA_W + NA_W], _swa_pair_order(w_out[:, GLA_W + NA_W:], 1)],
                          axis=1).astype(BF16)
    w_gate_b, w_up_b, w_down_b = w_gate.astype(BF16), w_up.astype(BF16), w_down.astype(BF16)
    norm_mix3, norm_ffn3 = norm_mix.reshape(depth, 1, d), norm_ffn.reshape(depth, 1, d)
    gla_norm3 = gla_norm.reshape(depth, 1, GLA_W)

    srcs = (ctx, x)
    for i in range(depth):
        last = i == depth - 1
        gla_in, za, na_qkv, sq, sk, sv = _in_proj(srcs, i, mods, norm_mix3, w_in_p, cos_t, sin_t)
        o_f, o_b, y_na, y_swa = _mixers(gla_in, za, na_qkv, sq, sk, sv, i, wa_f, wa_b, ba_f, ba_b, na_bias,
                                        swa_sink, ctx_len, ctx_out=not last)
        srcs = (_out_ffn(srcs, o_f, o_b, gla_in, y_na, y_swa, i, mods, norm_ffn3, gla_norm3, head_avg,
                         w_o, w_gate_b, w_up_b, w_down_b, final_norm.reshape(1, d),
                         skip_ctx=last, final=last),)
    return srcs[0]
```

```python
import functools

import jax
import jax.numpy as jnp
import numpy as np
from jax import lax
from jax.experimental import pallas as pl
from jax.experimental.pallas import tpu as pltpu

F32 = jnp.float32
BF16 = jnp.bfloat16

D_MODEL = 1024
HEAD_DIM = 64
GRID_W = 64
GLA_HEADS = 4
NA_HEADS = 6
SWA_HEADS = 6
SWA_KV_HEADS = 2
GLA_W = GLA_HEADS * HEAD_DIM
NA_W = NA_HEADS * HEAD_DIM
SWA_W = SWA_HEADS * HEAD_DIM
SWA_KV_W = SWA_KV_HEADS * HEAD_DIM
GLA_RANK = 16
GLA_TAU = 16.0
GLA_CHUNK = 64
NA_KH = 8
NA_KW = 16
SWA_WINDOW = 128
ROPE_THETA = 10000.0
NORM_EPS = 1e-6

LANES = 128
ROW_BLOCK = 256
PAIR_W = 2 * HEAD_DIM
LOG2E = float(np.log2(np.e))
FF_CHUNK = 256
MOD_BLOCK_COLS = 1536
V7X_VMEM_BYTES = 64 * 1024 * 1024
VMEM_LIMIT = V7X_VMEM_BYTES * 7 // 8

C_GLA = 0
C_NA = C_GLA + 4 * GLA_W
C_SV = C_NA + 3 * NA_W
C_SQ = C_SV + SWA_KV_W
C_SK = C_SQ + SWA_W
C_ZA = C_SK + SWA_KV_W
IN_PERM_W = C_ZA + LANES


def _dot(a, b):
    return jnp.dot(a, b, preferred_element_type=F32)


def _dot_nt(a, b):
    return lax.dot_general(a, b, (((1,), (1,)), ((), ())), preferred_element_type=F32)


def _dot_tn(a, b):
    return lax.dot_general(a, b, (((0,), (0,)), ((), ())), preferred_element_type=F32)


def _idiv(a, n):
    assert n & (n - 1) == 0
    return a >> (n.bit_length() - 1)


def _imod(a, n):
    assert n & (n - 1) == 0
    return a & (n - 1)


def _split2(a):
    hi = a.astype(BF16)
    lo = (a - hi.astype(F32)).astype(BF16)
    return hi, lo


def _split3(a):
    hi = a.astype(BF16)
    r = a - hi.astype(F32)
    mid = r.astype(BF16)
    lo = (r - mid.astype(F32)).astype(BF16)
    return hi, mid, lo


def _dot_split(a, b):
    ah, al = _split2(a)
    bh, bl = _split2(b)
    return _dot(ah, bh) + _dot(al, bh) + _dot(ah, bl)


def _silu(a):
    return a * jax.nn.sigmoid(a)


def _rms(x):
    return x * lax.rsqrt(jnp.mean(x * x, axis=-1, keepdims=True) + NORM_EPS)


def _params(n_parallel, n_arbitrary=0):
    return pltpu.CompilerParams(
        dimension_semantics=("parallel",) * n_parallel + ("arbitrary",) * n_arbitrary,
        vmem_limit_bytes=VMEM_LIMIT)


def _mod_kernel(c_ref, w_ref, b_ref, o_ref):
    o_ref[...] = _dot_split(_silu(c_ref[...]), w_ref[...]) + b_ref[...]


def _modulation(cc, w_mod, b_mod):
    depth, d, n = w_mod.shape
    tn = MOD_BLOCK_COLS
    assert n % tn == 0
    return pl.pallas_call(
        _mod_kernel,
        grid=(depth, n // tn),
        in_specs=[
            pl.BlockSpec((8, d), lambda i, k: (0, 0)),
            pl.BlockSpec((None, d, tn), lambda i, k: (i, 0, k)),
            pl.BlockSpec((None, 1, tn), lambda i, k: (i, 0, k)),
        ],
        out_specs=pl.BlockSpec((None, 8, tn), lambda i, k: (i, 0, k)),
        out_shape=jax.ShapeDtypeStruct((depth, 8, n), F32),
        compiler_params=_params(2),
    )(cc, w_mod, b_mod.reshape(depth, 1, n))


def _mod_spec(layer, chunk, batch, ctx_first):
    if ctx_first:
        return pl.BlockSpec((None, None, 1, D_MODEL), lambda b, j: (layer, jnp.where(j == 0, batch, b), 0, chunk))
    return pl.BlockSpec((None, None, 1, D_MODEL), lambda b, j: (layer, b, 0, chunk))


def _layer_spec(layer, shape):
    return pl.BlockSpec((None,) + shape, lambda b, j: (layer,) + (0,) * len(shape))


def _token_rows(src_refs):
    if len(src_refs) == 1:
        return src_refs[0][...]
    ctx_ref, x_ref = src_refs
    return jnp.where(pl.program_id(1) == 0, ctx_ref[...], x_ref[...])


def _token_specs(srcs, tm, off=0):
    d = srcs[0].shape[-1]
    if len(srcs) == 1:
        return [pl.BlockSpec((None, tm, d), lambda b, j: (b, j + off, 0))]
    assert off == 0
    return [pl.BlockSpec((None, tm, d), lambda b, j: (b, 0, 0)),
            pl.BlockSpec((None, tm, d), lambda b, j: (b, jnp.maximum(j - 1, 0), 0))]


def _in_proj_kernel(*refs, n_src):
    src_refs = refs[:n_src]
    sh_ref, sc_ref, nw_ref, w_ref, cos_ref, sin_ref, gla_ref, za_ref, na_ref, sq_ref, sk_ref, sv_ref = refs[n_src:]
    x = _token_rows(src_refs)
    tm = x.shape[0]
    lane = lax.broadcasted_iota(jnp.int32, (tm // 2, LANES), 1)
    first_half = _imod(lane, HEAD_DIM) < (HEAD_DIM // 2)
    n_q = SWA_W // LANES
    for rows in (slice(0, tm // 2), slice(tm // 2, tm)):
        h = _rms(x[rows]) * nw_ref[...]
        h = (h * (1.0 + sc_ref[...]) + sh_ref[...]).astype(BF16)
        qkz = _dot(h, w_ref[:, C_SQ:IN_PERM_W])
        qk = qkz[:, :C_ZA - C_SQ]
        cos = cos_ref[rows, :]
        sin = sin_ref[rows, :]
        for cb in range(n_q + 1):
            a = qk[:, cb * LANES:(cb + 1) * LANES]
            partner = jnp.where(first_half, pltpu.roll(a, LANES - HEAD_DIM // 2, 1), pltpu.roll(a, HEAD_DIM // 2, 1))
            r = (a * cos + partner * sin).astype(BF16)
            if cb < n_q:
                sq_ref[rows, cb * LANES:(cb + 1) * LANES] = r
            else:
                sk_ref[rows, :] = r
        za_ref[rows, :] = qkz[:, C_ZA - C_SQ:]
        gla_ref[rows, :] = _dot(h, w_ref[:, C_GLA:C_NA])
        nv = _dot(h, w_ref[:, C_NA:C_SQ]).astype(BF16)
        na_ref[rows, :] = nv[:, :C_SV - C_NA]
        sv_ref[rows, :] = nv[:, C_SV - C_NA:]


def _in_proj(srcs, layer, mods, norm_w, w_perm, cos_t, sin_t):
    batch, d = srcs[0].shape[0], srcs[0].shape[2]
    s = sum(a.shape[1] for a in srcs)
    tm = ROW_BLOCK
    row = lambda w: pl.BlockSpec((None, tm, w), lambda b, j: (b, j, 0))
    tab = pl.BlockSpec((tm, LANES), lambda b, j: (j, 0))
    return pl.pallas_call(
        functools.partial(_in_proj_kernel, n_src=len(srcs)),
        grid=(batch, s // tm),
        in_specs=_token_specs(srcs, tm) + [
            _mod_spec(layer, 0, batch, True), _mod_spec(layer, 1, batch, True), _layer_spec(layer, (1, d)),
            _layer_spec(layer, (d, IN_PERM_W)), tab, tab],
        out_specs=[row(4 * GLA_W), row(LANES), row(3 * NA_W), row(SWA_W), row(SWA_KV_W), row(SWA_KV_W)],
        out_shape=[
            jax.ShapeDtypeStruct((batch, s, 4 * GLA_W), F32),
            jax.ShapeDtypeStruct((batch, s, LANES), F32),
            jax.ShapeDtypeStruct((batch, s, 3 * NA_W), BF16),
            jax.ShapeDtypeStruct((batch, s, SWA_W), BF16),
            jax.ShapeDtypeStruct((batch, s, SWA_KV_W), BF16),
            jax.ShapeDtypeStruct((batch, s, SWA_KV_W), BF16),
        ],
        compiler_params=_params(2),
    )(*srcs, mods, mods, norm_w, w_perm, cos_t, sin_t)


def _gla_stages(qf_ref, kf_ref, vf_ref, zf_ref, qb_ref, kb_ref, vb_ref, zb_ref,
                waf_ref, wab_ref, baf_ref, bab_ref, of_ref, ob_ref, stf_ref, stb_ref):
    c = GLA_CHUNK
    w = GLA_W
    n_chunks = ROW_BLOCK // c
    r64 = lax.broadcasted_iota(jnp.int32, (c, c), 0)
    c64 = lax.broadcasted_iota(jnp.int32, (c, c), 1)
    row_h = _idiv(lax.broadcasted_iota(jnp.int32, (GLA_HEADS * c, w), 0), c)
    col_h = _idiv(lax.broadcasted_iota(jnp.int32, (GLA_HEADS * c, w), 1), c)
    same_head = row_h == col_h
    ar = lax.broadcasted_iota(jnp.int32, (c, w), 0)
    ac = _imod(lax.broadcasted_iota(jnp.int32, (c, w), 1), c)
    dirs = [
        dict(q=qf_ref, k=kf_ref, v=vf_ref, z=zf_ref, wa=waf_ref, ba=baf_ref, o=of_ref, st=stf_ref, rev=False,
             order=range(n_chunks), cum=(r64 >= c64).astype(BF16), causal=ar >= ac),
        dict(q=qb_ref, k=kb_ref, v=vb_ref, z=zb_ref, wa=wab_ref, ba=bab_ref, o=ob_ref, st=stb_ref, rev=True,
             order=range(n_chunks - 1, -1, -1), cum=(r64 <= c64).astype(BF16), causal=ar <= ac),
    ]
    units = [(d, ci) for d in dirs for ci in d['order']]
    rows = lambda ci: slice(ci * c, (ci + 1) * c)
    t = {}

    def decay():
        log_a = []
        for d in dirs:
            z = _dot_split(d['z'][...], d['wa'][...]) + d['ba'][...]
            log_a.append((jnp.minimum(z, 0.0) - jnp.log1p(jnp.exp(-jnp.abs(z)))) * (1.0 / GLA_TAU))
        t['b'], t['b_tot'] = [], []
        for (d, ci), la in zip(units, [la for la in log_a for _ in range(n_chunks)]):
            hi, mid, lo = _split3(la[rows(ci)])
            t['b'].append(_dot(d['cum'], hi) + _dot(d['cum'], mid) + _dot(d['cum'], lo))
            t['b_tot'].append(t['b'][-1][0:1] if d['rev'] else t['b'][-1][c - 1:c])

    def scale():
        b, b_tot = t['b'], t['b_tot']
        k = [d['k'][rows(ci), :] for d, ci in units]
        t['v'] = [d['v'][rows(ci), :].astype(BF16) for d, ci in units]
        t['q_in'] = [(d['q'][rows(ci), :] * jnp.exp(bu)).astype(BF16) for (d, ci), bu in zip(units, b)]
        t['k_in'] = [(ku * jnp.exp(-bu)).astype(BF16) for ku, bu in zip(k, b)]
        t['k_end'] = [(ku * jnp.exp(bt - bu)).astype(BF16) for ku, bu, bt in zip(k, b, b_tot)]
        expand = lambda a: jnp.where(same_head, jnp.concatenate([a] * GLA_HEADS, axis=0), jnp.zeros((), BF16))
        t['kx'] = [expand(ku) for ku in t['k_in']]
        t['vx'] = [expand(vu) for vu in t['v']]

    def intra():
        a = [jnp.where(d['causal'], _dot_nt(qu, kxu), 0.0).astype(BF16)
             for (d, _), qu, kxu in zip(units, t['q_in'], t['kx'])]
        t['o_intra'] = [_dot(au, vxu) for au, vxu in zip(a, t['vx'])]
        t['upd'] = [jnp.where(same_head, _dot_tn(vu, ku), 0.0) for vu, ku in zip(t['v'], t['k_end'])]

    def inter():
        st_enter = []
        for di, d in enumerate(dirs):
            st = d['st'][...]
            for u in range(di * n_chunks, (di + 1) * n_chunks):
                st_enter.append(st.astype(BF16))
                st = st * jnp.exp(t['b_tot'][u]) + t['upd'][u]
            d['st'][...] = st
        for (d, ci), qu, su, ou in zip(units, t['q_in'], st_enter, t['o_intra']):
            d['o'][rows(ci), :] = _dot_nt(qu, su) + ou

    return [decay, scale, intra, inter]


def _stack_pair(q):
    left = lax.broadcasted_iota(jnp.int32, q.shape, 1) < HEAD_DIM
    zero = jnp.zeros((), q.dtype)
    return jnp.concatenate([jnp.where(left, q, zero), jnp.where(left, zero, q)], axis=0)


def _with_ones(v):
    left = lax.broadcasted_iota(jnp.int32, v.shape, 1) < HEAD_DIM
    one = jnp.ones((), v.dtype)
    return jnp.concatenate([jnp.where(left, v, one), jnp.where(left, one, v)], axis=1)


def _normalise(o, extra=None):
    denom = pltpu.roll(o, HEAD_DIM, 1)
    if extra is not None:
        denom = denom + extra
    return o / denom


def _softmax_weights(scores, extra_logit=None):
    m = functools.reduce(jnp.maximum, [jnp.max(s, axis=-1, keepdims=True) for s in scores])
    if extra_logit is not None:
        m = jnp.maximum(m, extra_logit)
    ps = [jnp.exp2(s - m).astype(BF16) for s in scores]
    return ps, (None if extra_logit is None else jnp.exp2(extra_logit - m))


def _na_stages(q_ref, k_ref, v_ref, bias_ref, o_ref, j, ctx_len, grid_rows, is_ctx):
    n_pairs = NA_W // PAIR_W
    rows_per_block = ROW_BLOCK // GRID_W
    n_loc = NA_KH * GRID_W
    unit = 2 * GRID_W
    pair_cols = [slice(p * PAIR_W, (p + 1) * PAIR_W) for p in range(n_pairs)]
    units = [(rr, p) for p in range(n_pairs) for rr in range(rows_per_block)]
    ctx_rows = lambda rr: slice(rr * unit, (rr + 1) * unit)
    t = {}

    def scores():
        qx = {(rr, p): _stack_pair(q_ref[rr * GRID_W:(rr + 1) * GRID_W, pair_cols[p]]) for rr, p in units}
        t['s_ctx'] = [_dot_nt(jnp.concatenate([qx[rr, p] for rr in range(rows_per_block)], axis=0),
                              k_ref[0:ctx_len, pair_cols[p]]) for p in range(n_pairs)]
        if is_ctx:
            return
        t['k0'], bias = [], {}
        for rr in range(rows_per_block):
            r = (j - 1) * rows_per_block + rr
            first_key_row = jnp.clip(r - NA_KH // 2, 0, grid_rows - NA_KH)
            t['k0'].append(pl.multiple_of(ctx_len + first_key_row * GRID_W, GRID_W))
            d0 = first_key_row - r + NA_KH - 1
            for p in range(n_pairs):
                bias[rr, p] = jnp.concatenate(
                    [jnp.concatenate([bias_ref[2 * p + side, d0 + 2 * m] for m in range(NA_KH // 2)], axis=1)
                     for side in (0, 1)], axis=0)
        t['s_loc'] = {(rr, p): _dot_nt(qx[rr, p], k_ref[pl.ds(t['k0'][rr], n_loc), pair_cols[p]]) + bias[rr, p]
                      for rr, p in units}

    def softmax():
        t['w'] = {(rr, p): _softmax_weights(([] if is_ctx else [t['s_loc'][rr, p]]) + [t['s_ctx'][p][ctx_rows(rr)]])
                  for rr, p in units}

    def values():
        w = t['w']
        left = lax.broadcasted_iota(jnp.int32, (GRID_W, PAIR_W), 1) < HEAD_DIM
        for p in range(n_pairs):
            o_ctx = _dot(jnp.concatenate([w[rr, p][0][-1] for rr in range(rows_per_block)], axis=0),
                         _with_ones(v_ref[0:ctx_len, pair_cols[p]]))
            for rr in range(rows_per_block):
                o = o_ctx[ctx_rows(rr)]
                if not is_ctx:
                    o = o + _dot(w[rr, p][0][0], _with_ones(v_ref[pl.ds(t['k0'][rr], n_loc), pair_cols[p]]))
                o_ref[rr * GRID_W:(rr + 1) * GRID_W, pair_cols[p]] = jnp.where(
                    left, _normalise(o[:GRID_W, :PAIR_W]), _normalise(o[GRID_W:, PAIR_W:])).astype(o_ref.dtype)

    return [scores, softmax, values]


def _swa_stages(sink, q_ref, k_ref, v_ref, o_ref, j, ctx_len, seq_len, is_ctx):
    tm = ROW_BLOCK
    n_pairs = SWA_W // PAIR_W
    n_loc = tm + 2 * SWA_WINDOW
    pair_cols = [slice(p * PAIR_W, (p + 1) * PAIR_W) for p in range(n_pairs)]
    lane = lax.broadcasted_iota(jnp.int32, (tm, PAIR_W), 1)
    zero = jnp.zeros((), BF16)
    units = [(p + side * n_pairs, p, side) for p in range(n_pairs) for side in (0, 1)]
    t = {}

    def scores():
        qx = [jnp.where((lane < HEAD_DIM) == (side == 0), q_ref[:, pair_cols[p]], zero) for _, p, side in units]
        k_ctx = k_ref[0:ctx_len, :]
        t['s'] = [[_dot_nt(q, k_ctx)] for q in qx]
        if is_ctx:
            return
        q0 = (j - 1) * tm
        k0 = jnp.clip(q0 - SWA_WINDOW, 0, seq_len - n_loc)
        qpos = q0 + lax.broadcasted_iota(jnp.int32, (tm, n_loc), 0)
        kpos = k0 + lax.broadcasted_iota(jnp.int32, (tm, n_loc), 1)
        in_window = jnp.abs(qpos - kpos) <= SWA_WINDOW
        t['start'] = pl.multiple_of(ctx_len + k0, SWA_WINDOW)
        k_loc = k_ref[pl.ds(t['start'], n_loc), :]
        for s, q in zip(t['s'], qx):
            s.insert(0, jnp.where(in_window, _dot_nt(q, k_loc), -jnp.inf))

    def softmax():
        t['w'] = [_softmax_weights(s, sink[h] * LOG2E) for s, (h, _, _) in zip(t['s'], units)]

    def values():
        vs = [v_ref[0:ctx_len, :]]
        if not is_ctx:
            vs.insert(0, v_ref[pl.ds(t['start'], n_loc), :])
        vs = [_with_ones(v) for v in vs]
        for p, cols in enumerate(pair_cols):
            (ps_a, extra_a), (ps_b, extra_b) = t['w'][2 * p], t['w'][2 * p + 1]
            o = functools.reduce(jnp.add, [_dot(jnp.concatenate([pa, pb], axis=0), v)
                                           for pa, pb, v in zip(ps_a, ps_b, vs)])
            o_ref[:, cols] = jnp.where(lane < HEAD_DIM, _normalise(o[:tm, :PAIR_W], extra_a),
                                       _normalise(o[tm:, PAIR_W:], extra_b)).astype(o_ref.dtype)

    return [scores, softmax, values]


def _mixers_kernel(sinks_ref, qf_ref, kf_ref, vf_ref, zf_ref, qb_ref, kb_ref, vb_ref, zb_ref,
                   waf_ref, wab_ref, baf_ref, bab_ref, nq_ref, nk_ref, nv_ref, nbias_ref, sq_ref, sk_ref, sv_ref,
                   of_ref, ob_ref, yna_ref, yswa_ref, stf_ref, stb_ref, *, layer, ctx_len, seq_len, ctx_out):
    j = pl.program_id(1)

    @pl.when(j == 0)
    def _():
        stf_ref[...] = jnp.zeros_like(stf_ref)
        stb_ref[...] = jnp.zeros_like(stb_ref)

    sink = [sinks_ref[layer, h] for h in range(SWA_HEADS)]

    def block(is_ctx):
        gla = _gla_stages(qf_ref, kf_ref, vf_ref, zf_ref, qb_ref, kb_ref, vb_ref, zb_ref,
                          waf_ref, wab_ref, baf_ref, bab_ref, of_ref, ob_ref, stf_ref, stb_ref)
        if is_ctx and not ctx_out:
            order = gla
        else:
            na = _na_stages(nq_ref, nk_ref, nv_ref, nbias_ref, yna_ref, j, ctx_len, seq_len // GRID_W, is_ctx)
            swa = _swa_stages(sink, sq_ref, sk_ref, sv_ref, yswa_ref, j, ctx_len, seq_len, is_ctx)
            order = [gla[0], na[0], swa[0], gla[1], gla[2], na[1], na[2], swa[1], swa[2], gla[3]]
        for stage in order:
            stage()

    pl.when(j == 0)(functools.partial(block, True))
    pl.when(j > 0)(functools.partial(block, False))


def _mixers(gla_in, za, na_qkv, sq, sk, sv, layer, wa_f, wa_b, ba_f, ba_b, na_bias, sinks, ctx_len, ctx_out):
    batch, s, _ = gla_in.shape
    tm = ROW_BLOCK
    nblk = s // tm
    first = 0 if ctx_out else 1
    fwd = lambda j: j
    bwd = lambda j: jnp.where(j == 0, 0, nblk - j)
    soft_out = lambda j: jnp.maximum(j - first, 0)

    def col(order, cb):
        return pl.BlockSpec((None, tm, GLA_W), lambda b, j: (b, order(j), cb))

    def rows(order, w):
        return pl.BlockSpec((None, tm, w), lambda b, j: (b, order(j), 0))

    def full(w, cb=0):
        return pl.BlockSpec((None, s, w), lambda b, j: (b, 0, cb))

    return pl.pallas_call(
        functools.partial(_mixers_kernel, layer=layer, ctx_len=ctx_len, seq_len=s - ctx_len, ctx_out=ctx_out),
        grid=(batch, nblk),
        in_specs=[pl.BlockSpec(memory_space=pltpu.SMEM),
                  col(fwd, 0), col(fwd, 1), col(fwd, 2), rows(fwd, LANES),
                  col(bwd, 0), col(bwd, 1), col(bwd, 2), rows(bwd, LANES),
                  _layer_spec(layer, (LANES, GLA_W)), _layer_spec(layer, (LANES, GLA_W)),
                  _layer_spec(layer, (1, GLA_W)), _layer_spec(layer, (1, GLA_W)),
                  rows(fwd, NA_W), full(NA_W, 1), full(NA_W, 2), _layer_spec(layer, na_bias.shape[1:]),
                  rows(fwd, SWA_W), full(SWA_KV_W), full(SWA_KV_W)],
        out_specs=[col(fwd, 0), col(bwd, 0), rows(soft_out, NA_W), rows(soft_out, SWA_W)],
        out_shape=[jax.ShapeDtypeStruct((batch, s, GLA_W), F32)] * 2
        + [jax.ShapeDtypeStruct((batch, s - first * tm, w), BF16) for w in (NA_W, SWA_W)],
        scratch_shapes=[pltpu.VMEM((GLA_W, GLA_W), F32)] * 2,
        compiler_params=_params(1, 1),
    )(sinks, gla_in, gla_in, gla_in, za, gla_in, gla_in, gla_in, za, wa_f, wa_b, ba_f, ba_b,
      na_qkv, na_qkv, na_qkv, na_bias, sq, sk, sv)


def _out_ffn_kernel(*refs, n_src, final):
    src_refs = refs[:n_src]
    (of_ref, ob_ref, gate_ref, na_ref, swa_ref, g1_ref, sh2_ref, sc2_ref, g2_ref, nffn_ref, gn_ref, avg_ref,
     wo_ref, wgate_ref, wup_ref, wdown_ref, fn_ref, o_ref) = refs[n_src:]
    x_in = _token_rows(src_refs)
    tm = x_in.shape[0]
    avg = avg_ref[...]
    halves = (slice(0, tm // 2), slice(tm // 2, tm))
    o = [of_ref[rows, :] + ob_ref[rows, :] for rows in halves]
    sq = [_split2(oh * oh) for oh in o]
    ms = [_dot(hi, avg) + _dot(lo, avg) for hi, lo in sq]
    y = [(oh * lax.rsqrt(msh + NORM_EPS) * gn_ref[...] * _silu(gate_ref[rows, :])).astype(BF16)
         for oh, msh, rows in zip(o, ms, halves)]
    mix = [_dot(jnp.concatenate([yh, na_ref[rows, :], swa_ref[rows, :]], axis=1), wo_ref[...])
           for yh, rows in zip(y, halves)]
    x1 = jnp.concatenate([x_in[rows] + g1_ref[...] * mh for mh, rows in zip(mix, halves)], axis=0)
    h = _rms(x1) * nffn_ref[...]
    h = (h * (1.0 + sc2_ref[...]) + sh2_ref[...]).astype(BF16)

    d_ff = wgate_ref.shape[1]
    chunks = [slice(c0, min(c0 + FF_CHUNK, d_ff)) for c0 in range(0, d_ff, FF_CHUNK)]
    acc = jnp.zeros(x1.shape, F32)
    gu = {}
    for t in range(len(chunks) + 1):
        if t < len(chunks):
            gu[t] = (_dot(h, wgate_ref[:, chunks[t]]), _dot(h, wup_ref[:, chunks[t]]))
        if t >= 1:
            g, u = gu.pop(t - 1)
            acc = acc + _dot((_silu(g) * u).astype(BF16), wdown_ref[chunks[t - 1], :])
    x2 = x1 + g2_ref[...] * acc
    if final:
        x2 = _rms(x2) * fn_ref[...]
    o_ref[...] = x2


def _out_ffn(srcs, o_f, o_b, gla_in, y_na, y_swa, layer, mods, norm_ffn, gla_norm, avg, w_o,
             w_gate, w_up, w_down, final_norm, skip_ctx, final):
    batch, s, _ = o_f.shape
    d = srcs[0].shape[2]
    d_ff = w_gate.shape[2]
    tm = ROW_BLOCK
    off = 1 if skip_ctx else 0
    nblk = s // tm - off
    tok_off = off
    if skip_ctx and len(srcs) == 2:
        srcs, tok_off = srcs[1:], 0
    row = lambda w, cb=0: pl.BlockSpec((None, tm, w), lambda b, j: (b, j + off, cb))
    out = lambda w: pl.BlockSpec((None, tm, w), lambda b, j: (b, j, 0))
    ms = lambda chunk: _mod_spec(layer, chunk, batch, not skip_ctx)
    return pl.pallas_call(
        functools.partial(_out_ffn_kernel, n_src=len(srcs), final=final),
        grid=(batch, nblk),
        in_specs=_token_specs(srcs, tm, tok_off) + [
            row(GLA_W), row(GLA_W), row(GLA_W, 3), out(NA_W), out(SWA_W),
            ms(2), ms(3), ms(4), ms(5), _layer_spec(layer, (1, d)), _layer_spec(layer, (1, GLA_W)),
            pl.BlockSpec((GLA_W, GLA_W), lambda b, j: (0, 0)),
            _layer_spec(layer, (GLA_W + NA_W + SWA_W, d)),
            _layer_spec(layer, (d, d_ff)), _layer_spec(layer, (d, d_ff)), _layer_spec(layer, (d_ff, d)),
            pl.BlockSpec((1, d), lambda b, j: (0, 0))],
        out_specs=out(d),
        out_shape=jax.ShapeDtypeStruct((batch, nblk * tm, d), F32),
        compiler_params=_params(2),
    )(*srcs, o_f, o_b, gla_in, y_na, y_swa, mods, mods, mods, mods, norm_ffn, gla_norm, avg, w_o,
      w_gate, w_up, w_down, final_norm)


def _rope_tables(ctx_len, seq_len):
    t = np.arange(seq_len)
    n_freq = HEAD_DIM // 4
    inv_freq = ROPE_THETA ** (-np.arange(n_freq) / n_freq)
    ang = np.concatenate([(t // GRID_W)[:, None] * inv_freq, (t % GRID_W)[:, None] * inv_freq], axis=-1)
    cos_h = np.concatenate([np.cos(ang), np.cos(ang)], axis=-1)
    sin_h = np.concatenate([-np.sin(ang), np.sin(ang)], axis=-1)
    cos_t = np.concatenate([np.ones((ctx_len, HEAD_DIM)), cos_h], axis=0)
    sin_t = np.concatenate([np.zeros((ctx_len, HEAD_DIM)), sin_h], axis=0)
    return jnp.asarray(np.tile(cos_t, (1, 2)), F32), jnp.asarray(np.tile(sin_t, (1, 2)), F32)


def _swa_pair_order(a, axis):
    n_pairs = SWA_W // PAIR_W
    heads = [h for p in range(n_pairs) for h in (p, p + n_pairs)]
    return jnp.concatenate([lax.slice_in_dim(a, h * HEAD_DIM, (h + 1) * HEAD_DIM, axis=axis) for h in heads],
                           axis=axis)


def _permute_w_in_kernel(w_ref, o_ref):
    scale = HEAD_DIM ** -0.5
    scale2 = scale * LOG2E
    o = np.cumsum([0, GLA_W, GLA_W, GLA_W, GLA_W, GLA_RANK, GLA_RANK, NA_W, NA_W, NA_W, SWA_W, SWA_KV_W, SWA_KV_W])
    n_pairs = SWA_W // PAIR_W
    swa_heads = [h for p in range(n_pairs) for h in (p, p + n_pairs)]
    moves = [(o[0], o[1], C_GLA, scale), (o[1], o[4], C_GLA + GLA_W, 1.0), (o[4], o[6], C_ZA, 1.0),
             (o[6], o[7], C_NA, scale2), (o[7], o[9], C_NA + NA_W, 1.0), (o[10], o[11], C_SK, 1.0),
             (o[11], o[12], C_SV, 1.0)]
    moves += [(o[9] + h * HEAD_DIM, o[9] + (h + 1) * HEAD_DIM, C_SQ + i * HEAD_DIM, scale2)
              for i, h in enumerate(swa_heads)]
    o_ref[:, C_ZA:IN_PERM_W] = jnp.zeros((o_ref.shape[0], LANES), o_ref.dtype)
    for lo, hi, dst, mult in moves:
        o_ref[:, dst:dst + (hi - lo)] = (w_ref[:, lo:hi] * mult).astype(o_ref.dtype)


def _permute_w_in(w):
    depth, d, n = w.shape
    tm = ROW_BLOCK
    return pl.pallas_call(
        _permute_w_in_kernel,
        grid=(depth, d // tm),
        in_specs=[pl.BlockSpec((None, tm, n), lambda i, k: (i, k, 0))],
        out_specs=pl.BlockSpec((None, tm, IN_PERM_W), lambda i, k: (i, k, 0)),
        out_shape=jax.ShapeDtypeStruct((depth, d, IN_PERM_W), BF16),
        compiler_params=_params(2),
    )(w)


def _na_bias_table(rpb):
    qc = np.arange(GRID_W)
    kc = np.arange(GRID_W)
    win = np.clip(qc - NA_KW // 2, 0, GRID_W - NA_KW)
    valid = (kc[None, :] >= win[:, None]) & (kc[None, :] < win[:, None] + NA_KW)
    dc = kc[None, :] - qc[:, None] + NA_KW - 1
    pick_col = ((dc[None] == np.arange(2 * NA_KW - 1)[:, None, None]) & valid[None]).astype(np.float32)
    t = jnp.einsum('lhde,eck->lhdck', rpb.astype(F32) * LOG2E, pick_col, precision=lax.Precision.HIGHEST)
    t = jnp.where(valid, t, -jnp.inf)
    return jnp.concatenate([t[:, :, :-1], t[:, :, 1:]], axis=-1)


def _pad_rank(wa2, offset):
    return jnp.pad(wa2, ((0, 0), (offset, LANES - offset - GLA_RANK), (0, 0)))


def kernel(x, c, ctx, c_ctx, w_mod, b_mod, norm_mix, norm_ffn, w_in, gla_wa2_f, gla_ba_f, gla_wa2_b, gla_ba_b,
           gla_norm, na_rpb, swa_sink, w_out, w_gate, w_up, w_down, final_norm):
    batch, seq_len, d = x.shape
    ctx_len = ctx.shape[1]
    depth = w_mod.shape[0]
    assert d == D_MODEL and ctx_len == ROW_BLOCK and seq_len % ROW_BLOCK == 0 and batch < 8
    assert seq_len % GRID_W == 0 and seq_len // GRID_W >= 2 * NA_KH

    cc = jnp.concatenate([c, c_ctx[None, :], jnp.zeros((8 - batch - 1, d), F32)], axis=0)
    mods = _modulation(cc, w_mod, b_mod).reshape(depth, 8, 1, 6 * d)
    cos_t, sin_t = _rope_tables(ctx_len, seq_len)
    head_avg = jnp.asarray(np.kron(np.eye(GLA_HEADS), np.full((HEAD_DIM, HEAD_DIM), 1.0 / HEAD_DIM)), BF16)
    w_in_p = _permute_w_in(w_in)
    wa_f, wa_b = _pad_rank(gla_wa2_f, 0), _pad_rank(gla_wa2_b, GLA_RANK)
    ba_f, ba_b = gla_ba_f.reshape(depth, 1, GLA_W), gla_ba_b.reshape(depth, 1, GLA_W)
    na_bias = _na_bias_table(na_rpb)
    w_o = jnp.concatenate([w_out[:, :GLA_W + NA_W], _swa_pair_order(w_out[:, GLA_W + NA_W:], 1)],
                          axis=1).astype(BF16)
    w_gate_b, w_up_b, w_down_b = w_gate.astype(BF16), w_up.astype(BF16), w_down.astype(BF16)
    norm_mix3, norm_ffn3 = norm_mix.reshape(depth, 1, d), norm_ffn.reshape(depth, 1, d)
    gla_norm3 = gla_norm.reshape(depth, 1, GLA_W)

    srcs = (ctx, x)
    for i in range(depth):
        last = i == depth - 1
        gla_in, za, na_qkv, sq, sk, sv = _in_proj(srcs, i, mods, norm_mix3, w_in_p, cos_t, sin_t)
        o_f, o_b, y_na, y_swa = _mixers(gla_in, za, na_qkv, sq, sk, sv, i, wa_f, wa_b, ba_f, ba_b, na_bias,
                                        swa_sink, ctx_len, ctx_out=not last)
        srcs = (_out_ffn(srcs, o_f, o_b, gla_in, y_na, y_swa, i, mods, norm_ffn3, gla_norm3, head_avg,
                         w_o, w_gate_b, w_up_b, w_down_b, final_norm.reshape(1, d),
                         skip_ctx=last, final=last),)
    return srcs[0]
```

```python
import functools

import jax
import jax.numpy as jnp
import numpy as np
from jax import lax
from jax.experimental import pallas as pl
from jax.experimental.pallas import tpu as pltpu

F32 = jnp.float32
BF16 = jnp.bfloat16

D_MODEL = 1024
HEAD_DIM = 64
GRID_W = 64
GLA_HEADS = 4
NA_HEADS = 6
SWA_HEADS = 6
SWA_KV_HEADS = 2
GLA_W = GLA_HEADS * HEAD_DIM
NA_W = NA_HEADS * HEAD_DIM
SWA_W = SWA_HEADS * HEAD_DIM
SWA_KV_W = SWA_KV_HEADS * HEAD_DIM
GLA_RANK = 16
GLA_TAU = 16.0
GLA_CHUNK = 64
NA_KH = 8
NA_KW = 16
SWA_WINDOW = 128
ROPE_THETA = 10000.0
NORM_EPS = 1e-6

LANES = 128
ROW_BLOCK = 256
PAIR_W = 2 * HEAD_DIM
LOG2E = float(np.log2(np.e))
FF_CHUNK = 256
MOD_BLOCK_COLS = 1536
V7X_VMEM_BYTES = 64 * 1024 * 1024
VMEM_LIMIT = V7X_VMEM_BYTES * 7 // 8

C_GLA = 0
C_NA = C_GLA + 4 * GLA_W
C_SV = C_NA + 3 * NA_W
C_SQ = C_SV + SWA_KV_W
C_SK = C_SQ + SWA_W
C_ZA = C_SK + SWA_KV_W
IN_PERM_W = C_ZA + LANES


def _dot(a, b):
    return jnp.dot(a, b, preferred_element_type=F32)


def _dot_nt(a, b):
    return lax.dot_general(a, b, (((1,), (1,)), ((), ())), preferred_element_type=F32)


def _dot_tn(a, b):
    return lax.dot_general(a, b, (((0,), (0,)), ((), ())), preferred_element_type=F32)


def _idiv(a, n):
    assert n & (n - 1) == 0
    return a >> (n.bit_length() - 1)


def _imod(a, n):
    assert n & (n - 1) == 0
    return a & (n - 1)


def _split2(a):
    hi = a.astype(BF16)
    lo = (a - hi.astype(F32)).astype(BF16)
    return hi, lo


def _split3(a):
    hi = a.astype(BF16)
    r = a - hi.astype(F32)
    mid = r.astype(BF16)
    lo = (r - mid.astype(F32)).astype(BF16)
    return hi, mid, lo


def _dot_split(a, b, merged=False):
    ah, al = _split2(a)
    bh, bl = _split2(b)
    if merged:
        return _dot(jnp.concatenate([ah, al, ah], axis=1), jnp.concatenate([bh, bh, bl], axis=0))
    return _dot(ah, bh) + _dot(al, bh) + _dot(ah, bl)


def _silu(a):
    return a * jax.nn.sigmoid(a)


def _rms(x):
    return x * lax.rsqrt(jnp.mean(x * x, axis=-1, keepdims=True) + NORM_EPS)


def _params(n_parallel, n_arbitrary=0):
    return pltpu.CompilerParams(
        dimension_semantics=("parallel",) * n_parallel + ("arbitrary",) * n_arbitrary,
        vmem_limit_bytes=VMEM_LIMIT)


def _mod_kernel(c_ref, w_ref, b_ref, o_ref):
    o_ref[...] = _dot_split(_silu(c_ref[...]), w_ref[...]) + b_ref[...]


def _modulation(cc, w_mod, b_mod):
    depth, d, n = w_mod.shape
    tn = MOD_BLOCK_COLS
    assert n % tn == 0
    return pl.pallas_call(
        _mod_kernel,
        grid=(depth, n // tn),
        in_specs=[
            pl.BlockSpec((8, d), lambda i, k: (0, 0)),
            pl.BlockSpec((None, d, tn), lambda i, k: (i, 0, k)),
            pl.BlockSpec((None, 1, tn), lambda i, k: (i, 0, k)),
        ],
        out_specs=pl.BlockSpec((None, 8, tn), lambda i, k: (i, 0, k)),
        out_shape=jax.ShapeDtypeStruct((depth, 8, n), F32),
        compiler_params=_params(2),
    )(cc, w_mod, b_mod.reshape(depth, 1, n))


def _mod_spec(layer, chunk, batch, ctx_first):
    if ctx_first:
        return pl.BlockSpec((None, None, 1, D_MODEL), lambda b, j: (layer, jnp.where(j == 0, batch, b), 0, chunk))
    return pl.BlockSpec((None, None, 1, D_MODEL), lambda b, j: (layer, b, 0, chunk))


def _layer_spec(layer, shape):
    return pl.BlockSpec((None,) + shape, lambda b, j: (layer,) + (0,) * len(shape))


def _token_rows(src_refs):
    if len(src_refs) == 1:
        return src_refs[0][...]
    ctx_ref, x_ref = src_refs
    return jnp.where(pl.program_id(1) == 0, ctx_ref[...], x_ref[...])


def _token_specs(srcs, tm, off=0):
    d = srcs[0].shape[-1]
    if len(srcs) == 1:
        return [pl.BlockSpec((None, tm, d), lambda b, j: (b, j + off, 0))]
    assert off == 0
    return [pl.BlockSpec((None, tm, d), lambda b, j: (b, 0, 0)),
            pl.BlockSpec((None, tm, d), lambda b, j: (b, jnp.maximum(j - 1, 0), 0))]


def _in_proj_kernel(*refs, n_src):
    src_refs = refs[:n_src]
    sh_ref, sc_ref, nw_ref, w_ref, cos_ref, sin_ref, gla_ref, za_ref, na_ref, sq_ref, sk_ref, sv_ref = refs[n_src:]
    x = _token_rows(src_refs)
    tm = x.shape[0]
    lane = lax.broadcasted_iota(jnp.int32, (tm // 2, LANES), 1)
    first_half = _imod(lane, HEAD_DIM) < (HEAD_DIM // 2)
    n_q = SWA_W // LANES
    for rows in (slice(0, tm // 2), slice(tm // 2, tm)):
        h = _rms(x[rows]) * nw_ref[...]
        h = (h * (1.0 + sc_ref[...]) + sh_ref[...]).astype(BF16)
        qkz = _dot(h, w_ref[:, C_SQ:IN_PERM_W])
        qk = qkz[:, :C_ZA - C_SQ]
        cos = cos_ref[rows, :]
        sin = sin_ref[rows, :]
        for cb in range(n_q + 1):
            a = qk[:, cb * LANES:(cb + 1) * LANES]
            partner = jnp.where(first_half, pltpu.roll(a, LANES - HEAD_DIM // 2, 1), pltpu.roll(a, HEAD_DIM // 2, 1))
            r = (a * cos + partner * sin).astype(BF16)
            if cb < n_q:
                sq_ref[rows, cb * LANES:(cb + 1) * LANES] = r
            else:
                sk_ref[rows, :] = r
        za_ref[rows, :] = qkz[:, C_ZA - C_SQ:]
        gla_ref[rows, :] = _dot(h, w_ref[:, C_GLA:C_NA])
        nv = _dot(h, w_ref[:, C_NA:C_SQ]).astype(BF16)
        na_ref[rows, :] = nv[:, :C_SV - C_NA]
        sv_ref[rows, :] = nv[:, C_SV - C_NA:]


def _in_proj(srcs, layer, mods, norm_w, w_perm, cos_t, sin_t):
    batch, d = srcs[0].shape[0], srcs[0].shape[2]
    s = sum(a.shape[1] for a in srcs)
    tm = ROW_BLOCK
    row = lambda w: pl.BlockSpec((None, tm, w), lambda b, j: (b, j, 0))
    tab = pl.BlockSpec((tm, LANES), lambda b, j: (j, 0))
    return pl.pallas_call(
        functools.partial(_in_proj_kernel, n_src=len(srcs)),
        grid=(batch, s // tm),
        in_specs=_token_specs(srcs, tm) + [
            _mod_spec(layer, 0, batch, True), _mod_spec(layer, 1, batch, True), _layer_spec(layer, (1, d)),
            _layer_spec(layer, (d, IN_PERM_W)), tab, tab],
        out_specs=[row(4 * GLA_W), row(LANES), row(3 * NA_W), row(SWA_W), row(SWA_KV_W), row(SWA_KV_W)],
        out_shape=[
            jax.ShapeDtypeStruct((batch, s, 4 * GLA_W), F32),
            jax.ShapeDtypeStruct((batch, s, LANES), F32),
            jax.ShapeDtypeStruct((batch, s, 3 * NA_W), BF16),
            jax.ShapeDtypeStruct((batch, s, SWA_W), BF16),
            jax.ShapeDtypeStruct((batch, s, SWA_KV_W), BF16),
            jax.ShapeDtypeStruct((batch, s, SWA_KV_W), BF16),
        ],
        compiler_params=_params(2),
    )(*srcs, mods, mods, norm_w, w_perm, cos_t, sin_t)


def _gla_stages(qf_ref, kf_ref, vf_ref, zf_ref, qb_ref, kb_ref, vb_ref, zb_ref,
                waf_ref, wab_ref, baf_ref, bab_ref, of_ref, ob_ref, stf_ref, stb_ref):
    c = GLA_CHUNK
    w = GLA_W
    n_chunks = ROW_BLOCK // c
    r64 = lax.broadcasted_iota(jnp.int32, (c, 3 * c), 0)
    c64 = _imod(lax.broadcasted_iota(jnp.int32, (c, 3 * c), 1), c)
    row_h = _idiv(lax.broadcasted_iota(jnp.int32, (GLA_HEADS * c, w), 0), c)
    col_h = _idiv(lax.broadcasted_iota(jnp.int32, (GLA_HEADS * c, w), 1), c)
    same_head = row_h == col_h
    ar = lax.broadcasted_iota(jnp.int32, (c, w), 0)
    ac = _imod(lax.broadcasted_iota(jnp.int32, (c, w), 1), c)
    dirs = [
        dict(q=qf_ref, k=kf_ref, v=vf_ref, z=zf_ref, wa=waf_ref, ba=baf_ref, o=of_ref, st=stf_ref, rev=False,
             order=range(n_chunks), cum=(r64 >= c64).astype(BF16), causal=ar >= ac),
        dict(q=qb_ref, k=kb_ref, v=vb_ref, z=zb_ref, wa=wab_ref, ba=bab_ref, o=ob_ref, st=stb_ref, rev=True,
             order=range(n_chunks - 1, -1, -1), cum=(r64 <= c64).astype(BF16), causal=ar <= ac),
    ]
    units = [(d, ci) for d in dirs for ci in d['order']]
    rows = lambda ci: slice(ci * c, (ci + 1) * c)
    t = {}

    def decay():
        log_a = []
        for d in dirs:
            z = _dot_split(d['z'][...], d['wa'][...], merged=True) + d['ba'][...]
            log_a.append((jnp.minimum(z, 0.0) - jnp.log1p(jnp.exp(-jnp.abs(z)))) * (1.0 / GLA_TAU))
        t['b'], t['b_tot'] = [], []
        for (d, ci), la in zip(units, [la for la in log_a for _ in range(n_chunks)]):
            t['b'].append(_dot(d['cum'], jnp.concatenate(_split3(la[rows(ci)]), axis=0)))
            t['b_tot'].append(t['b'][-1][0:1] if d['rev'] else t['b'][-1][c - 1:c])

    def scale():
        b, b_tot = t['b'], t['b_tot']
        k = [d['k'][rows(ci), :] for d, ci in units]
        t['v'] = [d['v'][rows(ci), :].astype(BF16) for d, ci in units]
        t['q_in'] = [(d['q'][rows(ci), :] * jnp.exp(bu)).astype(BF16) for (d, ci), bu in zip(units, b)]
        t['k_in'] = [(ku * jnp.exp(-bu)).astype(BF16) for ku, bu in zip(k, b)]
        t['k_end'] = [(ku * jnp.exp(bt - bu)).astype(BF16) for ku, bu, bt in zip(k, b, b_tot)]
        expand = lambda a: jnp.where(same_head, jnp.concatenate([a] * GLA_HEADS, axis=0), jnp.zeros((), BF16))
        t['kx'] = [expand(ku) for ku in t['k_in']]
        t['vx'] = [expand(vu) for vu in t['v']]

    def intra():
        a = [jnp.where(d['causal'], _dot_nt(qu, kxu), 0.0).astype(BF16)
             for (d, _), qu, kxu in zip(units, t['q_in'], t['kx'])]
        t['o_intra'] = [_dot(au, vxu) for au, vxu in zip(a, t['vx'])]
        t['upd'] = [jnp.where(same_head, _dot_tn(vu, ku), 0.0) for vu, ku in zip(t['v'], t['k_end'])]

    def inter():
        st_enter = []
        for di, d in enumerate(dirs):
            st = d['st'][...]
            for u in range(di * n_chunks, (di + 1) * n_chunks):
                st_enter.append(st.astype(BF16))
                st = st * jnp.exp(t['b_tot'][u]) + t['upd'][u]
            d['st'][...] = st
        for (d, ci), qu, su, ou in zip(units, t['q_in'], st_enter, t['o_intra']):
            d['o'][rows(ci), :] = _dot_nt(qu, su) + ou

    return [decay, scale, intra, inter]


def _stack_pair(q):
    left = lax.broadcasted_iota(jnp.int32, q.shape, 1) < HEAD_DIM
    zero = jnp.zeros((), q.dtype)
    return jnp.concatenate([jnp.where(left, q, zero), jnp.where(left, zero, q)], axis=0)


def _with_ones(v):
    left = lax.broadcasted_iota(jnp.int32, v.shape, 1) < HEAD_DIM
    one = jnp.ones((), v.dtype)
    return jnp.concatenate([jnp.where(left, v, one), jnp.where(left, one, v)], axis=1)


def _normalise(o, extra=None):
    denom = pltpu.roll(o, HEAD_DIM, 1)
    if extra is not None:
        denom = denom + extra
    return o / denom


def _softmax_weights(scores, extra_logit=None):
    m = functools.reduce(jnp.maximum, [jnp.max(s, axis=-1, keepdims=True) for s in scores])
    if extra_logit is not None:
        m = jnp.maximum(m, extra_logit)
    ps = [jnp.exp2(s - m).astype(BF16) for s in scores]
    return ps, (None if extra_logit is None else jnp.exp2(extra_logit - m))


def _na_stages(q_ref, k_ref, v_ref, bias_ref, o_ref, j, ctx_len, grid_rows, is_ctx):
    n_pairs = NA_W // PAIR_W
    rows_per_block = ROW_BLOCK // GRID_W
    n_loc = NA_KH * GRID_W
    unit = 2 * GRID_W
    pair_cols = [slice(p * PAIR_W, (p + 1) * PAIR_W) for p in range(n_pairs)]
    units = [(rr, p) for p in range(n_pairs) for rr in range(rows_per_block)]
    ctx_rows = lambda rr: slice(rr * unit, (rr + 1) * unit)
    t = {}

    def scores():
        qx = {(rr, p): _stack_pair(q_ref[rr * GRID_W:(rr + 1) * GRID_W, pair_cols[p]]) for rr, p in units}
        t['s_ctx'] = [_dot_nt(jnp.concatenate([qx[rr, p] for rr in range(rows_per_block)], axis=0),
                              k_ref[0:ctx_len, pair_cols[p]]) for p in range(n_pairs)]
        if is_ctx:
            return
        t['k0'], bias = [], {}
        for rr in range(rows_per_block):
            r = (j - 1) * rows_per_block + rr
            first_key_row = jnp.clip(r - NA_KH // 2, 0, grid_rows - NA_KH)
            t['k0'].append(pl.multiple_of(ctx_len + first_key_row * GRID_W, GRID_W))
            d0 = first_key_row - r + NA_KH - 1
            for p in range(n_pairs):
                bias[rr, p] = jnp.concatenate(
                    [jnp.concatenate([bias_ref[2 * p + side, d0 + 2 * m] for m in range(NA_KH // 2)], axis=1)
                     for side in (0, 1)], axis=0)
        t['s_loc'] = {(rr, p): _dot_nt(qx[rr, p], k_ref[pl.ds(t['k0'][rr], n_loc), pair_cols[p]]) + bias[rr, p]
                      for rr, p in units}

    def softmax():
        t['w'] = {(rr, p): _softmax_weights(([] if is_ctx else [t['s_loc'][rr, p]]) + [t['s_ctx'][p][ctx_rows(rr)]])
                  for rr, p in units}

    def values():
        w = t['w']
        left = lax.broadcasted_iota(jnp.int32, (GRID_W, PAIR_W), 1) < HEAD_DIM
        for p in range(n_pairs):
            o_ctx = _dot(jnp.concatenate([w[rr, p][0][-1] for rr in range(rows_per_block)], axis=0),
                         _with_ones(v_ref[0:ctx_len, pair_cols[p]]))
            for rr in range(rows_per_block):
                o = o_ctx[ctx_rows(rr)]
                if not is_ctx:
                    o = o + _dot(w[rr, p][0][0], _with_ones(v_ref[pl.ds(t['k0'][rr], n_loc), pair_cols[p]]))
                o_ref[rr * GRID_W:(rr + 1) * GRID_W, pair_cols[p]] = jnp.where(
                    left, _normalise(o[:GRID_W, :PAIR_W]), _normalise(o[GRID_W:, PAIR_W:])).astype(o_ref.dtype)

    return [scores, softmax, values]


def _swa_stages(sink, q_ref, k_ref, v_ref, o_ref, j, ctx_len, seq_len, is_ctx):
    tm = ROW_BLOCK
    n_pairs = SWA_W // PAIR_W
    n_loc = tm + 2 * SWA_WINDOW
    pair_cols = [slice(p * PAIR_W, (p + 1) * PAIR_W) for p in range(n_pairs)]
    lane = lax.broadcasted_iota(jnp.int32, (tm, PAIR_W), 1)
    zero = jnp.zeros((), BF16)
    units = [(p + side * n_pairs, p, side) for p in range(n_pairs) for side in (0, 1)]
    t = {}

    def scores():
        qx = [jnp.where((lane < HEAD_DIM) == (side == 0), q_ref[:, pair_cols[p]], zero) for _, p, side in units]
        k_ctx = k_ref[0:ctx_len, :]
        t['s'] = [[_dot_nt(q, k_ctx)] for q in qx]
        if is_ctx:
            return
        q0 = (j - 1) * tm
        k0 = jnp.clip(q0 - SWA_WINDOW, 0, seq_len - n_loc)
        qpos = q0 + lax.broadcasted_iota(jnp.int32, (tm, n_loc), 0)
        kpos = k0 + lax.broadcasted_iota(jnp.int32, (tm, n_loc), 1)
        in_window = jnp.abs(qpos - kpos) <= SWA_WINDOW
        t['start'] = pl.multiple_of(ctx_len + k0, SWA_WINDOW)
        k_loc = k_ref[pl.ds(t['start'], n_loc), :]
        for s, q in zip(t['s'], qx):
            s.insert(0, jnp.where(in_window, _dot_nt(q, k_loc), -jnp.inf))

    def softmax():
        t['w'] = [_softmax_weights(s, sink[h] * LOG2E) for s, (h, _, _) in zip(t['s'], units)]

    def values():
        vs = [v_ref[0:ctx_len, :]]
        if not is_ctx:
            vs.insert(0, v_ref[pl.ds(t['start'], n_loc), :])
        vs = [_with_ones(v) for v in vs]
        for p, cols in enumerate(pair_cols):
            (ps_a, extra_a), (ps_b, extra_b) = t['w'][2 * p], t['w'][2 * p + 1]
            o = functools.reduce(jnp.add, [_dot(jnp.concatenate([pa, pb], axis=0), v)
                                           for pa, pb, v in zip(ps_a, ps_b, vs)])
            o_ref[:, cols] = jnp.where(lane < HEAD_DIM, _normalise(o[:tm, :PAIR_W], extra_a),
                                       _normalise(o[tm:, PAIR_W:], extra_b)).astype(o_ref.dtype)

    return [scores, softmax, values]


def _mixers_kernel(sinks_ref, qf_ref, kf_ref, vf_ref, zf_ref, qb_ref, kb_ref, vb_ref, zb_ref,
                   waf_ref, wab_ref, baf_ref, bab_ref, nq_ref, nk_ref, nv_ref, nbias_ref, sq_ref, sk_ref, sv_ref,
                   of_ref, ob_ref, yna_ref, yswa_ref, stf_ref, stb_ref, *, layer, ctx_len, seq_len, ctx_out):
    j = pl.program_id(1)

    @pl.when(j == 0)
    def _():
        stf_ref[...] = jnp.zeros_like(stf_ref)
        stb_ref[...] = jnp.zeros_like(stb_ref)

    sink = [sinks_ref[layer, h] for h in range(SWA_HEADS)]

    def block(is_ctx):
        gla = _gla_stages(qf_ref, kf_ref, vf_ref, zf_ref, qb_ref, kb_ref, vb_ref, zb_ref,
                          waf_ref, wab_ref, baf_ref, bab_ref, of_ref, ob_ref, stf_ref, stb_ref)
        if is_ctx and not ctx_out:
            order = gla
        else:
            na = _na_stages(nq_ref, nk_ref, nv_ref, nbias_ref, yna_ref, j, ctx_len, seq_len // GRID_W, is_ctx)
            swa = _swa_stages(sink, sq_ref, sk_ref, sv_ref, yswa_ref, j, ctx_len, seq_len, is_ctx)
            order = [gla[0], na[0], swa[0], gla[1], gla[2], na[1], na[2], swa[1], swa[2], gla[3]]
        for stage in order:
            stage()

    pl.when(j == 0)(functools.partial(block, True))
    pl.when(j > 0)(functools.partial(block, False))


def _mixers(gla_in, za, na_qkv, sq, sk, sv, layer, wa_f, wa_b, ba_f, ba_b, na_bias, sinks, ctx_len, ctx_out):
    batch, s, _ = gla_in.shape
    tm = ROW_BLOCK
    nblk = s // tm
    first = 0 if ctx_out else 1
    fwd = lambda j: j
    bwd = lambda j: jnp.where(j == 0, 0, nblk - j)
    soft_out = lambda j: jnp.maximum(j - first, 0)

    def col(order, cb):
        return pl.BlockSpec((None, tm, GLA_W), lambda b, j: (b, order(j), cb))

    def rows(order, w):
        return pl.BlockSpec((None, tm, w), lambda b, j: (b, order(j), 0))

    def full(w, cb=0):
        return pl.BlockSpec((None, s, w), lambda b, j: (b, 0, cb))

    return pl.pallas_call(
        functools.partial(_mixers_kernel, layer=layer, ctx_len=ctx_len, seq_len=s - ctx_len, ctx_out=ctx_out),
        grid=(batch, nblk),
        in_specs=[pl.BlockSpec(memory_space=pltpu.SMEM),
                  col(fwd, 0), col(fwd, 1), col(fwd, 2), rows(fwd, LANES),
                  col(bwd, 0), col(bwd, 1), col(bwd, 2), rows(bwd, LANES),
                  _layer_spec(layer, (LANES, GLA_W)), _layer_spec(layer, (LANES, GLA_W)),
                  _layer_spec(layer, (1, GLA_W)), _layer_spec(layer, (1, GLA_W)),
                  rows(fwd, NA_W), full(NA_W, 1), full(NA_W, 2), _layer_spec(layer, na_bias.shape[1:]),
                  rows(fwd, SWA_W), full(SWA_KV_W), full(SWA_KV_W)],
        out_specs=[col(fwd, 0), col(bwd, 0), rows(soft_out, NA_W), rows(soft_out, SWA_W)],
        out_shape=[jax.ShapeDtypeStruct((batch, s, GLA_W), F32)] * 2
        + [jax.ShapeDtypeStruct((batch, s - first * tm, w), BF16) for w in (NA_W, SWA_W)],
        scratch_shapes=[pltpu.VMEM((GLA_W, GLA_W), F32)] * 2,
        compiler_params=_params(1, 1),
    )(sinks, gla_in, gla_in, gla_in, za, gla_in, gla_in, gla_in, za, wa_f, wa_b, ba_f, ba_b,
      na_qkv, na_qkv, na_qkv, na_bias, sq, sk, sv)


def _out_ffn_kernel(*refs, n_src, final):
    src_refs = refs[:n_src]
    (of_ref, ob_ref, gate_ref, na_ref, swa_ref, g1_ref, sh2_ref, sc2_ref, g2_ref, nffn_ref, gn_ref, avg_ref,
     wo_ref, wgate_ref, wup_ref, wdown_ref, fn_ref, o_ref) = refs[n_src:]
    x_in = _token_rows(src_refs)
    tm = x_in.shape[0]
    avg = avg_ref[...]
    halves = (slice(0, tm // 2), slice(tm // 2, tm))
    o = [of_ref[rows, :] + ob_ref[rows, :] for rows in halves]
    sq = [_split2(oh * oh) for oh in o]
    ms = [_dot(hi, avg) + _dot(lo, avg) for hi, lo in sq]
    y = [(oh * lax.rsqrt(msh + NORM_EPS) * gn_ref[...] * _silu(gate_ref[rows, :])).astype(BF16)
         for oh, msh, rows in zip(o, ms, halves)]
    mix = [_dot(jnp.concatenate([yh, na_ref[rows, :], swa_ref[rows, :]], axis=1), wo_ref[...])
           for yh, rows in zip(y, halves)]
    x1 = jnp.concatenate([x_in[rows] + g1_ref[...] * mh for mh, rows in zip(mix, halves)], axis=0)
    h = _rms(x1) * nffn_ref[...]
    h = (h * (1.0 + sc2_ref[...]) + sh2_ref[...]).astype(BF16)

    d_ff = wgate_ref.shape[1]
    chunks = [slice(c0, min(c0 + FF_CHUNK, d_ff)) for c0 in range(0, d_ff, FF_CHUNK)]
    acc = jnp.zeros(x1.shape, F32)
    gu = {}
    for t in range(len(chunks) + 1):
        if t < len(chunks):
            gu[t] = (_dot(h, wgate_ref[:, chunks[t]]), _dot(h, wup_ref[:, chunks[t]]))
        if t >= 1:
            g, u = gu.pop(t - 1)
            acc = acc + _dot((_silu(g) * u).astype(BF16), wdown_ref[chunks[t - 1], :])
    x2 = x1 + g2_ref[...] * acc
    if final:
        x2 = _rms(x2) * fn_ref[...]
    o_ref[...] = x2


def _out_ffn(srcs, o_f, o_b, gla_in, y_na, y_swa, layer, mods, norm_ffn, gla_norm, avg, w_o,
             w_gate, w_up, w_down, final_norm, skip_ctx, final):
    batch, s, _ = o_f.shape
    d = srcs[0].shape[2]
    d_ff = w_gate.shape[2]
    tm = ROW_BLOCK
    off = 1 if skip_ctx else 0
    nblk = s // tm - off
    tok_off = off
    if skip_ctx and len(srcs) == 2:
        srcs, tok_off = srcs[1:], 0
    row = lambda w, cb=0: pl.BlockSpec((None, tm, w), lambda b, j: (b, j + off, cb))
    out = lambda w: pl.BlockSpec((None, tm, w), lambda b, j: (b, j, 0))
    ms = lambda chunk: _mod_spec(layer, chunk, batch, not skip_ctx)
    return pl.pallas_call(
        functools.partial(_out_ffn_kernel, n_src=len(srcs), final=final),
        grid=(batch, nblk),
        in_specs=_token_specs(srcs, tm, tok_off) + [
            row(GLA_W), row(GLA_W), row(GLA_W, 3), out(NA_W), out(SWA_W),
            ms(2), ms(3), ms(4), ms(5), _layer_spec(layer, (1, d)), _layer_spec(layer, (1, GLA_W)),
            pl.BlockSpec((GLA_W, GLA_W), lambda b, j: (0, 0)),
            _layer_spec(layer, (GLA_W + NA_W + SWA_W, d)),
            _layer_spec(layer, (d, d_ff)), _layer_spec(layer, (d, d_ff)), _layer_spec(layer, (d_ff, d)),
            pl.BlockSpec((1, d), lambda b, j: (0, 0))],
        out_specs=out(d),
        out_shape=jax.ShapeDtypeStruct((batch, nblk * tm, d), F32),
        compiler_params=_params(2),
    )(*srcs, o_f, o_b, gla_in, y_na, y_swa, mods, mods, mods, mods, norm_ffn, gla_norm, avg, w_o,
      w_gate, w_up, w_down, final_norm)


def _rope_tables(ctx_len, seq_len):
    t = np.arange(seq_len)
    n_freq = HEAD_DIM // 4
    inv_freq = ROPE_THETA ** (-np.arange(n_freq) / n_freq)
    ang = np.concatenate([(t // GRID_W)[:, None] * inv_freq, (t % GRID_W)[:, None] * inv_freq], axis=-1)
    cos_h = np.concatenate([np.cos(ang), np.cos(ang)], axis=-1)
    sin_h = np.concatenate([-np.sin(ang), np.sin(ang)], axis=-1)
    cos_t = np.concatenate([np.ones((ctx_len, HEAD_DIM)), cos_h], axis=0)
    sin_t = np.concatenate([np.zeros((ctx_len, HEAD_DIM)), sin_h], axis=0)
    return jnp.asarray(np.tile(cos_t, (1, 2)), F32), jnp.asarray(np.tile(sin_t, (1, 2)), F32)


def _swa_pair_order(a, axis):
    n_pairs = SWA_W // PAIR_W
    heads = [h for p in range(n_pairs) for h in (p, p + n_pairs)]
    return jnp.concatenate([lax.slice_in_dim(a, h * HEAD_DIM, (h + 1) * HEAD_DIM, axis=axis) for h in heads],
                           axis=axis)


def _permute_w_in_kernel(w_ref, o_ref):
    scale = HEAD_DIM ** -0.5
    scale2 = scale * LOG2E
    o = np.cumsum([0, GLA_W, GLA_W, GLA_W, GLA_W, GLA_RANK, GLA_RANK, NA_W, NA_W, NA_W, SWA_W, SWA_KV_W, SWA_KV_W])
    n_pairs = SWA_W // PAIR_W
    swa_heads = [h for p in range(n_pairs) for h in (p, p + n_pairs)]
    moves = [(o[0], o[1], C_GLA, scale), (o[1], o[4], C_GLA + GLA_W, 1.0), (o[4], o[6], C_ZA, 1.0),
             (o[6], o[7], C_NA, scale2), (o[7], o[9], C_NA + NA_W, 1.0), (o[10], o[11], C_SK, 1.0),
             (o[11], o[12], C_SV, 1.0)]
    moves += [(o[9] + h * HEAD_DIM, o[9] + (h + 1) * HEAD_DIM, C_SQ + i * HEAD_DIM, scale2)
              for i, h in enumerate(swa_heads)]
    o_ref[:, C_ZA:IN_PERM_W] = jnp.zeros((o_ref.shape[0], LANES), o_ref.dtype)
    for lo, hi, dst, mult in moves:
        o_ref[:, dst:dst + (hi - lo)] = (w_ref[:, lo:hi] * mult).astype(o_ref.dtype)


def _permute_w_in(w):
    depth, d, n = w.shape
    tm = ROW_BLOCK
    return pl.pallas_call(
        _permute_w_in_kernel,
        grid=(depth, d // tm),
        in_specs=[pl.BlockSpec((None, tm, n), lambda i, k: (i, k, 0))],
        out_specs=pl.BlockSpec((None, tm, IN_PERM_W), lambda i, k: (i, k, 0)),
        out_shape=jax.ShapeDtypeStruct((depth, d, IN_PERM_W), BF16),
        compiler_params=_params(2),
    )(w)


def _na_bias_table(rpb):
    qc = np.arange(GRID_W)
    kc = np.arange(GRID_W)
    win = np.clip(qc - NA_KW // 2, 0, GRID_W - NA_KW)
    valid = (kc[None, :] >= win[:, None]) & (kc[None, :] < win[:, None] + NA_KW)
    dc = kc[None, :] - qc[:, None] + NA_KW - 1
    pick_col = ((dc[None] == np.arange(2 * NA_KW - 1)[:, None, None]) & valid[None]).astype(np.float32)
    t = jnp.einsum('lhde,eck->lhdck', rpb.astype(F32) * LOG2E, pick_col, precision=lax.Precision.HIGHEST)
    t = jnp.where(valid, t, -jnp.inf)
    return jnp.concatenate([t[:, :, :-1], t[:, :, 1:]], axis=-1)


def _pad_rank(wa2, offset):
    return jnp.pad(wa2, ((0, 0), (offset, LANES - offset - GLA_RANK), (0, 0)))


def kernel(x, c, ctx, c_ctx, w_mod, b_mod, norm_mix, norm_ffn, w_in, gla_wa2_f, gla_ba_f, gla_wa2_b, gla_ba_b,
           gla_norm, na_rpb, swa_sink, w_out, w_gate, w_up, w_down, final_norm):
    batch, seq_len, d = x.shape
    ctx_len = ctx.shape[1]
    depth = w_mod.shape[0]
    assert d == D_MODEL and ctx_len == ROW_BLOCK and seq_len % ROW_BLOCK == 0 and batch < 8
    assert seq_len % GRID_W == 0 and seq_len // GRID_W >= 2 * NA_KH

    cc = jnp.concatenate([c, c_ctx[None, :], jnp.zeros((8 - batch - 1, d), F32)], axis=0)
    mods = _modulation(cc, w_mod, b_mod).reshape(depth, 8, 1, 6 * d)
    cos_t, sin_t = _rope_tables(ctx_len, seq_len)
    head_avg = jnp.asarray(np.kron(np.eye(GLA_HEADS), np.full((HEAD_DIM, HEAD_DIM), 1.0 / HEAD_DIM)), BF16)
    w_in_p = _permute_w_in(w_in)
    wa_f, wa_b = _pad_rank(gla_wa2_f, 0), _pad_rank(gla_wa2_b, GLA_RANK)
    ba_f, ba_b = gla_ba_f.reshape(depth, 1, GLA_W), gla_ba_b.reshape(depth, 1, GLA_W)
    na_bias = _na_bias_table(na_rpb)
    w_o = jnp.concatenate([w_out[:, :GLA_W + NA_W], _swa_pair_order(w_out[:, GLA_W + NA_W:], 1)],
                          axis=1).astype(BF16)
    w_gate_b, w_up_b, w_down_b = w_gate.astype(BF16), w_up.astype(BF16), w_down.astype(BF16)
    norm_mix3, norm_ffn3 = norm_mix.reshape(depth, 1, d), norm_ffn.reshape(depth, 1, d)
    gla_norm3 = gla_norm.reshape(depth, 1, GLA_W)

    srcs = (ctx, x)
    for i in range(depth):
        last = i == depth - 1
        gla_in, za, na_qkv, sq, sk, sv = _in_proj(srcs, i, mods, norm_mix3, w_in_p, cos_t, sin_t)
        o_f, o_b, y_na, y_swa = _mixers(gla_in, za, na_qkv, sq, sk, sv, i, wa_f, wa_b, ba_f, ba_b, na_bias,
                                        swa_sink, ctx_len, ctx_out=not last)
        srcs = (_out_ffn(srcs, o_f, o_b, gla_in, y_na, y_swa, i, mods, norm_ffn3, gla_norm3, head_avg,
                         w_o, w_gate_b, w_up_b, w_down_b, final_norm.reshape(1, d),
                         skip_ctx=last, final=last),)
    return srcs[0]
```

```python
import functools

import jax
import jax.numpy as jnp
import numpy as np
from jax import lax
from jax.experimental import pallas as pl
from jax.experimental.pallas import tpu as pltpu

F32 = jnp.float32
BF16 = jnp.bfloat16

D_MODEL = 1024
HEAD_DIM = 64
GRID_W = 64
GLA_HEADS = 4
NA_HEADS = 6
SWA_HEADS = 6
SWA_KV_HEADS = 2
GLA_W = GLA_HEADS * HEAD_DIM
NA_W = NA_HEADS * HEAD_DIM
SWA_W = SWA_HEADS * HEAD_DIM
SWA_KV_W = SWA_KV_HEADS * HEAD_DIM
GLA_RANK = 16
GLA_TAU = 16.0
GLA_CHUNK = 64
NA_KH = 8
NA_KW = 16
SWA_WINDOW = 128
ROPE_THETA = 10000.0
NORM_EPS = 1e-6

LANES = 128
ROW_BLOCK = 256
PAIR_W = 2 * HEAD_DIM
LOG2E = float(np.log2(np.e))
FF_CHUNK = 256
MOD_BLOCK_COLS = 1536
V7X_VMEM_BYTES = 64 * 1024 * 1024
VMEM_LIMIT = V7X_VMEM_BYTES * 7 // 8

C_GLA = 0
C_NA = C_GLA + 4 * GLA_W
C_SV = C_NA + 3 * NA_W
C_SQ = C_SV + SWA_KV_W
C_SK = C_SQ + SWA_W
C_ZA = C_SK + SWA_KV_W
IN_PERM_W = C_ZA + LANES


def _dot(a, b):
    return jnp.dot(a, b, preferred_element_type=F32)


def _dot_nt(a, b):
    return lax.dot_general(a, b, (((1,), (1,)), ((), ())), preferred_element_type=F32)


def _dot_tn(a, b):
    return lax.dot_general(a, b, (((0,), (0,)), ((), ())), preferred_element_type=F32)


def _idiv(a, n):
    assert n & (n - 1) == 0
    return a >> (n.bit_length() - 1)


def _imod(a, n):
    assert n & (n - 1) == 0
    return a & (n - 1)


def _split2(a):
    hi = a.astype(BF16)
    lo = (a - hi.astype(F32)).astype(BF16)
    return hi, lo


def _split3(a):
    hi = a.astype(BF16)
    r = a - hi.astype(F32)
    mid = r.astype(BF16)
    lo = (r - mid.astype(F32)).astype(BF16)
    return hi, mid, lo


def _dot_split(a, b, merged=False):
    ah, al = _split2(a)
    bh, bl = _split2(b)
    if merged:
        return _dot(jnp.concatenate([ah, al, ah], axis=1), jnp.concatenate([bh, bh, bl], axis=0))
    return _dot(ah, bh) + _dot(al, bh) + _dot(ah, bl)


def _silu(a):
    return a * jax.nn.sigmoid(a)


def _rms(x):
    return x * lax.rsqrt(jnp.mean(x * x, axis=-1, keepdims=True) + NORM_EPS)


def _params(n_parallel, n_arbitrary=0):
    return pltpu.CompilerParams(
        dimension_semantics=("parallel",) * n_parallel + ("arbitrary",) * n_arbitrary,
        vmem_limit_bytes=VMEM_LIMIT)


def _mod_kernel(c_ref, w_ref, b_ref, o_ref):
    o_ref[...] = _dot_split(_silu(c_ref[...]), w_ref[...]) + b_ref[...]


def _modulation(cc, w_mod, b_mod):
    depth, d, n = w_mod.shape
    tn = MOD_BLOCK_COLS
    assert n % tn == 0
    return pl.pallas_call(
        _mod_kernel,
        grid=(depth, n // tn),
        in_specs=[
            pl.BlockSpec((8, d), lambda i, k: (0, 0)),
            pl.BlockSpec((None, d, tn), lambda i, k: (i, 0, k)),
            pl.BlockSpec((None, 1, tn), lambda i, k: (i, 0, k)),
        ],
        out_specs=pl.BlockSpec((None, 8, tn), lambda i, k: (i, 0, k)),
        out_shape=jax.ShapeDtypeStruct((depth, 8, n), F32),
        compiler_params=_params(2),
    )(cc, w_mod, b_mod.reshape(depth, 1, n))


def _mod_spec(layer, chunk, batch, ctx_first):
    if ctx_first:
        return pl.BlockSpec((None, None, 1, D_MODEL), lambda b, j: (layer, jnp.where(j == 0, batch, b), 0, chunk))
    return pl.BlockSpec((None, None, 1, D_MODEL), lambda b, j: (layer, b, 0, chunk))


def _layer_spec(layer, shape):
    return pl.BlockSpec((None,) + shape, lambda b, j: (layer,) + (0,) * len(shape))


def _token_rows(src_refs):
    if len(src_refs) == 1:
        return src_refs[0][...]
    ctx_ref, x_ref = src_refs
    return jnp.where(pl.program_id(1) == 0, ctx_ref[...], x_ref[...])


def _token_specs(srcs, tm, off=0):
    d = srcs[0].shape[-1]
    if len(srcs) == 1:
        return [pl.BlockSpec((None, tm, d), lambda b, j: (b, j + off, 0))]
    assert off == 0
    return [pl.BlockSpec((None, tm, d), lambda b, j: (b, 0, 0)),
            pl.BlockSpec((None, tm, d), lambda b, j: (b, jnp.maximum(j - 1, 0), 0))]


def _in_proj_kernel(*refs, n_src):
    src_refs = refs[:n_src]
    sh_ref, sc_ref, nw_ref, w_ref, cos_ref, sin_ref, gla_ref, za_ref, na_ref, sq_ref, sk_ref, sv_ref = refs[n_src:]
    x = _token_rows(src_refs)
    tm = x.shape[0]
    lane = lax.broadcasted_iota(jnp.int32, (tm // 2, LANES), 1)
    first_half = _imod(lane, HEAD_DIM) < (HEAD_DIM // 2)
    n_q = SWA_W // LANES
    for rows in (slice(0, tm // 2), slice(tm // 2, tm)):
        h = _rms(x[rows]) * nw_ref[...]
        h = (h * (1.0 + sc_ref[...]) + sh_ref[...]).astype(BF16)
        qkz = _dot(h, w_ref[:, C_SQ:IN_PERM_W])
        qk = qkz[:, :C_ZA - C_SQ]
        cos = cos_ref[rows, :]
        sin = sin_ref[rows, :]
        for cb in range(n_q + 1):
            a = qk[:, cb * LANES:(cb + 1) * LANES]
            partner = jnp.where(first_half, pltpu.roll(a, LANES - HEAD_DIM // 2, 1), pltpu.roll(a, HEAD_DIM // 2, 1))
            r = (a * cos + partner * sin).astype(BF16)
            if cb < n_q:
                sq_ref[rows, cb * LANES:(cb + 1) * LANES] = r
            else:
                sk_ref[rows, :] = r
        za_ref[rows, :] = qkz[:, C_ZA - C_SQ:]
        gla_ref[rows, :] = _dot(h, w_ref[:, C_GLA:C_NA])
        nv = _dot(h, w_ref[:, C_NA:C_SQ]).astype(BF16)
        na_ref[rows, :] = nv[:, :C_SV - C_NA]
        sv_ref[rows, :] = nv[:, C_SV - C_NA:]


def _in_proj(srcs, layer, mods, norm_w, w_perm, cos_t, sin_t):
    batch, d = srcs[0].shape[0], srcs[0].shape[2]
    s = sum(a.shape[1] for a in srcs)
    tm = ROW_BLOCK
    row = lambda w: pl.BlockSpec((None, tm, w), lambda b, j: (b, j, 0))
    tab = pl.BlockSpec((tm, LANES), lambda b, j: (j, 0))
    return pl.pallas_call(
        functools.partial(_in_proj_kernel, n_src=len(srcs)),
        grid=(batch, s // tm),
        in_specs=_token_specs(srcs, tm) + [
            _mod_spec(layer, 0, batch, True), _mod_spec(layer, 1, batch, True), _layer_spec(layer, (1, d)),
            _layer_spec(layer, (d, IN_PERM_W)), tab, tab],
        out_specs=[row(4 * GLA_W), row(LANES), row(3 * NA_W), row(SWA_W), row(SWA_KV_W), row(SWA_KV_W)],
        out_shape=[
            jax.ShapeDtypeStruct((batch, s, 4 * GLA_W), F32),
            jax.ShapeDtypeStruct((batch, s, LANES), F32),
            jax.ShapeDtypeStruct((batch, s, 3 * NA_W), BF16),
            jax.ShapeDtypeStruct((batch, s, SWA_W), BF16),
            jax.ShapeDtypeStruct((batch, s, SWA_KV_W), BF16),
            jax.ShapeDtypeStruct((batch, s, SWA_KV_W), BF16),
        ],
        compiler_params=_params(2),
    )(*srcs, mods, mods, norm_w, w_perm, cos_t, sin_t)


def _gla_stages(qf_ref, kf_ref, vf_ref, zf_ref, qb_ref, kb_ref, vb_ref, zb_ref,
                waf_ref, wab_ref, baf_ref, bab_ref, of_ref, ob_ref, stf_ref, stb_ref):
    c = GLA_CHUNK
    w = GLA_W
    n_chunks = ROW_BLOCK // c
    r64 = lax.broadcasted_iota(jnp.int32, (c, 3 * c), 0)
    c64 = _imod(lax.broadcasted_iota(jnp.int32, (c, 3 * c), 1), c)
    row_h = _idiv(lax.broadcasted_iota(jnp.int32, (GLA_HEADS * c, w), 0), c)
    col_h = _idiv(lax.broadcasted_iota(jnp.int32, (GLA_HEADS * c, w), 1), c)
    same_head = row_h == col_h
    ar = lax.broadcasted_iota(jnp.int32, (c, w), 0)
    ac = _imod(lax.broadcasted_iota(jnp.int32, (c, w), 1), c)
    dirs = [
        dict(q=qf_ref, k=kf_ref, v=vf_ref, z=zf_ref, wa=waf_ref, ba=baf_ref, o=of_ref, st=stf_ref, rev=False,
             order=range(n_chunks), cum=(r64 >= c64).astype(BF16), causal=ar >= ac),
        dict(q=qb_ref, k=kb_ref, v=vb_ref, z=zb_ref, wa=wab_ref, ba=bab_ref, o=ob_ref, st=stb_ref, rev=True,
             order=range(n_chunks - 1, -1, -1), cum=(r64 <= c64).astype(BF16), causal=ar <= ac),
    ]
    units = [(d, ci) for d in dirs for ci in d['order']]
    rows = lambda ci: slice(ci * c, (ci + 1) * c)
    t = {}

    def decay():
        log_a = []
        for d in dirs:
            z = _dot_split(d['z'][...], d['wa'][...], merged=True) + d['ba'][...]
            log_a.append((jnp.minimum(z, 0.0) - jnp.log1p(jnp.exp(-jnp.abs(z)))) * (1.0 / GLA_TAU))
        t['b'], t['b_tot'] = [], []
        for (d, ci), la in zip(units, [la for la in log_a for _ in range(n_chunks)]):
            t['b'].append(_dot(d['cum'], jnp.concatenate(_split3(la[rows(ci)]), axis=0)))
            t['b_tot'].append(t['b'][-1][0:1] if d['rev'] else t['b'][-1][c - 1:c])

    def scale():
        b, b_tot = t['b'], t['b_tot']
        k = [d['k'][rows(ci), :] for d, ci in units]
        t['v'] = [d['v'][rows(ci), :].astype(BF16) for d, ci in units]
        t['q_in'] = [(d['q'][rows(ci), :] * jnp.exp(bu)).astype(BF16) for (d, ci), bu in zip(units, b)]
        t['k_in'] = [(ku * jnp.exp(-bu)).astype(BF16) for ku, bu in zip(k, b)]
        t['k_end'] = [(ku * jnp.exp(bt - bu)).astype(BF16) for ku, bu, bt in zip(k, b, b_tot)]
        expand = lambda a: jnp.where(same_head, jnp.concatenate([a] * GLA_HEADS, axis=0), jnp.zeros((), BF16))
        t['kx'] = [expand(ku) for ku in t['k_in']]
        t['vx'] = [expand(vu) for vu in t['v']]

    def intra():
        a = [jnp.where(d['causal'], _dot_nt(qu, kxu), 0.0).astype(BF16)
             for (d, _), qu, kxu in zip(units, t['q_in'], t['kx'])]
        t['o_intra'] = [_dot(au, vxu) for au, vxu in zip(a, t['vx'])]
        t['upd'] = [jnp.where(same_head, _dot_tn(vu, ku), 0.0) for vu, ku in zip(t['v'], t['k_end'])]

    def inter():
        st_enter = []
        for di, d in enumerate(dirs):
            st = d['st'][...]
            for u in range(di * n_chunks, (di + 1) * n_chunks):
                st_enter.append(st.astype(BF16))
                st = st * jnp.exp(t['b_tot'][u]) + t['upd'][u]
            d['st'][...] = st
        for (d, ci), qu, su, ou in zip(units, t['q_in'], st_enter, t['o_intra']):
            d['o'][rows(ci), :] = _dot_nt(qu, su) + ou

    return [decay, scale, intra, inter]


def _stack_pair(q):
    left = lax.broadcasted_iota(jnp.int32, q.shape, 1) < HEAD_DIM
    zero = jnp.zeros((), q.dtype)
    return jnp.concatenate([jnp.where(left, q, zero), jnp.where(left, zero, q)], axis=0)


def _with_ones(v):
    left = lax.broadcasted_iota(jnp.int32, v.shape, 1) < HEAD_DIM
    one = jnp.ones((), v.dtype)
    return jnp.concatenate([jnp.where(left, v, one), jnp.where(left, one, v)], axis=1)


def _normalise(o, extra=None):
    denom = pltpu.roll(o, HEAD_DIM, 1)
    if extra is not None:
        denom = denom + extra
    return o / denom


def _softmax_weights(scores, extra_logit=None):
    m = functools.reduce(jnp.maximum, [jnp.max(s, axis=-1, keepdims=True) for s in scores])
    if extra_logit is not None:
        m = jnp.maximum(m, extra_logit)
    ps = [jnp.exp2(s - m).astype(BF16) for s in scores]
    return ps, (None if extra_logit is None else jnp.exp2(extra_logit - m))


def _na_stages(q_ref, k_ref, v_ref, bias_ref, o_ref, j, ctx_len, grid_rows, is_ctx):
    n_pairs = NA_W // PAIR_W
    rows_per_block = ROW_BLOCK // GRID_W
    n_loc = NA_KH * GRID_W
    unit = 2 * GRID_W
    pair_cols = [slice(p * PAIR_W, (p + 1) * PAIR_W) for p in range(n_pairs)]
    units = [(rr, p) for p in range(n_pairs) for rr in range(rows_per_block)]
    ctx_rows = lambda rr: slice(rr * unit, (rr + 1) * unit)
    t = {}

    def scores():
        qx = {(rr, p): _stack_pair(q_ref[rr * GRID_W:(rr + 1) * GRID_W, pair_cols[p]]) for rr, p in units}
        t['s_ctx'] = [_dot_nt(jnp.concatenate([qx[rr, p] for rr in range(rows_per_block)], axis=0),
                              k_ref[0:ctx_len, pair_cols[p]]) for p in range(n_pairs)]
        if is_ctx:
            return
        t['k0'], bias = [], {}
        for rr in range(rows_per_block):
            r = (j - 1) * rows_per_block + rr
            first_key_row = jnp.clip(r - NA_KH // 2, 0, grid_rows - NA_KH)
            t['k0'].append(pl.multiple_of(ctx_len + first_key_row * GRID_W, GRID_W))
            d0 = first_key_row - r + NA_KH - 1
            for p in range(n_pairs):
                bias[rr, p] = jnp.concatenate(
                    [jnp.concatenate([bias_ref[2 * p + side, d0 + 2 * m] for m in range(NA_KH // 2)], axis=1)
                     for side in (0, 1)], axis=0)
        t['s_loc'] = {(rr, p): _dot_nt(qx[rr, p], k_ref[pl.ds(t['k0'][rr], n_loc), pair_cols[p]]) + bias[rr, p]
                      for rr, p in units}

    def softmax():
        t['w'] = {(rr, p): _softmax_weights(([] if is_ctx else [t['s_loc'][rr, p]]) + [t['s_ctx'][p][ctx_rows(rr)]])
                  for rr, p in units}

    def values():
        w = t['w']
        left = lax.broadcasted_iota(jnp.int32, (GRID_W, PAIR_W), 1) < HEAD_DIM
        for p in range(n_pairs):
            o_ctx = _dot(jnp.concatenate([w[rr, p][0][-1] for rr in range(rows_per_block)], axis=0),
                         _with_ones(v_ref[0:ctx_len, pair_cols[p]]))
            for rr in range(rows_per_block):
                o = o_ctx[ctx_rows(rr)]
                if not is_ctx:
                    o = o + _dot(w[rr, p][0][0], _with_ones(v_ref[pl.ds(t['k0'][rr], n_loc), pair_cols[p]]))
                o_ref[rr * GRID_W:(rr + 1) * GRID_W, pair_cols[p]] = jnp.where(
                    left, _normalise(o[:GRID_W, :PAIR_W]), _normalise(o[GRID_W:, PAIR_W:])).astype(o_ref.dtype)

    return [scores, softmax, values]


def _swa_stages(sink, q_ref, k_ref, v_ref, o_ref, j, ctx_len, seq_len, is_ctx):
    tm = ROW_BLOCK
    n_pairs = SWA_W // PAIR_W
    part_rows = tm if is_ctx else tm // 2
    parts = [slice(r0, r0 + part_rows) for r0 in range(0, tm, part_rows)]
    n_loc = part_rows + 2 * SWA_WINDOW
    pair_cols = [slice(p * PAIR_W, (p + 1) * PAIR_W) for p in range(n_pairs)]
    lane = lax.broadcasted_iota(jnp.int32, (part_rows, PAIR_W), 1)
    zero = jnp.zeros((), BF16)
    units = [(p + side * n_pairs, p, side) for p in range(n_pairs) for side in (0, 1)]
    t = {}

    def scores():
        k_ctx = k_ref[0:ctx_len, :]
        t['start'], in_window = [], []
        if not is_ctx:
            for r in range(len(parts)):
                q0 = (j - 1) * tm + r * part_rows
                k0 = jnp.clip(q0 - SWA_WINDOW, 0, seq_len - n_loc)
                qpos = q0 + lax.broadcasted_iota(jnp.int32, (part_rows, n_loc), 0)
                kpos = k0 + lax.broadcasted_iota(jnp.int32, (part_rows, n_loc), 1)
                in_window.append(jnp.abs(qpos - kpos) <= SWA_WINDOW)
                t['start'].append(pl.multiple_of(ctx_len + k0, SWA_WINDOW))
        t['s'] = {}
        for u, (_, p, side) in enumerate(units):
            for r, rows in enumerate(parts):
                q = jnp.where((lane < HEAD_DIM) == (side == 0), q_ref[rows, pair_cols[p]], zero)
                t['s'][u, r] = [_dot_nt(q, k_ctx)]
                if not is_ctx:
                    k_loc = k_ref[pl.ds(t['start'][r], n_loc), :]
                    t['s'][u, r].insert(0, jnp.where(in_window[r], _dot_nt(q, k_loc), -jnp.inf))

    def softmax():
        t['w'] = {(u, r): _softmax_weights(t['s'][u, r], sink[h] * LOG2E)
                  for u, (h, _, _) in enumerate(units) for r in range(len(parts))}

    def values():
        for r, rows in enumerate(parts):
            vs = [v_ref[0:ctx_len, :]]
            if not is_ctx:
                vs.insert(0, v_ref[pl.ds(t['start'][r], n_loc), :])
            vs = [_with_ones(v) for v in vs]
            for p, cols in enumerate(pair_cols):
                (ps_a, extra_a), (ps_b, extra_b) = t['w'][2 * p, r], t['w'][2 * p + 1, r]
                o = functools.reduce(jnp.add, [_dot(jnp.concatenate([pa, pb], axis=0), v)
                                               for pa, pb, v in zip(ps_a, ps_b, vs)])
                o_ref[rows, cols] = jnp.where(lane < HEAD_DIM, _normalise(o[:part_rows, :PAIR_W], extra_a),
                                              _normalise(o[part_rows:, PAIR_W:], extra_b)).astype(o_ref.dtype)

    return [scores, softmax, values]


def _mixers_kernel(sinks_ref, qf_ref, kf_ref, vf_ref, zf_ref, qb_ref, kb_ref, vb_ref, zb_ref,
                   waf_ref, wab_ref, baf_ref, bab_ref, nq_ref, nk_ref, nv_ref, nbias_ref, sq_ref, sk_ref, sv_ref,
                   of_ref, ob_ref, yna_ref, yswa_ref, stf_ref, stb_ref, *, layer, ctx_len, seq_len, ctx_out):
    j = pl.program_id(1)

    @pl.when(j == 0)
    def _():
        stf_ref[...] = jnp.zeros_like(stf_ref)
        stb_ref[...] = jnp.zeros_like(stb_ref)

    sink = [sinks_ref[layer, h] for h in range(SWA_HEADS)]

    def block(is_ctx):
        gla = _gla_stages(qf_ref, kf_ref, vf_ref, zf_ref, qb_ref, kb_ref, vb_ref, zb_ref,
                          waf_ref, wab_ref, baf_ref, bab_ref, of_ref, ob_ref, stf_ref, stb_ref)
        if is_ctx and not ctx_out:
            order = gla
        else:
            na = _na_stages(nq_ref, nk_ref, nv_ref, nbias_ref, yna_ref, j, ctx_len, seq_len // GRID_W, is_ctx)
            swa = _swa_stages(sink, sq_ref, sk_ref, sv_ref, yswa_ref, j, ctx_len, seq_len, is_ctx)
            order = [gla[0], na[0], swa[0], gla[1], gla[2], na[1], na[2], swa[1], swa[2], gla[3]]
        for stage in order:
            stage()

    pl.when(j == 0)(functools.partial(block, True))
    pl.when(j > 0)(functools.partial(block, False))


def _mixers(gla_in, za, na_qkv, sq, sk, sv, layer, wa_f, wa_b, ba_f, ba_b, na_bias, sinks, ctx_len, ctx_out):
    batch, s, _ = gla_in.shape
    tm = ROW_BLOCK
    nblk = s // tm
    first = 0 if ctx_out else 1
    fwd = lambda j: j
    bwd = lambda j: jnp.where(j == 0, 0, nblk - j)
    soft_out = lambda j: jnp.maximum(j - first, 0)

    def col(order, cb):
        return pl.BlockSpec((None, tm, GLA_W), lambda b, j: (b, order(j), cb))

    def rows(order, w):
        return pl.BlockSpec((None, tm, w), lambda b, j: (b, order(j), 0))

    def full(w, cb=0):
        return pl.BlockSpec((None, s, w), lambda b, j: (b, 0, cb))

    return pl.pallas_call(
        functools.partial(_mixers_kernel, layer=layer, ctx_len=ctx_len, seq_len=s - ctx_len, ctx_out=ctx_out),
        grid=(batch, nblk),
        in_specs=[pl.BlockSpec(memory_space=pltpu.SMEM),
                  col(fwd, 0), col(fwd, 1), col(fwd, 2), rows(fwd, LANES),
                  col(bwd, 0), col(bwd, 1), col(bwd, 2), rows(bwd, LANES),
                  _layer_spec(layer, (LANES, GLA_W)), _layer_spec(layer, (LANES, GLA_W)),
                  _layer_spec(layer, (1, GLA_W)), _layer_spec(layer, (1, GLA_W)),
                  rows(fwd, NA_W), full(NA_W, 1), full(NA_W, 2), _layer_spec(layer, na_bias.shape[1:]),
                  rows(fwd, SWA_W), full(SWA_KV_W), full(SWA_KV_W)],
        out_specs=[col(fwd, 0), col(bwd, 0), rows(soft_out, NA_W), rows(soft_out, SWA_W)],
        out_shape=[jax.ShapeDtypeStruct((batch, s, GLA_W), F32)] * 2
        + [jax.ShapeDtypeStruct((batch, s - first * tm, w), BF16) for w in (NA_W, SWA_W)],
        scratch_shapes=[pltpu.VMEM((GLA_W, GLA_W), F32)] * 2,
        compiler_params=_params(1, 1),
    )(sinks, gla_in, gla_in, gla_in, za, gla_in, gla_in, gla_in, za, wa_f, wa_b, ba_f, ba_b,
      na_qkv, na_qkv, na_qkv, na_bias, sq, sk, sv)


def _out_ffn_kernel(*refs, n_src, final):
    src_refs = refs[:n_src]
    (of_ref, ob_ref, gate_ref, na_ref, swa_ref, g1_ref, sh2_ref, sc2_ref, g2_ref, nffn_ref, gn_ref, avg_ref,
     wo_ref, wgate_ref, wup_ref, wdown_ref, fn_ref, o_ref) = refs[n_src:]
    x_in = _token_rows(src_refs)
    tm = x_in.shape[0]
    avg = avg_ref[...]
    halves = (slice(0, tm // 2), slice(tm // 2, tm))
    o = [of_ref[rows, :] + ob_ref[rows, :] for rows in halves]
    sq = [_split2(oh * oh) for oh in o]
    ms = [_dot(hi, avg) + _dot(lo, avg) for hi, lo in sq]
    y = [(oh * lax.rsqrt(msh + NORM_EPS) * gn_ref[...] * _silu(gate_ref[rows, :])).astype(BF16)
         for oh, msh, rows in zip(o, ms, halves)]
    mix = [_dot(jnp.concatenate([yh, na_ref[rows, :], swa_ref[rows, :]], axis=1), wo_ref[...])
           for yh, rows in zip(y, halves)]
    x1 = jnp.concatenate([x_in[rows] + g1_ref[...] * mh for mh, rows in zip(mix, halves)], axis=0)
    h = _rms(x1) * nffn_ref[...]
    h = (h * (1.0 + sc2_ref[...]) + sh2_ref[...]).astype(BF16)

    d_ff = wgate_ref.shape[1]
    chunks = [slice(c0, min(c0 + FF_CHUNK, d_ff)) for c0 in range(0, d_ff, FF_CHUNK)]
    acc = jnp.zeros(x1.shape, F32)
    gu = {}
    for t in range(len(chunks) + 1):
        if t < len(chunks):
            gu[t] = (_dot(h, wgate_ref[:, chunks[t]]), _dot(h, wup_ref[:, chunks[t]]))
        if t >= 1:
            g, u = gu.pop(t - 1)
            acc = acc + _dot((_silu(g) * u).astype(BF16), wdown_ref[chunks[t - 1], :])
    x2 = x1 + g2_ref[...] * acc
    if final:
        x2 = _rms(x2) * fn_ref[...]
    o_ref[...] = x2


def _out_ffn(srcs, o_f, o_b, gla_in, y_na, y_swa, layer, mods, norm_ffn, gla_norm, avg, w_o,
             w_gate, w_up, w_down, final_norm, skip_ctx, final):
    batch, s, _ = o_f.shape
    d = srcs[0].shape[2]
    d_ff = w_gate.shape[2]
    tm = ROW_BLOCK
    off = 1 if skip_ctx else 0
    nblk = s // tm - off
    tok_off = off
    if skip_ctx and len(srcs) == 2:
        srcs, tok_off = srcs[1:], 0
    row = lambda w, cb=0: pl.BlockSpec((None, tm, w), lambda b, j: (b, j + off, cb))
    out = lambda w: pl.BlockSpec((None, tm, w), lambda b, j: (b, j, 0))
    ms = lambda chunk: _mod_spec(layer, chunk, batch, not skip_ctx)
    return pl.pallas_call(
        functools.partial(_out_ffn_kernel, n_src=len(srcs), final=final),
        grid=(batch, nblk),
        in_specs=_token_specs(srcs, tm, tok_off) + [
            row(GLA_W), row(GLA_W), row(GLA_W, 3), out(NA_W), out(SWA_W),
            ms(2), ms(3), ms(4), ms(5), _layer_spec(layer, (1, d)), _layer_spec(layer, (1, GLA_W)),
            pl.BlockSpec((GLA_W, GLA_W), lambda b, j: (0, 0)),
            _layer_spec(layer, (GLA_W + NA_W + SWA_W, d)),
            _layer_spec(layer, (d, d_ff)), _layer_spec(layer, (d, d_ff)), _layer_spec(layer, (d_ff, d)),
            pl.BlockSpec((1, d), lambda b, j: (0, 0))],
        out_specs=out(d),
        out_shape=jax.ShapeDtypeStruct((batch, nblk * tm, d), F32),
        compiler_params=_params(2),
    )(*srcs, o_f, o_b, gla_in, y_na, y_swa, mods, mods, mods, mods, norm_ffn, gla_norm, avg, w_o,
      w_gate, w_up, w_down, final_norm)


def _rope_tables(ctx_len, seq_len):
    t = np.arange(seq_len)
    n_freq = HEAD_DIM // 4
    inv_freq = ROPE_THETA ** (-np.arange(n_freq) / n_freq)
    ang = np.concatenate([(t // GRID_W)[:, None] * inv_freq, (t % GRID_W)[:, None] * inv_freq], axis=-1)
    cos_h = np.concatenate([np.cos(ang), np.cos(ang)], axis=-1)
    sin_h = np.concatenate([-np.sin(ang), np.sin(ang)], axis=-1)
    cos_t = np.concatenate([np.ones((ctx_len, HEAD_DIM)), cos_h], axis=0)
    sin_t = np.concatenate([np.zeros((ctx_len, HEAD_DIM)), sin_h], axis=0)
    return jnp.asarray(np.tile(cos_t, (1, 2)), F32), jnp.asarray(np.tile(sin_t, (1, 2)), F32)


def _swa_pair_order(a, axis):
    n_pairs = SWA_W // PAIR_W
    heads = [h for p in range(n_pairs) for h in (p, p + n_pairs)]
    return jnp.concatenate([lax.slice_in_dim(a, h * HEAD_DIM, (h + 1) * HEAD_DIM, axis=axis) for h in heads],
                           axis=axis)


def _permute_w_in_kernel(w_ref, o_ref):
    scale = HEAD_DIM ** -0.5
    scale2 = scale * LOG2E
    o = np.cumsum([0, GLA_W, GLA_W, GLA_W, GLA_W, GLA_RANK, GLA_RANK, NA_W, NA_W, NA_W, SWA_W, SWA_KV_W, SWA_KV_W])
    n_pairs = SWA_W // PAIR_W
    swa_heads = [h for p in range(n_pairs) for h in (p, p + n_pairs)]
    moves = [(o[0], o[1], C_GLA, scale), (o[1], o[4], C_GLA + GLA_W, 1.0), (o[4], o[6], C_ZA, 1.0),
             (o[6], o[7], C_NA, scale2), (o[7], o[9], C_NA + NA_W, 1.0), (o[10], o[11], C_SK, 1.0),
             (o[11], o[12], C_SV, 1.0)]
    moves += [(o[9] + h * HEAD_DIM, o[9] + (h + 1) * HEAD_DIM, C_SQ + i * HEAD_DIM, scale2)
              for i, h in enumerate(swa_heads)]
    o_ref[:, C_ZA:IN_PERM_W] = jnp.zeros((o_ref.shape[0], LANES), o_ref.dtype)
    for lo, hi, dst, mult in moves:
        o_ref[:, dst:dst + (hi - lo)] = (w_ref[:, lo:hi] * mult).astype(o_ref.dtype)


def _permute_w_in(w):
    depth, d, n = w.shape
    tm = ROW_BLOCK
    return pl.pallas_call(
        _permute_w_in_kernel,
        grid=(depth, d // tm),
        in_specs=[pl.BlockSpec((None, tm, n), lambda i, k: (i, k, 0))],
        out_specs=pl.BlockSpec((None, tm, IN_PERM_W), lambda i, k: (i, k, 0)),
        out_shape=jax.ShapeDtypeStruct((depth, d, IN_PERM_W), BF16),
        compiler_params=_params(2),
    )(w)


def _na_bias_table(rpb):
    qc = np.arange(GRID_W)
    kc = np.arange(GRID_W)
    win = np.clip(qc - NA_KW // 2, 0, GRID_W - NA_KW)
    valid = (kc[None, :] >= win[:, None]) & (kc[None, :] < win[:, None] + NA_KW)
    dc = kc[None, :] - qc[:, None] + NA_KW - 1
    pick_col = ((dc[None] == np.arange(2 * NA_KW - 1)[:, None, None]) & valid[None]).astype(np.float32)
    t = jnp.einsum('lhde,eck->lhdck', rpb.astype(F32) * LOG2E, pick_col, precision=lax.Precision.HIGHEST)
    t = jnp.where(valid, t, -jnp.inf)
    return jnp.concatenate([t[:, :, :-1], t[:, :, 1:]], axis=-1)


def _pad_rank(wa2, offset):
    return jnp.pad(wa2, ((0, 0), (offset, LANES - offset - GLA_RANK), (0, 0)))


def kernel(x, c, ctx, c_ctx, w_mod, b_mod, norm_mix, norm_ffn, w_in, gla_wa2_f, gla_ba_f, gla_wa2_b, gla_ba_b,
           gla_norm, na_rpb, swa_sink, w_out, w_gate, w_up, w_down, final_norm):
    batch, seq_len, d = x.shape
    ctx_len = ctx.shape[1]
    depth = w_mod.shape[0]
    assert d == D_MODEL and ctx_len == ROW_BLOCK and seq_len % ROW_BLOCK == 0 and batch < 8
    assert seq_len % GRID_W == 0 and seq_len // GRID_W >= 2 * NA_KH

    cc = jnp.concatenate([c, c_ctx[None, :], jnp.zeros((8 - batch - 1, d), F32)], axis=0)
    mods = _modulation(cc, w_mod, b_mod).reshape(depth, 8, 1, 6 * d)
    cos_t, sin_t = _rope_tables(ctx_len, seq_len)
    head_avg = jnp.asarray(np.kron(np.eye(GLA_HEADS), np.full((HEAD_DIM, HEAD_DIM), 1.0 / HEAD_DIM)), BF16)
    w_in_p = _permute_w_in(w_in)
    wa_f, wa_b = _pad_rank(gla_wa2_f, 0), _pad_rank(gla_wa2_b, GLA_RANK)
    ba_f, ba_b = gla_ba_f.reshape(depth, 1, GLA_W), gla_ba_b.reshape(depth, 1, GLA_W)
    na_bias = _na_bias_table(na_rpb)
    w_o = jnp.concatenate([w_out[:, :GLA_W + NA_W], _swa_pair_order(w_out[:, GLA_W + NA_W:], 1)],
                          axis=1).astype(BF16)
    w_gate_b, w_up_b, w_down_b = w_gate.astype(BF16), w_up.astype(BF16), w_down.astype(BF16)
    norm_mix3, norm_ffn3 = norm_mix.reshape(depth, 1, d), norm_ffn.reshape(depth, 1, d)
    gla_norm3 = gla_norm.reshape(depth, 1, GLA_W)

    srcs = (ctx, x)
    for i in range(depth):
        last = i == depth - 1
        gla_in, za, na_qkv, sq, sk, sv = _in_proj(srcs, i, mods, norm_mix3, w_in_p, cos_t, sin_t)
        o_f, o_b, y_na, y_swa = _mixers(gla_in, za, na_qkv, sq, sk, sv, i, wa_f, wa_b, ba_f, ba_b, na_bias,
                                        swa_sink, ctx_len, ctx_out=not last)
        srcs = (_out_ffn(srcs, o_f, o_b, gla_in, y_na, y_swa, i, mods, norm_ffn3, gla_norm3, head_avg,
                         w_o, w_gate_b, w_up_b, w_down_b, final_norm.reshape(1, d),
                         skip_ctx=last, final=last),)
    return srcs[0]
```

```python
import functools

import jax
import jax.numpy as jnp
import numpy as np
from jax import lax
from jax.experimental import pallas as pl
from jax.experimental.pallas import tpu as pltpu

F32 = jnp.float32
BF16 = jnp.bfloat16

D_MODEL = 1024
HEAD_DIM = 64
GRID_W = 64
GLA_HEADS = 4
NA_HEADS = 6
SWA_HEADS = 6
SWA_KV_HEADS = 2
GLA_W = GLA_HEADS * HEAD_DIM
NA_W = NA_HEADS * HEAD_DIM
SWA_W = SWA_HEADS * HEAD_DIM
SWA_KV_W = SWA_KV_HEADS * HEAD_DIM
GLA_RANK = 16
GLA_TAU = 16.0
GLA_CHUNK = 64
NA_KH = 8
NA_KW = 16
SWA_WINDOW = 128
ROPE_THETA = 10000.0
NORM_EPS = 1e-6

LANES = 128
ROW_BLOCK = 256
PAIR_W = 2 * HEAD_DIM
LOG2E = float(np.log2(np.e))
FF_CHUNK = 256
MOD_BLOCK_COLS = 1536
V7X_VMEM_BYTES = 64 * 1024 * 1024
VMEM_LIMIT = V7X_VMEM_BYTES * 7 // 8

C_GLA = 0
C_NA = C_GLA + 4 * GLA_W
C_SV = C_NA + 3 * NA_W
C_SQ = C_SV + SWA_KV_W
C_SK = C_SQ + SWA_W
C_ZA = C_SK + SWA_KV_W
IN_PERM_W = C_ZA + LANES


def _dot(a, b):
    return jnp.dot(a, b, preferred_element_type=F32)


def _dot_nt(a, b):
    return lax.dot_general(a, b, (((1,), (1,)), ((), ())), preferred_element_type=F32)


def _dot_tn(a, b):
    return lax.dot_general(a, b, (((0,), (0,)), ((), ())), preferred_element_type=F32)


def _idiv(a, n):
    assert n & (n - 1) == 0
    return a >> (n.bit_length() - 1)


def _imod(a, n):
    assert n & (n - 1) == 0
    return a & (n - 1)


def _split2(a):
    hi = a.astype(BF16)
    lo = (a - hi.astype(F32)).astype(BF16)
    return hi, lo


def _split3(a):
    hi = a.astype(BF16)
    r = a - hi.astype(F32)
    mid = r.astype(BF16)
    lo = (r - mid.astype(F32)).astype(BF16)
    return hi, mid, lo


def _dot_split(a, b, merged=False):
    ah, al = _split2(a)
    bh, bl = _split2(b)
    if merged:
        return _dot(jnp.concatenate([ah, al, ah], axis=1), jnp.concatenate([bh, bh, bl], axis=0))
    return _dot(ah, bh) + _dot(al, bh) + _dot(ah, bl)


def _silu(a):
    return a * jax.nn.sigmoid(a)


def _rms(x):
    return x * lax.rsqrt(jnp.mean(x * x, axis=-1, keepdims=True) + NORM_EPS)


def _params(n_parallel, n_arbitrary=0):
    return pltpu.CompilerParams(
        dimension_semantics=("parallel",) * n_parallel + ("arbitrary",) * n_arbitrary,
        vmem_limit_bytes=VMEM_LIMIT)


def _mod_kernel(c_ref, w_ref, b_ref, o_ref):
    o_ref[...] = _dot_split(_silu(c_ref[...]), w_ref[...]) + b_ref[...]


def _modulation(cc, w_mod, b_mod):
    depth, d, n = w_mod.shape
    tn = MOD_BLOCK_COLS
    assert n % tn == 0
    return pl.pallas_call(
        _mod_kernel,
        grid=(depth, n // tn),
        in_specs=[
            pl.BlockSpec((8, d), lambda i, k: (0, 0)),
            pl.BlockSpec((None, d, tn), lambda i, k: (i, 0, k)),
            pl.BlockSpec((None, 1, tn), lambda i, k: (i, 0, k)),
        ],
        out_specs=pl.BlockSpec((None, 8, tn), lambda i, k: (i, 0, k)),
        out_shape=jax.ShapeDtypeStruct((depth, 8, n), F32),
        compiler_params=_params(2),
    )(cc, w_mod, b_mod.reshape(depth, 1, n))


def _mod_spec(layer, chunk, batch, ctx_first):
    if ctx_first:
        return pl.BlockSpec((None, None, 1, D_MODEL), lambda b, j: (layer, jnp.where(j == 0, batch, b), 0, chunk))
    return pl.BlockSpec((None, None, 1, D_MODEL), lambda b, j: (layer, b, 0, chunk))


def _layer_spec(layer, shape):
    return pl.BlockSpec((None,) + shape, lambda b, j: (layer,) + (0,) * len(shape))


def _token_rows(src_refs, rows):
    if len(src_refs) == 1:
        return src_refs[0][rows, :]
    ctx_ref, x_ref = src_refs
    return jnp.where(pl.program_id(1) == 0, ctx_ref[rows, :], x_ref[rows, :])


def _token_specs(srcs, tm, off=0):
    d = srcs[0].shape[-1]
    if len(srcs) == 1:
        return [pl.BlockSpec((None, tm, d), lambda b, j: (b, j + off, 0))]
    assert off == 0
    return [pl.BlockSpec((None, tm, d), lambda b, j: (b, 0, 0)),
            pl.BlockSpec((None, tm, d), lambda b, j: (b, jnp.maximum(j - 1, 0), 0))]


def _in_proj_kernel(*refs, n_src):
    src_refs = refs[:n_src]
    sh_ref, sc_ref, nw_ref, w_ref, cos_ref, sin_ref, gla_ref, za_ref, na_ref, sq_ref, sk_ref, sv_ref = refs[n_src:]
    tm = gla_ref.shape[0]
    lane = lax.broadcasted_iota(jnp.int32, (tm // 2, LANES), 1)
    first_half = _imod(lane, HEAD_DIM) < (HEAD_DIM // 2)
    n_q = SWA_W // LANES
    for rows in (slice(0, tm // 2), slice(tm // 2, tm)):
        h = _rms(_token_rows(src_refs, rows)) * nw_ref[...]
        h = (h * (1.0 + sc_ref[...]) + sh_ref[...]).astype(BF16)
        qkz = _dot(h, w_ref[:, C_SQ:IN_PERM_W])
        qk = qkz[:, :C_ZA - C_SQ]
        cos = cos_ref[rows, :]
        sin = sin_ref[rows, :]
        for cb in range(n_q + 1):
            a = qk[:, cb * LANES:(cb + 1) * LANES]
            partner = jnp.where(first_half, pltpu.roll(a, LANES - HEAD_DIM // 2, 1), pltpu.roll(a, HEAD_DIM // 2, 1))
            r = (a * cos + partner * sin).astype(BF16)
            if cb < n_q:
                sq_ref[rows, cb * LANES:(cb + 1) * LANES] = r
            else:
                sk_ref[rows, :] = r
        za_ref[rows, :] = qkz[:, C_ZA - C_SQ:]
        gla_ref[rows, :] = _dot(h, w_ref[:, C_GLA:C_NA])
        nv = _dot(h, w_ref[:, C_NA:C_SQ]).astype(BF16)
        na_ref[rows, :] = nv[:, :C_SV - C_NA]
        sv_ref[rows, :] = nv[:, C_SV - C_NA:]


def _in_proj(srcs, layer, mods, norm_w, w_perm, cos_t, sin_t):
    batch, d = srcs[0].shape[0], srcs[0].shape[2]
    s = sum(a.shape[1] for a in srcs)
    tm = ROW_BLOCK
    row = lambda w: pl.BlockSpec((None, tm, w), lambda b, j: (b, j, 0))
    tab = pl.BlockSpec((tm, LANES), lambda b, j: (j, 0))
    return pl.pallas_call(
        functools.partial(_in_proj_kernel, n_src=len(srcs)),
        grid=(batch, s // tm),
        in_specs=_token_specs(srcs, tm) + [
            _mod_spec(layer, 0, batch, True), _mod_spec(layer, 1, batch, True), _layer_spec(layer, (1, d)),
            _layer_spec(layer, (d, IN_PERM_W)), tab, tab],
        out_specs=[row(4 * GLA_W), row(LANES), row(3 * NA_W), row(SWA_W), row(SWA_KV_W), row(SWA_KV_W)],
        out_shape=[
            jax.ShapeDtypeStruct((batch, s, 4 * GLA_W), F32),
            jax.ShapeDtypeStruct((batch, s, LANES), F32),
            jax.ShapeDtypeStruct((batch, s, 3 * NA_W), BF16),
            jax.ShapeDtypeStruct((batch, s, SWA_W), BF16),
            jax.ShapeDtypeStruct((batch, s, SWA_KV_W), BF16),
            jax.ShapeDtypeStruct((batch, s, SWA_KV_W), BF16),
        ],
        compiler_params=_params(2),
    )(*srcs, mods, mods, norm_w, w_perm, cos_t, sin_t)


def _gla_stages(qf_ref, kf_ref, vf_ref, zf_ref, qb_ref, kb_ref, vb_ref, zb_ref,
                waf_ref, wab_ref, baf_ref, bab_ref, of_ref, ob_ref, stf_ref, stb_ref):
    c = GLA_CHUNK
    w = GLA_W
    n_chunks = ROW_BLOCK // c
    r64 = lax.broadcasted_iota(jnp.int32, (c, 3 * c), 0)
    c64 = _imod(lax.broadcasted_iota(jnp.int32, (c, 3 * c), 1), c)
    row_h = _idiv(lax.broadcasted_iota(jnp.int32, (GLA_HEADS * c, w), 0), c)
    col_h = _idiv(lax.broadcasted_iota(jnp.int32, (GLA_HEADS * c, w), 1), c)
    same_head = row_h == col_h
    ar = lax.broadcasted_iota(jnp.int32, (c, w), 0)
    ac = _imod(lax.broadcasted_iota(jnp.int32, (c, w), 1), c)
    dirs = [
        dict(q=qf_ref, k=kf_ref, v=vf_ref, z=zf_ref, wa=waf_ref, ba=baf_ref, o=of_ref, st=stf_ref, rev=False,
             order=range(n_chunks), cum=(r64 >= c64).astype(BF16), causal=ar >= ac),
        dict(q=qb_ref, k=kb_ref, v=vb_ref, z=zb_ref, wa=wab_ref, ba=bab_ref, o=ob_ref, st=stb_ref, rev=True,
             order=range(n_chunks - 1, -1, -1), cum=(r64 <= c64).astype(BF16), causal=ar <= ac),
    ]
    units = [(d, ci) for d in dirs for ci in d['order']]
    rows = lambda ci: slice(ci * c, (ci + 1) * c)
    t = {}

    def decay():
        log_a = []
        for d in dirs:
            z = _dot_split(d['z'][...], d['wa'][...], merged=True) + d['ba'][...]
            log_a.append((jnp.minimum(z, 0.0) - jnp.log1p(jnp.exp(-jnp.abs(z)))) * (1.0 / GLA_TAU))
        t['b'], t['b_tot'] = [], []
        for (d, ci), la in zip(units, [la for la in log_a for _ in range(n_chunks)]):
            t['b'].append(_dot(d['cum'], jnp.concatenate(_split3(la[rows(ci)]), axis=0)))
            t['b_tot'].append(t['b'][-1][0:1] if d['rev'] else t['b'][-1][c - 1:c])

    def scale():
        b, b_tot = t['b'], t['b_tot']
        k = [d['k'][rows(ci), :] for d, ci in units]
        t['v'] = [d['v'][rows(ci), :].astype(BF16) for d, ci in units]
        t['q_in'] = [(d['q'][rows(ci), :] * jnp.exp(bu)).astype(BF16) for (d, ci), bu in zip(units, b)]
        t['k_in'] = [(ku * jnp.exp(-bu)).astype(BF16) for ku, bu in zip(k, b)]
        t['k_end'] = [(ku * jnp.exp(bt - bu)).astype(BF16) for ku, bu, bt in zip(k, b, b_tot)]
        expand = lambda a: jnp.where(same_head, jnp.concatenate([a] * GLA_HEADS, axis=0), jnp.zeros((), BF16))
        t['kx'] = [expand(ku) for ku in t['k_in']]
        t['vx'] = [expand(vu) for vu in t['v']]

    def intra():
        a = [jnp.where(d['causal'], _dot_nt(qu, kxu), 0.0).astype(BF16)
             for (d, _), qu, kxu in zip(units, t['q_in'], t['kx'])]
        t['o_intra'] = [_dot(au, vxu) for au, vxu in zip(a, t['vx'])]
        t['upd'] = [jnp.where(same_head, _dot_tn(vu, ku), 0.0) for vu, ku in zip(t['v'], t['k_end'])]

    def inter():
        st_enter = []
        for di, d in enumerate(dirs):
            st = d['st'][...]
            for u in range(di * n_chunks, (di + 1) * n_chunks):
                st_enter.append(st.astype(BF16))
                st = st * jnp.exp(t['b_tot'][u]) + t['upd'][u]
            d['st'][...] = st
        for (d, ci), qu, su, ou in zip(units, t['q_in'], st_enter, t['o_intra']):
            d['o'][rows(ci), :] = _dot_nt(qu, su) + ou

    return [decay, scale, intra, inter]


def _stack_pair(q):
    left = lax.broadcasted_iota(jnp.int32, q.shape, 1) < HEAD_DIM
    zero = jnp.zeros((), q.dtype)
    return jnp.concatenate([jnp.where(left, q, zero), jnp.where(left, zero, q)], axis=0)


def _with_ones(v):
    left = lax.broadcasted_iota(jnp.int32, v.shape, 1) < HEAD_DIM
    one = jnp.ones((), v.dtype)
    return jnp.concatenate([jnp.where(left, v, one), jnp.where(left, one, v)], axis=1)


def _normalise(o, extra=None):
    denom = pltpu.roll(o, HEAD_DIM, 1)
    if extra is not None:
        denom = denom + extra
    return o / denom


def _softmax_weights(scores, extra_logit=None):
    m = functools.reduce(jnp.maximum, [jnp.max(s, axis=-1, keepdims=True) for s in scores])
    if extra_logit is not None:
        m = jnp.maximum(m, extra_logit)
    ps = [jnp.exp2(s - m).astype(BF16) for s in scores]
    return ps, (None if extra_logit is None else jnp.exp2(extra_logit - m))


def _na_stages(q_ref, k_ref, v_ref, bias_ref, o_ref, j, ctx_len, grid_rows, is_ctx):
    n_pairs = NA_W // PAIR_W
    rows_per_block = ROW_BLOCK // GRID_W
    n_loc = NA_KH * GRID_W
    unit = 2 * GRID_W
    pair_cols = [slice(p * PAIR_W, (p + 1) * PAIR_W) for p in range(n_pairs)]
    units = [(rr, p) for p in range(n_pairs) for rr in range(rows_per_block)]
    ctx_rows = lambda rr: slice(rr * unit, (rr + 1) * unit)
    t = {}

    def scores():
        qx = {(rr, p): _stack_pair(q_ref[rr * GRID_W:(rr + 1) * GRID_W, pair_cols[p]]) for rr, p in units}
        t['s_ctx'] = [_dot_nt(jnp.concatenate([qx[rr, p] for rr in range(rows_per_block)], axis=0),
                              k_ref[0:ctx_len, pair_cols[p]]) for p in range(n_pairs)]
        if is_ctx:
            return
        t['k0'], bias = [], {}
        for rr in range(rows_per_block):
            r = (j - 1) * rows_per_block + rr
            first_key_row = jnp.clip(r - NA_KH // 2, 0, grid_rows - NA_KH)
            t['k0'].append(pl.multiple_of(ctx_len + first_key_row * GRID_W, GRID_W))
            d0 = first_key_row - r + NA_KH - 1
            for p in range(n_pairs):
                bias[rr, p] = jnp.concatenate(
                    [jnp.concatenate([bias_ref[2 * p + side, d0 + 2 * m] for m in range(NA_KH // 2)], axis=1)
                     for side in (0, 1)], axis=0)
        t['s_loc'] = {(rr, p): _dot_nt(qx[rr, p], k_ref[pl.ds(t['k0'][rr], n_loc), pair_cols[p]]) + bias[rr, p]
                      for rr, p in units}

    def softmax():
        t['w'] = {(rr, p): _softmax_weights(([] if is_ctx else [t['s_loc'][rr, p]]) + [t['s_ctx'][p][ctx_rows(rr)]])
                  for rr, p in units}

    def values():
        w = t['w']
        left = lax.broadcasted_iota(jnp.int32, (GRID_W, PAIR_W), 1) < HEAD_DIM
        for p in range(n_pairs):
            o_ctx = _dot(jnp.concatenate([w[rr, p][0][-1] for rr in range(rows_per_block)], axis=0),
                         _with_ones(v_ref[0:ctx_len, pair_cols[p]]))
            for rr in range(rows_per_block):
                o = o_ctx[ctx_rows(rr)]
                if not is_ctx:
                    o = o + _dot(w[rr, p][0][0], _with_ones(v_ref[pl.ds(t['k0'][rr], n_loc), pair_cols[p]]))
                o_ref[rr * GRID_W:(rr + 1) * GRID_W, pair_cols[p]] = jnp.where(
                    left, _normalise(o[:GRID_W, :PAIR_W]), _normalise(o[GRID_W:, PAIR_W:])).astype(o_ref.dtype)

    return [scores, softmax, values]


def _swa_stages(sink, q_ref, k_ref, v_ref, o_ref, j, ctx_len, seq_len, is_ctx):
    tm = ROW_BLOCK
    n_pairs = SWA_W // PAIR_W
    part_rows = tm if is_ctx else tm // 2
    parts = [slice(r0, r0 + part_rows) for r0 in range(0, tm, part_rows)]
    n_loc = part_rows + 2 * SWA_WINDOW
    pair_cols = [slice(p * PAIR_W, (p + 1) * PAIR_W) for p in range(n_pairs)]
    lane = lax.broadcasted_iota(jnp.int32, (part_rows, PAIR_W), 1)
    zero = jnp.zeros((), BF16)
    units = [(p + side * n_pairs, p, side) for p in range(n_pairs) for side in (0, 1)]
    t = {}

    def scores():
        k_ctx = k_ref[0:ctx_len, :]
        t['start'], in_window = [], []
        if not is_ctx:
            for r in range(len(parts)):
                q0 = (j - 1) * tm + r * part_rows
                k0 = jnp.clip(q0 - SWA_WINDOW, 0, seq_len - n_loc)
                qpos = q0 + lax.broadcasted_iota(jnp.int32, (part_rows, n_loc), 0)
                kpos = k0 + lax.broadcasted_iota(jnp.int32, (part_rows, n_loc), 1)
                in_window.append(jnp.abs(qpos - kpos) <= SWA_WINDOW)
                t['start'].append(pl.multiple_of(ctx_len + k0, SWA_WINDOW))
        t['s'] = {}
        for u, (_, p, side) in enumerate(units):
            for r, rows in enumerate(parts):
                q = jnp.where((lane < HEAD_DIM) == (side == 0), q_ref[rows, pair_cols[p]], zero)
                t['s'][u, r] = [_dot_nt(q, k_ctx)]
                if not is_ctx:
                    k_loc = k_ref[pl.ds(t['start'][r], n_loc), :]
                    t['s'][u, r].insert(0, jnp.where(in_window[r], _dot_nt(q, k_loc), -jnp.inf))

    def softmax():
        t['w'] = {(u, r): _softmax_weights(t['s'][u, r], sink[h] * LOG2E)
                  for u, (h, _, _) in enumerate(units) for r in range(len(parts))}

    def values():
        for r, rows in enumerate(parts):
            vs = [v_ref[0:ctx_len, :]]
            if not is_ctx:
                vs.insert(0, v_ref[pl.ds(t['start'][r], n_loc), :])
            vs = [_with_ones(v) for v in vs]
            for p, cols in enumerate(pair_cols):
                (ps_a, extra_a), (ps_b, extra_b) = t['w'][2 * p, r], t['w'][2 * p + 1, r]
                o = functools.reduce(jnp.add, [_dot(jnp.concatenate([pa, pb], axis=0), v)
                                               for pa, pb, v in zip(ps_a, ps_b, vs)])
                o_ref[rows, cols] = jnp.where(lane < HEAD_DIM, _normalise(o[:part_rows, :PAIR_W], extra_a),
                                              _normalise(o[part_rows:, PAIR_W:], extra_b)).astype(o_ref.dtype)

    return [scores, softmax, values]


def _mixers_kernel(sinks_ref, qf_ref, kf_ref, vf_ref, zf_ref, qb_ref, kb_ref, vb_ref, zb_ref,
                   waf_ref, wab_ref, baf_ref, bab_ref, nq_ref, nk_ref, nv_ref, nbias_ref, sq_ref, sk_ref, sv_ref,
                   of_ref, ob_ref, yna_ref, yswa_ref, stf_ref, stb_ref, *, layer, ctx_len, seq_len, ctx_out):
    j = pl.program_id(1)

    @pl.when(j == 0)
    def _():
        stf_ref[...] = jnp.zeros_like(stf_ref)
        stb_ref[...] = jnp.zeros_like(stb_ref)

    sink = [sinks_ref[layer, h] for h in range(SWA_HEADS)]

    def block(is_ctx):
        gla = _gla_stages(qf_ref, kf_ref, vf_ref, zf_ref, qb_ref, kb_ref, vb_ref, zb_ref,
                          waf_ref, wab_ref, baf_ref, bab_ref, of_ref, ob_ref, stf_ref, stb_ref)
        if is_ctx and not ctx_out:
            order = gla
        else:
            na = _na_stages(nq_ref, nk_ref, nv_ref, nbias_ref, yna_ref, j, ctx_len, seq_len // GRID_W, is_ctx)
            swa = _swa_stages(sink, sq_ref, sk_ref, sv_ref, yswa_ref, j, ctx_len, seq_len, is_ctx)
            order = [gla[0], na[0], swa[0], gla[1], gla[2], na[1], na[2], swa[1], swa[2], gla[3]]
        for stage in order:
            stage()

    pl.when(j == 0)(functools.partial(block, True))
    pl.when(j > 0)(functools.partial(block, False))


def _mixers(gla_in, za, na_qkv, sq, sk, sv, layer, wa_f, wa_b, ba_f, ba_b, na_bias, sinks, ctx_len, ctx_out):
    batch, s, _ = gla_in.shape
    tm = ROW_BLOCK
    nblk = s // tm
    first = 0 if ctx_out else 1
    fwd = lambda j: j
    bwd = lambda j: jnp.where(j == 0, 0, nblk - j)
    soft_out = lambda j: jnp.maximum(j - first, 0)

    def col(order, cb):
        return pl.BlockSpec((None, tm, GLA_W), lambda b, j: (b, order(j), cb))

    def rows(order, w):
        return pl.BlockSpec((None, tm, w), lambda b, j: (b, order(j), 0))

    def full(w, cb=0):
        return pl.BlockSpec((None, s, w), lambda b, j: (b, 0, cb))

    return pl.pallas_call(
        functools.partial(_mixers_kernel, layer=layer, ctx_len=ctx_len, seq_len=s - ctx_len, ctx_out=ctx_out),
        grid=(batch, nblk),
        in_specs=[pl.BlockSpec(memory_space=pltpu.SMEM),
                  col(fwd, 0), col(fwd, 1), col(fwd, 2), rows(fwd, LANES),
                  col(bwd, 0), col(bwd, 1), col(bwd, 2), rows(bwd, LANES),
                  _layer_spec(layer, (LANES, GLA_W)), _layer_spec(layer, (LANES, GLA_W)),
                  _layer_spec(layer, (1, GLA_W)), _layer_spec(layer, (1, GLA_W)),
                  rows(fwd, NA_W), full(NA_W, 1), full(NA_W, 2), _layer_spec(layer, na_bias.shape[1:]),
                  rows(fwd, SWA_W), full(SWA_KV_W), full(SWA_KV_W)],
        out_specs=[col(fwd, 0), col(bwd, 0), rows(soft_out, NA_W), rows(soft_out, SWA_W)],
        out_shape=[jax.ShapeDtypeStruct((batch, s, GLA_W), F32)] * 2
        + [jax.ShapeDtypeStruct((batch, s - first * tm, w), BF16) for w in (NA_W, SWA_W)],
        scratch_shapes=[pltpu.VMEM((GLA_W, GLA_W), F32)] * 2,
        compiler_params=_params(1, 1),
    )(sinks, gla_in, gla_in, gla_in, za, gla_in, gla_in, gla_in, za, wa_f, wa_b, ba_f, ba_b,
      na_qkv, na_qkv, na_qkv, na_bias, sq, sk, sv)


def _out_ffn_kernel(*refs, n_src, final):
    src_refs = refs[:n_src]
    (of_ref, ob_ref, gate_ref, na_ref, swa_ref, g1_ref, sh2_ref, sc2_ref, g2_ref, nffn_ref, gn_ref, avg_ref,
     wo_ref, wgate_ref, wup_ref, wdown_ref, fn_ref, o_ref) = refs[n_src:]
    tm = o_ref.shape[0]
    avg = avg_ref[...]
    halves = (slice(0, tm // 2), slice(tm // 2, tm))
    o = [of_ref[rows, :] + ob_ref[rows, :] for rows in halves]
    sq = [_split2(oh * oh) for oh in o]
    ms = [_dot(hi, avg) + _dot(lo, avg) for hi, lo in sq]
    y = [(oh * lax.rsqrt(msh + NORM_EPS) * gn_ref[...] * _silu(gate_ref[rows, :])).astype(BF16)
         for oh, msh, rows in zip(o, ms, halves)]
    mix = [_dot(jnp.concatenate([yh, na_ref[rows, :], swa_ref[rows, :]], axis=1), wo_ref[...])
           for yh, rows in zip(y, halves)]
    x1 = jnp.concatenate([_token_rows(src_refs, rows) + g1_ref[...] * mh for mh, rows in zip(mix, halves)], axis=0)
    h = _rms(x1) * nffn_ref[...]
    h = (h * (1.0 + sc2_ref[...]) + sh2_ref[...]).astype(BF16)

    d_ff = wgate_ref.shape[1]
    chunks = [slice(c0, min(c0 + FF_CHUNK, d_ff)) for c0 in range(0, d_ff, FF_CHUNK)]
    acc = jnp.zeros(x1.shape, F32)
    gu = {}
    for t in range(len(chunks) + 1):
        if t < len(chunks):
            gu[t] = (_dot(h, wgate_ref[:, chunks[t]]), _dot(h, wup_ref[:, chunks[t]]))
        if t >= 1:
            g, u = gu.pop(t - 1)
            acc = acc + _dot((_silu(g) * u).astype(BF16), wdown_ref[chunks[t - 1], :])
    x2 = x1 + g2_ref[...] * acc
    if final:
        x2 = _rms(x2) * fn_ref[...]
    o_ref[...] = x2


def _out_ffn(srcs, o_f, o_b, gla_in, y_na, y_swa, layer, mods, norm_ffn, gla_norm, avg, w_o,
             w_gate, w_up, w_down, final_norm, skip_ctx, final):
    batch, s, _ = o_f.shape
    d = srcs[0].shape[2]
    d_ff = w_gate.shape[2]
    tm = ROW_BLOCK
    off = 1 if skip_ctx else 0
    nblk = s // tm - off
    tok_off = off
    if skip_ctx and len(srcs) == 2:
        srcs, tok_off = srcs[1:], 0
    row = lambda w, cb=0: pl.BlockSpec((None, tm, w), lambda b, j: (b, j + off, cb))
    out = lambda w: pl.BlockSpec((None, tm, w), lambda b, j: (b, j, 0))
    ms = lambda chunk: _mod_spec(layer, chunk, batch, not skip_ctx)
    return pl.pallas_call(
        functools.partial(_out_ffn_kernel, n_src=len(srcs), final=final),
        grid=(batch, nblk),
        in_specs=_token_specs(srcs, tm, tok_off) + [
            row(GLA_W), row(GLA_W), row(GLA_W, 3), out(NA_W), out(SWA_W),
            ms(2), ms(3), ms(4), ms(5), _layer_spec(layer, (1, d)), _layer_spec(layer, (1, GLA_W)),
            pl.BlockSpec((GLA_W, GLA_W), lambda b, j: (0, 0)),
            _layer_spec(layer, (GLA_W + NA_W + SWA_W, d)),
            _layer_spec(layer, (d, d_ff)), _layer_spec(layer, (d, d_ff)), _layer_spec(layer, (d_ff, d)),
            pl.BlockSpec((1, d), lambda b, j: (0, 0))],
        out_specs=out(d),
        out_shape=jax.ShapeDtypeStruct((batch, nblk * tm, d), F32),
        compiler_params=_params(2),
    )(*srcs, o_f, o_b, gla_in, y_na, y_swa, mods, mods, mods, mods, norm_ffn, gla_norm, avg, w_o,
      w_gate, w_up, w_down, final_norm)


def _rope_tables(ctx_len, seq_len):
    t = np.arange(seq_len)
    n_freq = HEAD_DIM // 4
    inv_freq = ROPE_THETA ** (-np.arange(n_freq) / n_freq)
    ang = np.concatenate([(t // GRID_W)[:, None] * inv_freq, (t % GRID_W)[:, None] * inv_freq], axis=-1)
    cos_h = np.concatenate([np.cos(ang), np.cos(ang)], axis=-1)
    sin_h = np.concatenate([-np.sin(ang), np.sin(ang)], axis=-1)
    cos_t = np.concatenate([np.ones((ctx_len, HEAD_DIM)), cos_h], axis=0)
    sin_t = np.concatenate([np.zeros((ctx_len, HEAD_DIM)), sin_h], axis=0)
    return jnp.asarray(np.tile(cos_t, (1, 2)), F32), jnp.asarray(np.tile(sin_t, (1, 2)), F32)


def _swa_pair_order(a, axis):
    n_pairs = SWA_W // PAIR_W
    heads = [h for p in range(n_pairs) for h in (p, p + n_pairs)]
    return jnp.concatenate([lax.slice_in_dim(a, h * HEAD_DIM, (h + 1) * HEAD_DIM, axis=axis) for h in heads],
                           axis=axis)


def _permute_w_in_kernel(w_ref, o_ref):
    scale = HEAD_DIM ** -0.5
    scale2 = scale * LOG2E
    o = np.cumsum([0, GLA_W, GLA_W, GLA_W, GLA_W, GLA_RANK, GLA_RANK, NA_W, NA_W, NA_W, SWA_W, SWA_KV_W, SWA_KV_W])
    n_pairs = SWA_W // PAIR_W
    swa_heads = [h for p in range(n_pairs) for h in (p, p + n_pairs)]
    moves = [(o[0], o[1], C_GLA, scale), (o[1], o[4], C_GLA + GLA_W, 1.0), (o[4], o[6], C_ZA, 1.0),
             (o[6], o[7], C_NA, scale2), (o[7], o[9], C_NA + NA_W, 1.0), (o[10], o[11], C_SK, 1.0),
             (o[11], o[12], C_SV, 1.0)]
    moves += [(o[9] + h * HEAD_DIM, o[9] + (h + 1) * HEAD_DIM, C_SQ + i * HEAD_DIM, scale2)
              for i, h in enumerate(swa_heads)]
    o_ref[:, C_ZA:IN_PERM_W] = jnp.zeros((o_ref.shape[0], LANES), o_ref.dtype)
    for lo, hi, dst, mult in moves:
        o_ref[:, dst:dst + (hi - lo)] = (w_ref[:, lo:hi] * mult).astype(o_ref.dtype)


def _permute_w_in(w):
    depth, d, n = w.shape
    tm = ROW_BLOCK
    return pl.pallas_call(
        _permute_w_in_kernel,
        grid=(depth, d // tm),
        in_specs=[pl.BlockSpec((None, tm, n), lambda i, k: (i, k, 0))],
        out_specs=pl.BlockSpec((None, tm, IN_PERM_W), lambda i, k: (i, k, 0)),
        out_shape=jax.ShapeDtypeStruct((depth, d, IN_PERM_W), BF16),
        compiler_params=_params(2),
    )(w)


def _na_bias_table(rpb):
    qc = np.arange(GRID_W)
    kc = np.arange(GRID_W)
    win = np.clip(qc - NA_KW // 2, 0, GRID_W - NA_KW)
    valid = (kc[None, :] >= win[:, None]) & (kc[None, :] < win[:, None] + NA_KW)
    dc = kc[None, :] - qc[:, None] + NA_KW - 1
    pick_col = ((dc[None] == np.arange(2 * NA_KW - 1)[:, None, None]) & valid[None]).astype(np.float32)
    t = jnp.einsum('lhde,eck->lhdck', rpb.astype(F32) * LOG2E, pick_col, precision=lax.Precision.HIGHEST)
    t = jnp.where(valid, t, -jnp.inf)
    return jnp.concatenate([t[:, :, :-1], t[:, :, 1:]], axis=-1)


def _pad_rank(wa2, offset):
    return jnp.pad(wa2, ((0, 0), (offset, LANES - offset - GLA_RANK), (0, 0)))


def kernel(x, c, ctx, c_ctx, w_mod, b_mod, norm_mix, norm_ffn, w_in, gla_wa2_f, gla_ba_f, gla_wa2_b, gla_ba_b,
           gla_norm, na_rpb, swa_sink, w_out, w_gate, w_up, w_down, final_norm):
    batch, seq_len, d = x.shape
    ctx_len = ctx.shape[1]
    depth = w_mod.shape[0]
    assert d == D_MODEL and ctx_len == ROW_BLOCK and seq_len % ROW_BLOCK == 0 and batch < 8
    assert seq_len % GRID_W == 0 and seq_len // GRID_W >= 2 * NA_KH

    cc = jnp.concatenate([c, c_ctx[None, :], jnp.zeros((8 - batch - 1, d), F32)], axis=0)
    mods = _modulation(cc, w_mod, b_mod).reshape(depth, 8, 1, 6 * d)
    cos_t, sin_t = _rope_tables(ctx_len, seq_len)
    head_avg = jnp.asarray(np.kron(np.eye(GLA_HEADS), np.full((HEAD_DIM, HEAD_DIM), 1.0 / HEAD_DIM)), BF16)
    w_in_p = _permute_w_in(w_in)
    wa_f, wa_b = _pad_rank(gla_wa2_f, 0), _pad_rank(gla_wa2_b, GLA_RANK)
    ba_f, ba_b = gla_ba_f.reshape(depth, 1, GLA_W), gla_ba_b.reshape(depth, 1, GLA_W)
    na_bias = _na_bias_table(na_rpb)
    w_o = jnp.concatenate([w_out[:, :GLA_W + NA_W], _swa_pair_order(w_out[:, GLA_W + NA_W:], 1)],
                          axis=1).astype(BF16)
    w_gate_b, w_up_b, w_down_b = w_gate.astype(BF16), w_up.astype(BF16), w_down.astype(BF16)
    norm_mix3, norm_ffn3 = norm_mix.reshape(depth, 1, d), norm_ffn.reshape(depth, 1, d)
    gla_norm3 = gla_norm.reshape(depth, 1, GLA_W)

    srcs = (ctx, x)
    for i in range(depth):
        last = i == depth - 1
        gla_in, za, na_qkv, sq, sk, sv = _in_proj(srcs, i, mods, norm_mix3, w_in_p, cos_t, sin_t)
        o_f, o_b, y_na, y_swa = _mixers(gla_in, za, na_qkv, sq, sk, sv, i, wa_f, wa_b, ba_f, ba_b, na_bias,
                                        swa_sink, ctx_len, ctx_out=not last)
        srcs = (_out_ffn(srcs, o_f, o_b, gla_in, y_na, y_swa, i, mods, norm_ffn3, gla_norm3, head_avg,
                         w_o, w_gate_b, w_up_b, w_down_b, final_norm.reshape(1, d),
                         skip_ctx=last, final=last),)
    return srcs[0]
```

```python
import functools

import jax
import jax.numpy as jnp
import numpy as np
from jax import lax
from jax.experimental import pallas as pl
from jax.experimental.pallas import tpu as pltpu

F32 = jnp.float32
BF16 = jnp.bfloat16

D_MODEL = 1024
HEAD_DIM = 64
GRID_W = 64
GLA_HEADS = 4
NA_HEADS = 6
SWA_HEADS = 6
SWA_KV_HEADS = 2
GLA_W = GLA_HEADS * HEAD_DIM
NA_W = NA_HEADS * HEAD_DIM
SWA_W = SWA_HEADS * HEAD_DIM
SWA_KV_W = SWA_KV_HEADS * HEAD_DIM
GLA_RANK = 16
GLA_TAU = 16.0
GLA_CHUNK = 64
NA_KH = 8
NA_KW = 16
SWA_WINDOW = 128
ROPE_THETA = 10000.0
NORM_EPS = 1e-6

LANES = 128
ROW_BLOCK = 256
PAIR_W = 2 * HEAD_DIM
LOG2E = float(np.log2(np.e))
FF_CHUNK = 256
MOD_BLOCK_COLS = 1536
V7X_VMEM_BYTES = 64 * 1024 * 1024
VMEM_LIMIT = V7X_VMEM_BYTES * 7 // 8

C_GLA = 0
C_NA = C_GLA + 4 * GLA_W
C_SV = C_NA + 3 * NA_W
C_SQ = C_SV + SWA_KV_W
C_SK = C_SQ + SWA_W
C_ZA = C_SK + SWA_KV_W
IN_PERM_W = C_ZA + LANES


def _dot(a, b):
    return jnp.dot(a, b, preferred_element_type=F32)


def _dot_nt(a, b):
    return lax.dot_general(a, b, (((1,), (1,)), ((), ())), preferred_element_type=F32)


def _dot_tn(a, b):
    return lax.dot_general(a, b, (((0,), (0,)), ((), ())), preferred_element_type=F32)


def _idiv(a, n):
    assert n & (n - 1) == 0
    return a >> (n.bit_length() - 1)


def _imod(a, n):
    assert n & (n - 1) == 0
    return a & (n - 1)


def _split2(a):
    hi = a.astype(BF16)
    lo = (a - hi.astype(F32)).astype(BF16)
    return hi, lo


def _split3(a):
    hi = a.astype(BF16)
    r = a - hi.astype(F32)
    mid = r.astype(BF16)
    lo = (r - mid.astype(F32)).astype(BF16)
    return hi, mid, lo


def _dot_split(a, b, merged=False):
    ah, al = _split2(a)
    bh, bl = _split2(b)
    if merged:
        return _dot(jnp.concatenate([ah, al, ah], axis=1), jnp.concatenate([bh, bh, bl], axis=0))
    return _dot(ah, bh) + _dot(al, bh) + _dot(ah, bl)


def _silu(a):
    return a * jax.nn.sigmoid(a)


def _rms(x):
    return x * lax.rsqrt(jnp.mean(x * x, axis=-1, keepdims=True) + NORM_EPS)


def _params(n_parallel, n_arbitrary=0):
    return pltpu.CompilerParams(
        dimension_semantics=("parallel",) * n_parallel + ("arbitrary",) * n_arbitrary,
        vmem_limit_bytes=VMEM_LIMIT)


def _mod_kernel(c_ref, w_ref, b_ref, o_ref):
    o_ref[...] = _dot_split(_silu(c_ref[...]), w_ref[...]) + b_ref[...]


def _modulation(cc, w_mod, b_mod):
    depth, d, n = w_mod.shape
    tn = MOD_BLOCK_COLS
    assert n % tn == 0
    return pl.pallas_call(
        _mod_kernel,
        grid=(depth, n // tn),
        in_specs=[
            pl.BlockSpec((8, d), lambda i, k: (0, 0)),
            pl.BlockSpec((None, d, tn), lambda i, k: (i, 0, k)),
            pl.BlockSpec((None, 1, tn), lambda i, k: (i, 0, k)),
        ],
        out_specs=pl.BlockSpec((None, 8, tn), lambda i, k: (i, 0, k)),
        out_shape=jax.ShapeDtypeStruct((depth, 8, n), F32),
        compiler_params=_params(2),
    )(cc, w_mod, b_mod.reshape(depth, 1, n))


def _mod_spec(layer, chunk, batch, ctx_first):
    if ctx_first:
        return pl.BlockSpec((None, None, 1, D_MODEL), lambda b, j: (layer, jnp.where(j == 0, batch, b), 0, chunk))
    return pl.BlockSpec((None, None, 1, D_MODEL), lambda b, j: (layer, b, 0, chunk))


def _layer_spec(layer, shape):
    return pl.BlockSpec((None,) + shape, lambda b, j: (layer,) + (0,) * len(shape))


def _token_rows(src_refs, rows):
    if len(src_refs) == 1:
        return src_refs[0][rows, :]
    ctx_ref, x_ref = src_refs
    return jnp.where(pl.program_id(1) == 0, ctx_ref[rows, :], x_ref[rows, :])


def _token_specs(srcs, tm, off=0):
    d = srcs[0].shape[-1]
    if len(srcs) == 1:
        return [pl.BlockSpec((None, tm, d), lambda b, j: (b, j + off, 0))]
    assert off == 0
    return [pl.BlockSpec((None, tm, d), lambda b, j: (b, 0, 0)),
            pl.BlockSpec((None, tm, d), lambda b, j: (b, jnp.maximum(j - 1, 0), 0))]


def _in_proj_kernel(*refs, n_src):
    src_refs = refs[:n_src]
    sh_ref, sc_ref, nw_ref, w_ref, cos_ref, sin_ref, gla_ref, za_ref, na_ref, sq_ref, sk_ref, sv_ref = refs[n_src:]
    tm = gla_ref.shape[0]
    lane = lax.broadcasted_iota(jnp.int32, (tm // 2, LANES), 1)
    first_half = _imod(lane, HEAD_DIM) < (HEAD_DIM // 2)
    n_q = SWA_W // LANES
    for rows in (slice(0, tm // 2), slice(tm // 2, tm)):
        h = _rms(_token_rows(src_refs, rows)) * nw_ref[...]
        h = (h * (1.0 + sc_ref[...]) + sh_ref[...]).astype(BF16)
        qkz = _dot(h, w_ref[:, C_SQ:IN_PERM_W])
        qk = qkz[:, :C_ZA - C_SQ]
        cos = cos_ref[rows, :]
        sin = sin_ref[rows, :]
        for cb in range(n_q + 1):
            a = qk[:, cb * LANES:(cb + 1) * LANES]
            partner = jnp.where(first_half, pltpu.roll(a, LANES - HEAD_DIM // 2, 1), pltpu.roll(a, HEAD_DIM // 2, 1))
            r = (a * cos + partner * sin).astype(BF16)
            if cb < n_q:
                sq_ref[rows, cb * LANES:(cb + 1) * LANES] = r
            else:
                sk_ref[rows, :] = r
        za_ref[rows, :] = qkz[:, C_ZA - C_SQ:]
        gla_ref[rows, :] = _dot(h, w_ref[:, C_GLA:C_NA])
        nv = _dot(h, w_ref[:, C_NA:C_SQ]).astype(BF16)
        na_ref[rows, :] = nv[:, :C_SV - C_NA]
        sv_ref[rows, :] = nv[:, C_SV - C_NA:]


def _in_proj(srcs, layer, mods, norm_w, w_perm, cos_t, sin_t):
    batch, d = srcs[0].shape[0], srcs[0].shape[2]
    s = sum(a.shape[1] for a in srcs)
    tm = ROW_BLOCK
    row = lambda w: pl.BlockSpec((None, tm, w), lambda b, j: (b, j, 0))
    tab = pl.BlockSpec((tm, LANES), lambda b, j: (j, 0))
    return pl.pallas_call(
        functools.partial(_in_proj_kernel, n_src=len(srcs)),
        grid=(batch, s // tm),
        in_specs=_token_specs(srcs, tm) + [
            _mod_spec(layer, 0, batch, True), _mod_spec(layer, 1, batch, True), _layer_spec(layer, (1, d)),
            _layer_spec(layer, (d, IN_PERM_W)), tab, tab],
        out_specs=[row(4 * GLA_W), row(LANES), row(3 * NA_W), row(SWA_W), row(SWA_KV_W), row(SWA_KV_W)],
        out_shape=[
            jax.ShapeDtypeStruct((batch, s, 4 * GLA_W), F32),
            jax.ShapeDtypeStruct((batch, s, LANES), F32),
            jax.ShapeDtypeStruct((batch, s, 3 * NA_W), BF16),
            jax.ShapeDtypeStruct((batch, s, SWA_W), BF16),
            jax.ShapeDtypeStruct((batch, s, SWA_KV_W), BF16),
            jax.ShapeDtypeStruct((batch, s, SWA_KV_W), BF16),
        ],
        compiler_params=_params(2),
    )(*srcs, mods, mods, norm_w, w_perm, cos_t, sin_t)


def _gla_stages(qf_ref, kf_ref, vf_ref, zf_ref, qb_ref, kb_ref, vb_ref, zb_ref,
                waf_ref, wab_ref, baf_ref, bab_ref, of_ref, ob_ref, stf_ref, stb_ref):
    c = GLA_CHUNK
    w = GLA_W
    n_chunks = ROW_BLOCK // c
    r64 = lax.broadcasted_iota(jnp.int32, (c, 3 * c), 0)
    c64 = _imod(lax.broadcasted_iota(jnp.int32, (c, 3 * c), 1), c)
    row_h = _idiv(lax.broadcasted_iota(jnp.int32, (GLA_HEADS * c, w), 0), c)
    col_h = _idiv(lax.broadcasted_iota(jnp.int32, (GLA_HEADS * c, w), 1), c)
    same_head = row_h == col_h
    ar = lax.broadcasted_iota(jnp.int32, (c, w), 0)
    ac = _imod(lax.broadcasted_iota(jnp.int32, (c, w), 1), c)
    dirs = [
        dict(q=qf_ref, k=kf_ref, v=vf_ref, z=zf_ref, wa=waf_ref, ba=baf_ref, o=of_ref, st=stf_ref, rev=False,
             order=range(n_chunks), cum=(r64 >= c64).astype(BF16), causal=ar >= ac),
        dict(q=qb_ref, k=kb_ref, v=vb_ref, z=zb_ref, wa=wab_ref, ba=bab_ref, o=ob_ref, st=stb_ref, rev=True,
             order=range(n_chunks - 1, -1, -1), cum=(r64 <= c64).astype(BF16), causal=ar <= ac),
    ]
    units = [(d, ci) for d in dirs for ci in d['order']]
    rows = lambda ci: slice(ci * c, (ci + 1) * c)
    t = {}

    def decay():
        log_a = []
        for d in dirs:
            z = _dot_split(d['z'][...], d['wa'][...], merged=True) + d['ba'][...]
            log_a.append((jnp.minimum(z, 0.0) - jnp.log1p(jnp.exp(-jnp.abs(z)))) * (1.0 / GLA_TAU))
        t['b'], t['b_tot'] = [], []
        for (d, ci), la in zip(units, [la for la in log_a for _ in range(n_chunks)]):
            t['b'].append(_dot(d['cum'], jnp.concatenate(_split3(la[rows(ci)]), axis=0)))
            t['b_tot'].append(t['b'][-1][0:1] if d['rev'] else t['b'][-1][c - 1:c])

    def scale():
        b, b_tot = t['b'], t['b_tot']
        k = lambda d, ci: d['k'][rows(ci), :]
        t['v'] = [d['v'][rows(ci), :].astype(BF16) for d, ci in units]
        t['q_in'] = [(d['q'][rows(ci), :] * jnp.exp(bu)).astype(BF16) for (d, ci), bu in zip(units, b)]
        t['k_in'] = [(k(d, ci) * jnp.exp(-bu)).astype(BF16) for (d, ci), bu in zip(units, b)]
        t['k_end'] = [(k(d, ci) * jnp.exp(bt - bu)).astype(BF16) for (d, ci), bu, bt in zip(units, b, b_tot)]
        expand = lambda a: jnp.where(same_head, jnp.concatenate([a] * GLA_HEADS, axis=0), jnp.zeros((), BF16))
        t['kx'] = [expand(ku) for ku in t['k_in']]
        t['vx'] = [expand(vu) for vu in t['v']]

    def intra():
        a = [jnp.where(d['causal'], _dot_nt(qu, kxu), 0.0).astype(BF16)
             for (d, _), qu, kxu in zip(units, t['q_in'], t['kx'])]
        t['o_intra'] = [_dot(au, vxu) for au, vxu in zip(a, t['vx'])]
        t['upd'] = [jnp.where(same_head, _dot_tn(vu, ku), 0.0) for vu, ku in zip(t['v'], t['k_end'])]

    def inter():
        st_enter = []
        for di, d in enumerate(dirs):
            st = d['st'][...]
            for u in range(di * n_chunks, (di + 1) * n_chunks):
                st_enter.append(st.astype(BF16))
                st = st * jnp.exp(t['b_tot'][u]) + t['upd'][u]
            d['st'][...] = st
        for (d, ci), qu, su, ou in zip(units, t['q_in'], st_enter, t['o_intra']):
            d['o'][rows(ci), :] = _dot_nt(qu, su) + ou

    return [decay, scale, intra, inter]


def _stack_pair(q):
    left = lax.broadcasted_iota(jnp.int32, q.shape, 1) < HEAD_DIM
    zero = jnp.zeros((), q.dtype)
    return jnp.concatenate([jnp.where(left, q, zero), jnp.where(left, zero, q)], axis=0)


def _with_ones(v):
    left = lax.broadcasted_iota(jnp.int32, v.shape, 1) < HEAD_DIM
    one = jnp.ones((), v.dtype)
    return jnp.concatenate([jnp.where(left, v, one), jnp.where(left, one, v)], axis=1)


def _normalise(o, extra=None):
    denom = pltpu.roll(o, HEAD_DIM, 1)
    if extra is not None:
        denom = denom + extra
    return o / denom


def _softmax_weights(scores, extra_logit=None):
    m = functools.reduce(jnp.maximum, [jnp.max(s, axis=-1, keepdims=True) for s in scores])
    if extra_logit is not None:
        m = jnp.maximum(m, extra_logit)
    ps = [jnp.exp2(s - m).astype(BF16) for s in scores]
    return ps, (None if extra_logit is None else jnp.exp2(extra_logit - m))


def _na_stages(q_ref, k_ref, v_ref, bias_ref, o_ref, j, ctx_len, grid_rows, is_ctx):
    n_pairs = NA_W // PAIR_W
    rows_per_block = ROW_BLOCK // GRID_W
    n_loc = NA_KH * GRID_W
    unit = 2 * GRID_W
    pair_cols = [slice(p * PAIR_W, (p + 1) * PAIR_W) for p in range(n_pairs)]
    units = [(rr, p) for p in range(n_pairs) for rr in range(rows_per_block)]
    ctx_rows = lambda rr: slice(rr * unit, (rr + 1) * unit)
    t = {}

    def scores():
        qx = {(rr, p): _stack_pair(q_ref[rr * GRID_W:(rr + 1) * GRID_W, pair_cols[p]]) for rr, p in units}
        t['s_ctx'] = [_dot_nt(jnp.concatenate([qx[rr, p] for rr in range(rows_per_block)], axis=0),
                              k_ref[0:ctx_len, pair_cols[p]]) for p in range(n_pairs)]
        if is_ctx:
            return
        t['k0'], bias = [], {}
        for rr in range(rows_per_block):
            r = (j - 1) * rows_per_block + rr
            first_key_row = jnp.clip(r - NA_KH // 2, 0, grid_rows - NA_KH)
            t['k0'].append(pl.multiple_of(ctx_len + first_key_row * GRID_W, GRID_W))
            d0 = first_key_row - r + NA_KH - 1
            for p in range(n_pairs):
                bias[rr, p] = jnp.concatenate(
                    [jnp.concatenate([bias_ref[2 * p + side, d0 + 2 * m] for m in range(NA_KH // 2)], axis=1)
                     for side in (0, 1)], axis=0)
        t['s_loc'] = {(rr, p): _dot_nt(qx[rr, p], k_ref[pl.ds(t['k0'][rr], n_loc), pair_cols[p]]) + bias[rr, p]
                      for rr, p in units}

    def softmax():
        t['w'] = {(rr, p): _softmax_weights(([] if is_ctx else [t['s_loc'][rr, p]]) + [t['s_ctx'][p][ctx_rows(rr)]])
                  for rr, p in units}

    def values():
        w = t['w']
        left = lax.broadcasted_iota(jnp.int32, (GRID_W, PAIR_W), 1) < HEAD_DIM
        for p in range(n_pairs):
            o_ctx = _dot(jnp.concatenate([w[rr, p][0][-1] for rr in range(rows_per_block)], axis=0),
                         _with_ones(v_ref[0:ctx_len, pair_cols[p]]))
            for rr in range(rows_per_block):
                o = o_ctx[ctx_rows(rr)]
                if not is_ctx:
                    o = o + _dot(w[rr, p][0][0], _with_ones(v_ref[pl.ds(t['k0'][rr], n_loc), pair_cols[p]]))
                o_ref[rr * GRID_W:(rr + 1) * GRID_W, pair_cols[p]] = jnp.where(
                    left, _normalise(o[:GRID_W, :PAIR_W]), _normalise(o[GRID_W:, PAIR_W:])).astype(o_ref.dtype)

    return [scores, softmax, values]


def _swa_stages(sink, q_ref, k_ref, v_ref, o_ref, j, ctx_len, seq_len, is_ctx):
    tm = ROW_BLOCK
    n_pairs = SWA_W // PAIR_W
    part_rows = tm if is_ctx else tm // 2
    parts = [slice(r0, r0 + part_rows) for r0 in range(0, tm, part_rows)]
    n_loc = part_rows + 2 * SWA_WINDOW
    pair_cols = [slice(p * PAIR_W, (p + 1) * PAIR_W) for p in range(n_pairs)]
    lane = lax.broadcasted_iota(jnp.int32, (part_rows, PAIR_W), 1)
    zero = jnp.zeros((), BF16)
    units = [(p + side * n_pairs, p, side) for p in range(n_pairs) for side in (0, 1)]
    t = {}

    def scores():
        k_ctx = k_ref[0:ctx_len, :]
        t['start'], in_window = [], []
        if not is_ctx:
            for r in range(len(parts)):
                q0 = (j - 1) * tm + r * part_rows
                k0 = jnp.clip(q0 - SWA_WINDOW, 0, seq_len - n_loc)
                qpos = q0 + lax.broadcasted_iota(jnp.int32, (part_rows, n_loc), 0)
                kpos = k0 + lax.broadcasted_iota(jnp.int32, (part_rows, n_loc), 1)
                in_window.append(jnp.abs(qpos - kpos) <= SWA_WINDOW)
                t['start'].append(pl.multiple_of(ctx_len + k0, SWA_WINDOW))
        t['s'] = {}
        for u, (_, p, side) in enumerate(units):
            for r, rows in enumerate(parts):
                q = jnp.where((lane < HEAD_DIM) == (side == 0), q_ref[rows, pair_cols[p]], zero)
                t['s'][u, r] = [_dot_nt(q, k_ctx)]
                if not is_ctx:
                    k_loc = k_ref[pl.ds(t['start'][r], n_loc), :]
                    t['s'][u, r].insert(0, jnp.where(in_window[r], _dot_nt(q, k_loc), -jnp.inf))

    def softmax():
        t['w'] = {(u, r): _softmax_weights(t['s'][u, r], sink[h] * LOG2E)
                  for u, (h, _, _) in enumerate(units) for r in range(len(parts))}

    def values():
        for r, rows in enumerate(parts):
            vs = [v_ref[0:ctx_len, :]]
            if not is_ctx:
                vs.insert(0, v_ref[pl.ds(t['start'][r], n_loc), :])
            vs = [_with_ones(v) for v in vs]
            for p, cols in enumerate(pair_cols):
                (ps_a, extra_a), (ps_b, extra_b) = t['w'][2 * p, r], t['w'][2 * p + 1, r]
                o = functools.reduce(jnp.add, [_dot(jnp.concatenate([pa, pb], axis=0), v)
                                               for pa, pb, v in zip(ps_a, ps_b, vs)])
                o_ref[rows, cols] = jnp.where(lane < HEAD_DIM, _normalise(o[:part_rows, :PAIR_W], extra_a),
                                              _normalise(o[part_rows:, PAIR_W:], extra_b)).astype(o_ref.dtype)

    return [scores, softmax, values]


def _mixers_kernel(sinks_ref, qf_ref, kf_ref, vf_ref, zf_ref, qb_ref, kb_ref, vb_ref, zb_ref,
                   waf_ref, wab_ref, baf_ref, bab_ref, nq_ref, nk_ref, nv_ref, nbias_ref, sq_ref, sk_ref, sv_ref,
                   of_ref, ob_ref, yna_ref, yswa_ref, stf_ref, stb_ref, *, layer, ctx_len, seq_len, ctx_out):
    j = pl.program_id(1)

    @pl.when(j == 0)
    def _():
        stf_ref[...] = jnp.zeros_like(stf_ref)
        stb_ref[...] = jnp.zeros_like(stb_ref)

    sink = [sinks_ref[layer, h] for h in range(SWA_HEADS)]

    def block(is_ctx):
        gla = _gla_stages(qf_ref, kf_ref, vf_ref, zf_ref, qb_ref, kb_ref, vb_ref, zb_ref,
                          waf_ref, wab_ref, baf_ref, bab_ref, of_ref, ob_ref, stf_ref, stb_ref)
        if is_ctx and not ctx_out:
            order = gla
        else:
            na = _na_stages(nq_ref, nk_ref, nv_ref, nbias_ref, yna_ref, j, ctx_len, seq_len // GRID_W, is_ctx)
            swa = _swa_stages(sink, sq_ref, sk_ref, sv_ref, yswa_ref, j, ctx_len, seq_len, is_ctx)
            order = [gla[0], na[0], swa[0], gla[1], gla[2], na[1], na[2], swa[1], swa[2], gla[3]]
        for stage in order:
            stage()

    pl.when(j == 0)(functools.partial(block, True))
    pl.when(j > 0)(functools.partial(block, False))


def _mixers(gla_in, za, na_qkv, sq, sk, sv, layer, wa_f, wa_b, ba_f, ba_b, na_bias, sinks, ctx_len, ctx_out):
    batch, s, _ = gla_in.shape
    tm = ROW_BLOCK
    nblk = s // tm
    first = 0 if ctx_out else 1
    fwd = lambda j: j
    bwd = lambda j: jnp.where(j == 0, 0, nblk - j)
    soft_out = lambda j: jnp.maximum(j - first, 0)

    def col(order, cb):
        return pl.BlockSpec((None, tm, GLA_W), lambda b, j: (b, order(j), cb))

    def rows(order, w):
        return pl.BlockSpec((None, tm, w), lambda b, j: (b, order(j), 0))

    def full(w, cb=0):
        return pl.BlockSpec((None, s, w), lambda b, j: (b, 0, cb))

    return pl.pallas_call(
        functools.partial(_mixers_kernel, layer=layer, ctx_len=ctx_len, seq_len=s - ctx_len, ctx_out=ctx_out),
        grid=(batch, nblk),
        in_specs=[pl.BlockSpec(memory_space=pltpu.SMEM),
                  col(fwd, 0), col(fwd, 1), col(fwd, 2), rows(fwd, LANES),
                  col(bwd, 0), col(bwd, 1), col(bwd, 2), rows(bwd, LANES),
                  _layer_spec(layer, (LANES, GLA_W)), _layer_spec(layer, (LANES, GLA_W)),
                  _layer_spec(layer, (1, GLA_W)), _layer_spec(layer, (1, GLA_W)),
                  rows(fwd, NA_W), full(NA_W, 1), full(NA_W, 2), _layer_spec(layer, na_bias.shape[1:]),
                  rows(fwd, SWA_W), full(SWA_KV_W), full(SWA_KV_W)],
        out_specs=[col(fwd, 0), col(bwd, 0), rows(soft_out, NA_W), rows(soft_out, SWA_W)],
        out_shape=[jax.ShapeDtypeStruct((batch, s, GLA_W), F32)] * 2
        + [jax.ShapeDtypeStruct((batch, s - first * tm, w), BF16) for w in (NA_W, SWA_W)],
        scratch_shapes=[pltpu.VMEM((GLA_W, GLA_W), F32)] * 2,
        compiler_params=_params(1, 1),
    )(sinks, gla_in, gla_in, gla_in, za, gla_in, gla_in, gla_in, za, wa_f, wa_b, ba_f, ba_b,
      na_qkv, na_qkv, na_qkv, na_bias, sq, sk, sv)


def _out_ffn_kernel(*refs, n_src, final):
    src_refs = refs[:n_src]
    (of_ref, ob_ref, gate_ref, na_ref, swa_ref, g1_ref, sh2_ref, sc2_ref, g2_ref, nffn_ref, gn_ref, avg_ref,
     wo_ref, wgate_ref, wup_ref, wdown_ref, fn_ref, o_ref) = refs[n_src:]
    tm = o_ref.shape[0]
    halves = (slice(0, tm // 2), slice(tm // 2, tm))
    o = [of_ref[rows, :] + ob_ref[rows, :] for rows in halves]
    sq = [_split2(oh * oh) for oh in o]
    ms = [_dot(hi, avg_ref[...]) + _dot(lo, avg_ref[...]) for hi, lo in sq]
    y = [(oh * lax.rsqrt(msh + NORM_EPS) * gn_ref[...] * _silu(gate_ref[rows, :])).astype(BF16)
         for oh, msh, rows in zip(o, ms, halves)]
    mix = [_dot(jnp.concatenate([yh, na_ref[rows, :], swa_ref[rows, :]], axis=1), wo_ref[...])
           for yh, rows in zip(y, halves)]
    x1 = jnp.concatenate([_token_rows(src_refs, rows) + g1_ref[...] * mh for mh, rows in zip(mix, halves)], axis=0)
    h = _rms(x1) * nffn_ref[...]
    h = (h * (1.0 + sc2_ref[...]) + sh2_ref[...]).astype(BF16)

    d_ff = wgate_ref.shape[1]
    chunks = [slice(c0, min(c0 + FF_CHUNK, d_ff)) for c0 in range(0, d_ff, FF_CHUNK)]
    acc = jnp.zeros(x1.shape, F32)
    gu = {}
    for t in range(len(chunks) + 1):
        if t < len(chunks):
            gu[t] = (_dot(h, wgate_ref[:, chunks[t]]), _dot(h, wup_ref[:, chunks[t]]))
        if t >= 1:
            g, u = gu.pop(t - 1)
            acc = acc + _dot((_silu(g) * u).astype(BF16), wdown_ref[chunks[t - 1], :])
    x2 = x1 + g2_ref[...] * acc
    if final:
        x2 = _rms(x2) * fn_ref[...]
    o_ref[...] = x2


def _out_ffn(srcs, o_f, o_b, gla_in, y_na, y_swa, layer, mods, norm_ffn, gla_norm, avg, w_o,
             w_gate, w_up, w_down, final_norm, skip_ctx, final):
    batch, s, _ = o_f.shape
    d = srcs[0].shape[2]
    d_ff = w_gate.shape[2]
    tm = ROW_BLOCK
    off = 1 if skip_ctx else 0
    nblk = s // tm - off
    tok_off = off
    if skip_ctx and len(srcs) == 2:
        srcs, tok_off = srcs[1:], 0
    row = lambda w, cb=0: pl.BlockSpec((None, tm, w), lambda b, j: (b, j + off, cb))
    out = lambda w: pl.BlockSpec((None, tm, w), lambda b, j: (b, j, 0))
    ms = lambda chunk: _mod_spec(layer, chunk, batch, not skip_ctx)
    return pl.pallas_call(
        functools.partial(_out_ffn_kernel, n_src=len(srcs), final=final),
        grid=(batch, nblk),
        in_specs=_token_specs(srcs, tm, tok_off) + [
            row(GLA_W), row(GLA_W), row(GLA_W, 3), out(NA_W), out(SWA_W),
            ms(2), ms(3), ms(4), ms(5), _layer_spec(layer, (1, d)), _layer_spec(layer, (1, GLA_W)),
            pl.BlockSpec((GLA_W, GLA_W), lambda b, j: (0, 0)),
            _layer_spec(layer, (GLA_W + NA_W + SWA_W, d)),
            _layer_spec(layer, (d, d_ff)), _layer_spec(layer, (d, d_ff)), _layer_spec(layer, (d_ff, d)),
            pl.BlockSpec((1, d), lambda b, j: (0, 0))],
        out_specs=out(d),
        out_shape=jax.ShapeDtypeStruct((batch, nblk * tm, d), F32),
        compiler_params=_params(2),
    )(*srcs, o_f, o_b, gla_in, y_na, y_swa, mods, mods, mods, mods, norm_ffn, gla_norm, avg, w_o,
      w_gate, w_up, w_down, final_norm)


def _rope_tables(ctx_len, seq_len):
    t = np.arange(seq_len)
    n_freq = HEAD_DIM // 4
    inv_freq = ROPE_THETA ** (-np.arange(n_freq) / n_freq)
    ang = np.concatenate([(t // GRID_W)[:, None] * inv_freq, (t % GRID_W)[:, None] * inv_freq], axis=-1)
    cos_h = np.concatenate([np.cos(ang), np.cos(ang)], axis=-1)
    sin_h = np.concatenate([-np.sin(ang), np.sin(ang)], axis=-1)
    cos_t = np.concatenate([np.ones((ctx_len, HEAD_DIM)), cos_h], axis=0)
    sin_t = np.concatenate([np.zeros((ctx_len, HEAD_DIM)), sin_h], axis=0)
    return jnp.asarray(np.tile(cos_t, (1, 2)), F32), jnp.asarray(np.tile(sin_t, (1, 2)), F32)


def _swa_pair_order(a, axis):
    n_pairs = SWA_W // PAIR_W
    heads = [h for p in range(n_pairs) for h in (p, p + n_pairs)]
    return jnp.concatenate([lax.slice_in_dim(a, h * HEAD_DIM, (h + 1) * HEAD_DIM, axis=axis) for h in heads],
                           axis=axis)


def _permute_w_in_kernel(w_ref, o_ref):
    scale = HEAD_DIM ** -0.5
    scale2 = scale * LOG2E
    o = np.cumsum([0, GLA_W, GLA_W, GLA_W, GLA_W, GLA_RANK, GLA_RANK, NA_W, NA_W, NA_W, SWA_W, SWA_KV_W, SWA_KV_W])
    n_pairs = SWA_W // PAIR_W
    swa_heads = [h for p in range(n_pairs) for h in (p, p + n_pairs)]
    moves = [(o[0], o[1], C_GLA, scale), (o[1], o[4], C_GLA + GLA_W, 1.0), (o[4], o[6], C_ZA, 1.0),
             (o[6], o[7], C_NA, scale2), (o[7], o[9], C_NA + NA_W, 1.0), (o[10], o[11], C_SK, 1.0),
             (o[11], o[12], C_SV, 1.0)]
    moves += [(o[9] + h * HEAD_DIM, o[9] + (h + 1) * HEAD_DIM, C_SQ + i * HEAD_DIM, scale2)
              for i, h in enumerate(swa_heads)]
    o_ref[:, C_ZA:IN_PERM_W] = jnp.zeros((o_ref.shape[0], LANES), o_ref.dtype)
    for lo, hi, dst, mult in moves:
        o_ref[:, dst:dst + (hi - lo)] = (w_ref[:, lo:hi] * mult).astype(o_ref.dtype)


def _permute_w_in(w):
    depth, d, n = w.shape
    tm = ROW_BLOCK
    return pl.pallas_call(
        _permute_w_in_kernel,
        grid=(depth, d // tm),
        in_specs=[pl.BlockSpec((None, tm, n), lambda i, k: (i, k, 0))],
        out_specs=pl.BlockSpec((None, tm, IN_PERM_W), lambda i, k: (i, k, 0)),
        out_shape=jax.ShapeDtypeStruct((depth, d, IN_PERM_W), BF16),
        compiler_params=_params(2),
    )(w)


def _na_bias_table(rpb):
    qc = np.arange(GRID_W)
    kc = np.arange(GRID_W)
    win = np.clip(qc - NA_KW // 2, 0, GRID_W - NA_KW)
    valid = (kc[None, :] >= win[:, None]) & (kc[None, :] < win[:, None] + NA_KW)
    dc = kc[None, :] - qc[:, None] + NA_KW - 1
    pick_col = ((dc[None] == np.arange(2 * NA_KW - 1)[:, None, None]) & valid[None]).astype(np.float32)
    t = jnp.einsum('lhde,eck->lhdck', rpb.astype(F32) * LOG2E, pick_col, precision=lax.Precision.HIGHEST)
    t = jnp.where(valid, t, -jnp.inf)
    return jnp.concatenate([t[:, :, :-1], t[:, :, 1:]], axis=-1)


def _pad_rank(wa2, offset):
    return jnp.pad(wa2, ((0, 0), (offset, LANES - offset - GLA_RANK), (0, 0)))


def kernel(x, c, ctx, c_ctx, w_mod, b_mod, norm_mix, norm_ffn, w_in, gla_wa2_f, gla_ba_f, gla_wa2_b, gla_ba_b,
           gla_norm, na_rpb, swa_sink, w_out, w_gate, w_up, w_down, final_norm):
    batch, seq_len, d = x.shape
    ctx_len = ctx.shape[1]
    depth = w_mod.shape[0]
    assert d == D_MODEL and ctx_len == ROW_BLOCK and seq_len % ROW_BLOCK == 0 and batch < 8
    assert seq_len % GRID_W == 0 and seq_len // GRID_W >= 2 * NA_KH

    cc = jnp.concatenate([c, c_ctx[None, :], jnp.zeros((8 - batch - 1, d), F32)], axis=0)
    mods = _modulation(cc, w_mod, b_mod).reshape(depth, 8, 1, 6 * d)
    cos_t, sin_t = _rope_tables(ctx_len, seq_len)
    head_avg = jnp.asarray(np.kron(np.eye(GLA_HEADS), np.full((HEAD_DIM, HEAD_DIM), 1.0 / HEAD_DIM)), BF16)
    w_in_p = _permute_w_in(w_in)
    wa_f, wa_b = _pad_rank(gla_wa2_f, 0), _pad_rank(gla_wa2_b, GLA_RANK)
    ba_f, ba_b = gla_ba_f.reshape(depth, 1, GLA_W), gla_ba_b.reshape(depth, 1, GLA_W)
    na_bias = _na_bias_table(na_rpb)
    w_o = jnp.concatenate([w_out[:, :GLA_W + NA_W], _swa_pair_order(w_out[:, GLA_W + NA_W:], 1)],
                          axis=1).astype(BF16)
    w_gate_b, w_up_b, w_down_b = w_gate.astype(BF16), w_up.astype(BF16), w_down.astype(BF16)
    norm_mix3, norm_ffn3 = norm_mix.reshape(depth, 1, d), norm_ffn.reshape(depth, 1, d)
    gla_norm3 = gla_norm.reshape(depth, 1, GLA_W)

    srcs = (ctx, x)
    for i in range(depth):
        last = i == depth - 1
        gla_in, za, na_qkv, sq, sk, sv = _in_proj(srcs, i, mods, norm_mix3, w_in_p, cos_t, sin_t)
        o_f, o_b, y_na, y_swa = _mixers(gla_in, za, na_qkv, sq, sk, sv, i, wa_f, wa_b, ba_f, ba_b, na_bias,
                                        swa_sink, ctx_len, ctx_out=not last)
        srcs = (_out_ffn(srcs, o_f, o_b, gla_in, y_na, y_swa, i, mods, norm_ffn3, gla_norm3, head_avg,
                         w_o, w_gate_b, w_up_b, w_down_b, final_norm.reshape(1, d),
                         skip_ctx=last, final=last),)
    return srcs[0]
```
